```python
import jax, jax.numpy as jnp
from jax import lax
import numpy as np

D_MODEL = 2048
BATCH = 8
SEQ = 4096
DEPTH = 2

CHUNK = 64
EPS = 1e-6
NEG_INF = -1e30

A_HEADS = 16
A_HEAD_DIM = 64
A_WIDTH = A_HEADS * A_HEAD_DIM
A_LEFT_CHUNKS = 8
A_PAD = A_LEFT_CHUNKS * CHUNK
A_BAND = (A_LEFT_CHUNKS + 1) * CHUNK
MAX_REL = 128

B_HEADS = 8
B_NOPE = 128
B_ROPE = 64
B_VDIM = 128
B_Q_LORA = 512
B_KV_LORA = 256
B_WIDTH = B_HEADS * B_VDIM
ROPE_THETA = 10000.0
Q_BLOCK = 128

C_BLOCK = 128
C_GROUPS = 8
C_WIDTH = 1024
C_GROUP_DIM = C_WIDTH // C_GROUPS

N_BRANCH = 3
BRANCH_WIDTH = 1024

A_COLS = 3 * A_WIDTH
B_COLS = B_Q_LORA + B_KV_LORA + B_ROPE
C_COLS = 2 * C_WIDTH
G_COLS = N_BRANCH * D_MODEL
IN_COLS = A_COLS + B_COLS + C_COLS + G_COLS

N_EXPERTS = 64
TOP_K = 8
N_EXPERT_GROUPS = 8
TOPK_GROUPS = 4
D_EXPERT = 512
D_SHARED = 512
ROUTED_SCALE = 2.5

kernel_name = "hybrid_chunk_causal_moe_trunk"


def rms_norm(x, g):
    xf = x.astype(jnp.float32)
    y = xf * lax.rsqrt(jnp.mean(xf * xf, axis=-1, keepdims=True) + EPS)
    return (y * g.astype(jnp.float32)).astype(x.dtype)


def layer_norm(x, g, b):
    xf = x.astype(jnp.float32)
    mu = jnp.mean(xf, axis=-1, keepdims=True)
    var = jnp.mean(jnp.square(xf - mu), axis=-1, keepdims=True)
    y = (xf - mu) * lax.rsqrt(var + EPS)
    return (y * g.astype(jnp.float32) + b.astype(jnp.float32)).astype(x.dtype)


def rope(x, positions):
    half = B_ROPE // 2
    inv = ROPE_THETA ** (-jnp.arange(half, dtype=jnp.float32) / half)
    ang = positions.astype(jnp.float32)[..., None] * inv
    ang = ang.reshape(ang.shape[:2] + (1,) * (x.ndim - 3) + (half,))
    cos, sin = jnp.cos(ang), jnp.sin(ang)
    x1 = x[..., :half].astype(jnp.float32)
    x2 = x[..., half:].astype(jnp.float32)
    out = jnp.concatenate([x1 * cos - x2 * sin, x2 * cos + x1 * sin], axis=-1)
    return out.astype(x.dtype)


def mixer_a(qkv, rel_bias):
    B, S, _ = qkv.shape
    qkv = qkv.reshape(B, S, 3, A_HEADS, A_HEAD_DIM)
    q, k, v = qkv[:, :, 0], qkv[:, :, 1], qkv[:, :, 2]
    kp = jnp.pad(k, ((0, 0), (A_PAD, 0), (0, 0), (0, 0)))
    vp = jnp.pad(v, ((0, 0), (A_PAD, 0), (0, 0), (0, 0)))
    i = jnp.arange(CHUNK)[:, None]
    j = jnp.arange(A_BAND)[None, :] - A_PAD
    rel_idx = jnp.clip(i - j, -MAX_REL, MAX_REL) + MAX_REL
    bias = rel_bias[:, rel_idx].astype(jnp.float32)
    scale = A_HEAD_DIM ** -0.5

    def one_chunk(ci):
        start = ci * CHUNK
        qb = lax.dynamic_slice_in_dim(q, start, CHUNK, axis=1)
        kb = lax.dynamic_slice_in_dim(kp, start, A_BAND, axis=1)
        vb = lax.dynamic_slice_in_dim(vp, start, A_BAND, axis=1)
        s = jnp.einsum('bqhd,bkhd->bhqk', qb, kb).astype(jnp.float32) * scale + bias
        valid = (start - A_PAD + jnp.arange(A_BAND)) >= 0
        s = jnp.where(valid[None, None, None, :], s, NEG_INF)
        p = jax.nn.softmax(s, axis=-1).astype(vb.dtype)
        return jnp.einsum('bhqk,bkhd->bqhd', p, vb)

    out = lax.map(one_chunk, jnp.arange(S // CHUNK))
    return jnp.moveaxis(out, 0, 1).reshape(B, S, A_WIDTH)


def mixer_b(bcols, positions, q_norm, w_uq, kv_norm, w_ukv):
    B, S, _ = bcols.shape
    c_q = bcols[..., :B_Q_LORA]
    c_kv = bcols[..., B_Q_LORA:B_Q_LORA + B_KV_LORA]
    k_pe = rope(bcols[..., B_Q_LORA + B_KV_LORA:], positions)
    q = (rms_norm(c_q, q_norm) @ w_uq).reshape(B, S, B_HEADS, B_NOPE + B_ROPE)
    q_nope = q[..., :B_NOPE]
    q_pe = rope(q[..., B_NOPE:], positions)
    kv = (rms_norm(c_kv, kv_norm) @ w_ukv).reshape(B, S, B_HEADS, B_NOPE + B_VDIM)
    k_nope, v = kv[..., :B_NOPE], kv[..., B_NOPE:]
    scale = (B_NOPE + B_ROPE) ** -0.5
    k_chunk = jnp.arange(S) // CHUNK

    def one_block(bi):
        qs = bi * Q_BLOCK
        qn = lax.dynamic_slice_in_dim(q_nope, qs, Q_BLOCK, axis=1)
        qp = lax.dynamic_slice_in_dim(q_pe, qs, Q_BLOCK, axis=1)
        s = (jnp.einsum('bqhd,bkhd->bhqk', qn, k_nope)
             + jnp.einsum('bqhr,bkr->bhqk', qp, k_pe)).astype(jnp.float32) * scale
        q_chunk = (qs + jnp.arange(Q_BLOCK)) // CHUNK
        s = jnp.where(k_chunk[None, :] <= q_chunk[:, None], s, NEG_INF)
        p = jax.nn.softmax(s, axis=-1).astype(v.dtype)
        return jnp.einsum('bhqk,bkhd->bqhd', p, v)

    out = lax.map(one_block, jnp.arange(S // Q_BLOCK))
    return jnp.moveaxis(out, 0, 1).reshape(B, S, B_WIDTH)


def mixer_c(ccols, ln_g, ln_b, w_s, b_s):
    B, S, _ = ccols.shape
    z = jax.nn.gelu(ccols)
    u, v = z[..., :C_WIDTH], z[..., C_WIDTH:]
    v = layer_norm(v, ln_g, ln_b)
    v = v.reshape(B, S // C_BLOCK, C_BLOCK, C_GROUPS, C_GROUP_DIM)
    causal = jnp.tril(jnp.ones((C_BLOCK, C_BLOCK), dtype=w_s.dtype))
    mixed = jnp.einsum('gts,bnsgc->bntgc', w_s * causal, v) + b_s.T[:, :, None]
    return u * mixed.reshape(B, S, C_WIDTH)


def mixer_block(h, positions, rel_bias, w_in, q_norm, w_uq, kv_norm, w_ukv,
                ln_g, ln_b, w_s, b_s, w_branch, w_out):
    B, S, _ = h.shape
    proj = h @ w_in
    o1 = A_COLS
    o2 = o1 + B_COLS
    o3 = o2 + C_COLS
    y_a = mixer_a(proj[..., :o1], rel_bias)
    y_b = mixer_b(proj[..., o1:o2], positions, q_norm, w_uq, kv_norm, w_ukv)
    y_c = mixer_c(proj[..., o2:o3], ln_g, ln_b, w_s, b_s)
    gates = jax.nn.sigmoid(proj[..., o3:]).reshape(B, S, N_BRANCH, D_MODEL)
    ys = jnp.stack([y_a, y_b, y_c], axis=2)
    branches = jnp.einsum('bsnw,nwd->bsnd', ys, w_branch)
    merged = jnp.sum(gates * branches, axis=2)
    return merged @ w_out


def moe(h, w_router, router_bias, we_gate, we_up, we_down, ws_gate, ws_up, ws_down):
    B, S, D = h.shape
    t = h.reshape(B * S, D)
    T = t.shape[0]
    scores = jax.nn.sigmoid((t @ w_router).astype(jnp.float32))
    sel = scores + router_bias.astype(jnp.float32)
    per_group = N_EXPERTS // N_EXPERT_GROUPS
    group_score = jnp.sum(lax.top_k(sel.reshape(T, N_EXPERT_GROUPS, per_group), 2)[0], axis=-1)
    _, gidx = lax.top_k(group_score, TOPK_GROUPS)
    gmask = jnp.any(gidx[:, :, None] == jnp.arange(N_EXPERT_GROUPS)[None, None, :], axis=1)
    emask = jnp.repeat(gmask, per_group, axis=1)
    _, eidx = lax.top_k(jnp.where(emask, sel, -jnp.inf), TOP_K)
    w = jnp.take_along_axis(scores, eidx, axis=-1)
    w = w / (jnp.sum(w, axis=-1, keepdims=True) + 1e-20) * ROUTED_SCALE
    combine = jnp.zeros((T, N_EXPERTS), jnp.float32).at[jnp.arange(T)[:, None], eidx].add(w)

    shared = (jax.nn.silu(t @ ws_gate) * (t @ ws_up)) @ ws_down

    def expert_step(acc, xs):
        wg, wu, wd, cw = xs
        y = (jax.nn.silu(t @ wg) * (t @ wu)) @ wd
        return acc + cw[:, None] * y, None

    out, _ = lax.scan(expert_step, shared,
                      (we_gate, we_up, we_down, combine.T.astype(t.dtype)))
    return out.reshape(B, S, D)


def setup_inputs(seed: int = 0) -> dict:
    key = jax.random.key(seed)
    ks = jax.random.split(key, 32)
    L, D = DEPTH, D_MODEL

    def nrm(k, shape, scale):
        return jax.random.normal(k, shape, jnp.float32) * scale

    def gain(k, shape):
        return 1.0 + 0.05 * jax.random.normal(k, shape, jnp.float32)

    offsets = jax.random.randint(ks[2], (BATCH, 1), 0, 1000) * CHUNK
    positions = (offsets + jnp.arange(SEQ, dtype=jnp.int32)[None, :]).astype(jnp.int32)
    return {
        "x": nrm(ks[0], (BATCH, SEQ, D), 1.0),
        "c": nrm(ks[1], (BATCH, D), 1.0),
        "positions": positions,
        "rel_bias": nrm(ks[3], (A_HEADS, 2 * MAX_REL + 1), 0.5),
        "norm_mix": gain(ks[4], (L, D)),
        "norm_ffn": gain(ks[5], (L, D)),
        "w_mod": nrm(ks[6], (L, D, 6 * D), 0.5 * D ** -0.5),
        "b_mod": nrm(ks[7], (L, 6 * D), 0.02),
        "w_in": nrm(ks[8], (L, D, IN_COLS), D ** -0.5),
        "mla_q_norm": gain(ks[9], (L, B_Q_LORA)),
        "mla_w_uq": nrm(ks[10], (L, B_Q_LORA, B_HEADS * (B_NOPE + B_ROPE)), B_Q_LORA ** -0.5),
        "mla_kv_norm": gain(ks[11], (L, B_KV_LORA)),
        "mla_w_ukv": nrm(ks[12], (L, B_KV_LORA, B_HEADS * (B_NOPE + B_VDIM)), B_KV_LORA ** -0.5),
        "gmlp_ln_g": gain(ks[13], (L, C_WIDTH)),
        "gmlp_ln_b": nrm(ks[14], (L, C_WIDTH), 0.02),
        "gmlp_w_s": nrm(ks[15], (L, C_GROUPS, C_BLOCK, C_BLOCK), C_BLOCK ** -0.5),
        "gmlp_b_s": gain(ks[16], (L, C_GROUPS, C_BLOCK)),
        "w_branch": nrm(ks[17], (L, N_BRANCH, BRANCH_WIDTH, D), BRANCH_WIDTH ** -0.5),
        "w_out": nrm(ks[18], (L, D, D), D ** -0.5),
        "w_router": nrm(ks[19], (L, D, N_EXPERTS), D ** -0.5),
        "router_bias": nrm(ks[20], (L, N_EXPERTS), 0.01),
        "we_gate": nrm(ks[21], (L, N_EXPERTS, D, D_EXPERT), D ** -0.5),
        "we_up": nrm(ks[22], (L, N_EXPERTS, D, D_EXPERT), D ** -0.5),
        "we_down": nrm(ks[23], (L, N_EXPERTS, D_EXPERT, D), D_EXPERT ** -0.5),
        "ws_gate": nrm(ks[24], (L, D, D_SHARED), D ** -0.5),
        "ws_up": nrm(ks[25], (L, D, D_SHARED), D ** -0.5),
        "ws_down": nrm(ks[26], (L, D_SHARED, D), D_SHARED ** -0.5),
        "final_norm": gain(ks[27], (D,)),
    }


def reference(x, c, positions, rel_bias, norm_mix, norm_ffn, w_mod, b_mod, w_in,
              mla_q_norm, mla_w_uq, mla_kv_norm, mla_w_ukv,
              gmlp_ln_g, gmlp_ln_b, gmlp_w_s, gmlp_b_s, w_branch, w_out,
              w_router, router_bias, we_gate, we_up, we_down,
              ws_gate, ws_up, ws_down, final_norm):
    c_act = jax.nn.silu(c)
    for l in range(DEPTH):
        mod = (c_act @ w_mod[l] + b_mod[l])[:, None, :]
        sh1, sc1, g1, sh2, sc2, g2 = jnp.split(mod, 6, axis=-1)
        h = rms_norm(x, norm_mix[l]) * (1 + sc1) + sh1
        x = x + g1 * mixer_block(h, positions, rel_bias, w_in[l],
                                 mla_q_norm[l], mla_w_uq[l], mla_kv_norm[l], mla_w_ukv[l],
                                 gmlp_ln_g[l], gmlp_ln_b[l], gmlp_w_s[l], gmlp_b_s[l],
                                 w_branch[l], w_out[l])
        h = rms_norm(x, norm_ffn[l]) * (1 + sc2) + sh2
        x = x + g2 * moe(h, w_router[l], router_bias[l], we_gate[l], we_up[l], we_down[l],
                         ws_gate[l], ws_up[l], ws_down[l])
    return rms_norm(x, final_norm)
```

```python
import functools

import jax
import jax.numpy as jnp
from jax import lax
from jax.experimental import pallas as pl
from jax.experimental.pallas import tpu as pltpu

F32 = jnp.float32
BF16 = jnp.bfloat16
U32 = jnp.uint32
I32 = jnp.int32

EPS = 1e-6
NEG_INF = -1e30

D_MODEL = 2048
HALF_D = D_MODEL // 2
CHUNK = 64

A_HEADS = 16
A_HEAD_DIM = 64
A_WIDTH = A_HEADS * A_HEAD_DIM
A_LEFT_CHUNKS = 8
MAX_REL = 128
A_QBLOCK = 256
A_KBLOCKS = 3

B_HEADS = 8
B_NOPE = 128
B_ROPE = 64
B_VDIM = 128
B_Q_LORA = 512
B_KV_LORA = 256
B_QK_PAD = 256
ROPE_THETA = 10000.0
B_TQ = 512

C_BLOCK = 128
C_GROUPS = 8
C_WIDTH = 1024

N_EXPERTS = 64
EXPERT_GROUP = 8
TOP_K = 8
TOPK_GROUPS = 4
D_EXPERT = 512
ROUTED_SCALE = 2.5
EXPERT_TILE = 256
COMBINE_TILE = 128
GATHER_UNROLL = 8

LANE = 128
V7X_VMEM_LIMIT = 56 * 1024 * 1024


def _params(*sem):
    return pltpu.CompilerParams(dimension_semantics=sem, vmem_limit_bytes=V7X_VMEM_LIMIT)


def _pack_halves(v):
    w = v.shape[1] // 2
    bits = lax.bitcast_convert_type(v.astype(BF16).astype(F32), U32)
    return (bits[:, :w] >> 16) | bits[:, w:]


def _unpack_halves(p):
    lo = lax.bitcast_convert_type(p << 16, F32)
    hi = lax.bitcast_convert_type(p & jnp.uint32(0xFFFF0000), F32)
    return lo, hi


def _mod_kernel(c_ref, w_ref, b_ref, o_ref):
    c = c_ref[...]
    ca = (c * jax.nn.sigmoid(c)).astype(BF16)
    o_ref[...] = jnp.dot(ca, w_ref[...].astype(BF16), preferred_element_type=F32) + b_ref[...]


def _modulation(c, w_mod, b_mod):
    L, D, N = w_mod.shape
    B = c.shape[0]
    tn = 512
    return pl.pallas_call(
        _mod_kernel,
        grid=(L, N // tn),
        in_specs=[
            pl.BlockSpec((B, D), lambda l, j: (0, 0)),
            pl.BlockSpec((None, D, tn), lambda l, j: (l, 0, j)),
            pl.BlockSpec((None, 1, tn), lambda l, j: (l, 0, j)),
        ],
        out_specs=pl.BlockSpec((None, B, tn), lambda l, j: (l, 0, j)),
        out_shape=jax.ShapeDtypeStruct((L, B, N), F32),
        compiler_params=_params("parallel", "arbitrary"),
        name="modulation",
    )(c, w_mod, b_mod.reshape(L, 1, N))


def _modulated_norm(x, g, sc, sh):
    r = lax.rsqrt(jnp.mean(x * x, axis=-1, keepdims=True) + EPS)
    return (x * r * g) * (1.0 + sc) + sh


def _norm_mod_kernel(x_ref, g_ref, sc_ref, sh_ref, o_ref):
    o_ref[...] = _modulated_norm(x_ref[...], g_ref[...], sc_ref[...], sh_ref[...]).astype(o_ref.dtype)


def _norm_mod(x, g, mod3, row_sc, row_sh, S, tm=512):
    T, D = x.shape
    per = S // tm
    return pl.pallas_call(
        _norm_mod_kernel,
        grid=(T // tm,),
        in_specs=[
            pl.BlockSpec((tm, D), lambda i: (i, 0)),
            pl.BlockSpec((1, D), lambda i: (0, 0)),
            pl.BlockSpec((None, 1, D), lambda i: (row_sc(i // per), 0, 0)),
            pl.BlockSpec((None, 1, D), lambda i: (row_sh(i // per), 0, 0)),
        ],
        out_specs=pl.BlockSpec((tm, D), lambda i: (i, 0)),
        out_shape=jax.ShapeDtypeStruct((T, D), BF16),
        compiler_params=_params("parallel"),
        name="norm_mod",
    )(x, g.reshape(1, D), mod3, mod3)


def _mm_kernel(a_ref, w_ref, o_ref, *, act):
    acc = jnp.dot(a_ref[...], w_ref[...], preferred_element_type=F32)
    if act == "gelu":
        acc = jax.nn.gelu(acc)
    elif act == "sigmoid":
        acc = jax.nn.sigmoid(acc)
    o_ref[...] = acc.astype(o_ref.dtype)


def _matmul(a, w, out_dtype, act=None, tm=1024, tn=512, name="matmul"):
    M, K = a.shape
    N = w.shape[1]
    tm = min(tm, M)
    return pl.pallas_call(
        functools.partial(_mm_kernel, act=act),
        grid=(M // tm, N // tn),
        in_specs=[
            pl.BlockSpec((tm, K), lambda i, j: (i, 0)),
            pl.BlockSpec((K, tn), lambda i, j: (0, j)),
        ],
        out_specs=pl.BlockSpec((tm, tn), lambda i, j: (i, j)),
        out_shape=jax.ShapeDtypeStruct((M, N), out_dtype),
        compiler_params=_params("parallel", "arbitrary"),
        name=name,
    )(a, w)


def _mm_residual_kernel(a_ref, w_ref, x_ref, g_ref, o_ref):
    acc = jnp.dot(a_ref[...], w_ref[...], preferred_element_type=F32)
    o_ref[...] = x_ref[...] + g_ref[...] * acc


def _matmul_residual(a, w, x, mod3, row_g, S, tm=1024, tn=512):
    M, K = a.shape
    N = w.shape[1]
    tm = min(tm, S)
    per = S // tm
    return pl.pallas_call(
        _mm_residual_kernel,
        grid=(M // tm, N // tn),
        in_specs=[
            pl.BlockSpec((tm, K), lambda i, j: (i, 0)),
            pl.BlockSpec((K, tn), lambda i, j: (0, j)),
            pl.BlockSpec((tm, tn), lambda i, j: (i, j)),
            pl.BlockSpec((None, 1, tn), lambda i, j: (row_g(i // per), 0, j)),
        ],
        out_specs=pl.BlockSpec((tm, tn), lambda i, j: (i, j)),
        out_shape=jax.ShapeDtypeStruct((M, N), F32),
        compiler_params=_params("parallel", "arbitrary"),
        name="out_proj_residual",
    )(a, w, x, mod3)


def _mixer_a_kernel(q_ref, k0_ref, k1_ref, k2_ref, v0_ref, v1_ref, v2_ref, bias_ref, o_ref, *, scale):
    i = pl.program_id(2)
    qb = A_QBLOCK
    kw = A_KBLOCKS * qb
    q = q_ref[...]
    k = jnp.concatenate([k0_ref[...], k1_ref[...], k2_ref[...]], axis=0)
    v = jnp.concatenate([v0_ref[...], v1_ref[...], v2_ref[...]], axis=0)
    lane = lax.broadcasted_iota(I32, (qb, LANE), 1)
    kpos = (i - (A_KBLOCKS - 1)) * qb + lax.broadcasted_iota(I32, (1, kw), 1)
    in_seq = kpos >= 0
    outs = []
    for hh in range(2):
        head_lanes = (lane < A_HEAD_DIM) if hh == 0 else (lane >= A_HEAD_DIM)
        qm = jnp.where(head_lanes, q, jnp.zeros_like(q))
        s = lax.dot_general(qm, k, (((1,), (1,)), ((), ())), preferred_element_type=F32)
        s = s * scale + bias_ref[hh]
        s = jnp.where(in_seq, s, NEG_INF)
        m = jnp.max(s, axis=-1, keepdims=True)
        p = jnp.exp(s - m)
        l = jnp.sum(p, axis=-1, keepdims=True)
        o = jnp.dot(p.astype(BF16), v, preferred_element_type=F32)
        outs.append(o / l)
    o_ref[...] = jnp.where(lane < A_HEAD_DIM, outs[0], outs[1]).astype(o_ref.dtype)


def _mixer_a_bias(rel_bias):
    qb = A_QBLOCK
    kw = A_KBLOCKS * qb
    qi = jnp.arange(qb)[:, None]
    kj = jnp.arange(kw)[None, :]
    d = qi + (A_KBLOCKS - 1) * qb - kj
    qc = qi // CHUNK
    kc = kj // CHUNK
    first = (A_KBLOCKS - 1) * qb // CHUNK - A_LEFT_CHUNKS
    in_band = (kc >= qc + first) & (kc <= qc + first + A_LEFT_CHUNKS)
    idx = jnp.clip(d, -MAX_REL, MAX_REL) + MAX_REL
    bias = rel_bias[:, idx].astype(F32)
    return jnp.where(in_band[None], bias, NEG_INF)


def _mixer_a(qkv, bias, B, S):
    T = qkv.shape[0]
    qb = A_QBLOCK
    nq = S // qb
    pairs = A_HEADS // 2

    def kv_spec(j, col0):
        return pl.BlockSpec(
            (qb, LANE), lambda p, b, i: (b * nq + jnp.maximum(i - (A_KBLOCKS - 1) + j, 0), col0 + p))

    return pl.pallas_call(
        functools.partial(_mixer_a_kernel, scale=A_HEAD_DIM ** -0.5),
        grid=(pairs, B, nq),
        in_specs=[
            pl.BlockSpec((qb, LANE), lambda p, b, i: (b * nq + i, p)),
            kv_spec(0, pairs), kv_spec(1, pairs), kv_spec(2, pairs),
            kv_spec(0, 2 * pairs), kv_spec(1, 2 * pairs), kv_spec(2, 2 * pairs),
            pl.BlockSpec((2, qb, A_KBLOCKS * qb), lambda p, b, i: (p, 0, 0)),
        ],
        out_specs=pl.BlockSpec((qb, LANE), lambda p, b, i: (b * nq + i, p)),
        out_shape=jax.ShapeDtypeStruct((T, A_WIDTH), BF16),
        compiler_params=_params("parallel", "parallel", "arbitrary"),
        name="mixer_a",
    )(qkv, qkv, qkv, qkv, qkv, qkv, qkv, bias)


def _rope_table_kernel(pos_ref, inv_ref, c_ref, s_ref):
    ang = pos_ref[...].astype(F32) * inv_ref[...]
    lane = lax.broadcasted_iota(I32, ang.shape, 1)
    first_half = (lane % B_ROPE) < (B_ROPE // 2)
    c_ref[...] = jnp.cos(ang)
    sn = jnp.sin(ang)
    s_ref[...] = jnp.where(first_half, -sn, sn)


def _rope_tables(positions, tm=512):
    T = positions.size
    half = B_ROPE // 2
    inv = ROPE_THETA ** (-jnp.arange(half, dtype=F32) / half)
    inv = jnp.tile(inv, LANE // half).reshape(1, LANE)
    return pl.pallas_call(
        _rope_table_kernel,
        grid=(T // tm,),
        in_specs=[pl.BlockSpec((tm, 1), lambda i: (i, 0)), pl.BlockSpec((1, LANE), lambda i: (0, 0))],
        out_specs=[pl.BlockSpec((tm, LANE), lambda i: (i, 0))] * 2,
        out_shape=[jax.ShapeDtypeStruct((T, LANE), F32)] * 2,
        compiler_params=_params("parallel"),
        name="rope_tables",
    )(positions.reshape(T, 1), inv)


def _mla_prep_kernel(bp_ref, c_ref, s_ref, qn_ref, kvn_ref, wq_ref, wkv_ref, q_out, k_out, v_out, *, scale):
    bp = bp_ref[...]
    cq = bp[:, :B_Q_LORA]
    ckv = bp[:, B_Q_LORA:B_Q_LORA + B_KV_LORA]
    kpe = bp[:, B_Q_LORA + B_KV_LORA:B_Q_LORA + B_KV_LORA + LANE]
    kpe_sw = bp[:, B_Q_LORA + B_KV_LORA + LANE:]
    cqn = cq * lax.rsqrt(jnp.mean(cq * cq, axis=-1, keepdims=True) + EPS) * qn_ref[...]
    ckvn = ckv * lax.rsqrt(jnp.mean(ckv * ckv, axis=-1, keepdims=True) + EPS) * kvn_ref[...]
    qa = jnp.dot(cqn.astype(BF16), wq_ref[...], preferred_element_type=F32)
    kv = jnp.dot(ckvn.astype(BF16), wkv_ref[...], preferred_element_type=F32)
    cos = c_ref[...]
    sin = s_ref[...]
    k_rot = (kpe * cos + kpe_sw * sin).astype(BF16)
    sw0 = B_HEADS * B_QK_PAD
    for h in range(B_HEADS):
        c0 = h * B_QK_PAD
        q_out[:, c0:c0 + B_NOPE] = (qa[:, c0:c0 + B_NOPE] * scale).astype(BF16)
        pe = qa[:, c0 + B_NOPE:c0 + B_QK_PAD]
        pe_sw = qa[:, sw0 + h * LANE:sw0 + (h + 1) * LANE]
        q_out[:, c0 + B_NOPE:c0 + B_QK_PAD] = ((pe * cos + pe_sw * sin) * scale).astype(BF16)
        k_out[:, c0:c0 + B_NOPE] = kv[:, h * B_NOPE:(h + 1) * B_NOPE].astype(BF16)
        k_out[:, c0 + B_NOPE:c0 + B_QK_PAD] = k_rot
    v_out[...] = kv[:, B_HEADS * B_NOPE:].astype(BF16)


def _mla_prep(bproj, cos, sin, q_norm, kv_norm, wq, wkv, tm=256):
    T = bproj.shape[0]
    scale = (B_NOPE + B_ROPE) ** -0.5
    qk_w = B_HEADS * B_QK_PAD
    v_w = B_HEADS * B_VDIM
    row = lambda i: (i, 0)
    fixed = lambda i: (0, 0)
    return pl.pallas_call(
        functools.partial(_mla_prep_kernel, scale=scale),
        grid=(T // tm,),
        in_specs=[
            pl.BlockSpec((tm, bproj.shape[1]), row),
            pl.BlockSpec((tm, LANE), row),
            pl.BlockSpec((tm, LANE), row),
            pl.BlockSpec((1, B_Q_LORA), fixed),
            pl.BlockSpec((1, B_KV_LORA), fixed),
            pl.BlockSpec(wq.shape, fixed),
            pl.BlockSpec(wkv.shape, fixed),
        ],
        out_specs=[pl.BlockSpec((tm, qk_w), row), pl.BlockSpec((tm, qk_w), row), pl.BlockSpec((tm, v_w), row)],
        out_shape=[jax.ShapeDtypeStruct((T, qk_w), BF16), jax.ShapeDtypeStruct((T, qk_w), BF16),
                   jax.ShapeDtypeStruct((T, v_w), BF16)],
        compiler_params=_params("parallel"),
        name="mla_prep",
    )(bproj, cos, sin, q_norm.reshape(1, -1), kv_norm.reshape(1, -1), wq, wkv)


def _mla_attn_kernel(qt_ref, kt_ref, q_ref, k_ref, v_ref, o_ref, m_sc, l_sc, acc_sc):
    p = pl.program_id(2)
    qi = qt_ref[p]
    kj = kt_ref[p]

    @pl.when(kj == 0)
    def _():
        m_sc[...] = jnp.full_like(m_sc, NEG_INF)
        l_sc[...] = jnp.zeros_like(l_sc)
        acc_sc[...] = jnp.zeros_like(acc_sc)

    def online_softmax_step(s):
        m_prev = m_sc[...]
        m_new = jnp.maximum(m_prev, jnp.max(s, axis=-1, keepdims=True))
        alpha = jnp.exp(m_prev - m_new)
        pr = jnp.exp(s - m_new)
        l_sc[...] = alpha * l_sc[...] + jnp.sum(pr, axis=-1, keepdims=True)
        acc_sc[...] = alpha * acc_sc[...] + jnp.dot(pr.astype(BF16), v_ref[...], preferred_element_type=F32)
        m_sc[...] = m_new

    def scores():
        return lax.dot_general(q_ref[...], k_ref[...], (((1,), (1,)), ((), ())), preferred_element_type=F32)

    @pl.when(kj < qi)
    def _():
        online_softmax_step(scores())

    @pl.when(kj == qi)
    def _():
        s = scores()
        qc = lax.broadcasted_iota(I32, s.shape, 0) // CHUNK
        kc = lax.broadcasted_iota(I32, s.shape, 1) // CHUNK
        online_softmax_step(jnp.where(kc <= qc, s, NEG_INF))
        o_ref[...] = (acc_sc[...] / l_sc[...]).astype(o_ref.dtype)


def _mla_attention(qp, kp, v, B, S):
    T = qp.shape[0]
    tq = min(B_TQ, S)
    nq = S // tq
    pairs = [(qi, kj) for qi in range(nq) for kj in range(qi + 1)]
    qt = jnp.asarray([p[0] for p in pairs], I32)
    kt = jnp.asarray([p[1] for p in pairs], I32)
    grid_spec = pltpu.PrefetchScalarGridSpec(
        num_scalar_prefetch=2,
        grid=(B, B_HEADS, len(pairs)),
        in_specs=[
            pl.BlockSpec((tq, B_QK_PAD), lambda b, h, p, qt, kt: (b * nq + qt[p], h)),
            pl.BlockSpec((tq, B_QK_PAD), lambda b, h, p, qt, kt: (b * nq + kt[p], h)),
            pl.BlockSpec((tq, B_VDIM), lambda b, h, p, qt, kt: (b * nq + kt[p], h)),
        ],
        out_specs=pl.BlockSpec((tq, B_VDIM), lambda b, h, p, qt, kt: (b * nq + qt[p], h)),
        scratch_shapes=[pltpu.VMEM((tq, 1), F32), pltpu.VMEM((tq, 1), F32), pltpu.VMEM((tq, B_VDIM), F32)],
    )
    return pl.pallas_call(
        _mla_attn_kernel,
        grid_spec=grid_spec,
        out_shape=jax.ShapeDtypeStruct((T, B_HEADS * B_VDIM), BF16),
        compiler_params=_params("parallel", "parallel", "arbitrary"),
        name="mla_attention",
    )(qt, kt, qp, kp, v)


def _mixer_c_kernel(z_ref, lg_ref, lb_ref, ws_ref, bs_ref, o_ref):
    tm = z_ref.shape[0]
    u = z_ref[:, :C_WIDTH].astype(F32)
    v = z_ref[:, C_WIDTH:].astype(F32)
    mu = jnp.mean(v, axis=-1, keepdims=True)
    var = jnp.mean(jnp.square(v - mu), axis=-1, keepdims=True)
    vn = ((v - mu) * lax.rsqrt(var + EPS) * lg_ref[...] + lb_ref[...]).astype(BF16)
    row = lax.broadcasted_iota(I32, (C_BLOCK, C_BLOCK), 0)
    col = lax.broadcasted_iota(I32, (C_BLOCK, C_BLOCK), 1)
    causal = col <= row
    gd = C_WIDTH // C_GROUPS
    for g in range(C_GROUPS):
        w = jnp.where(causal, ws_ref[g], 0.0).astype(BF16)
        b = bs_ref[:, g:g + 1]
        for n in range(tm // C_BLOCK):
            rows = slice(n * C_BLOCK, (n + 1) * C_BLOCK)
            cols = slice(g * gd, (g + 1) * gd)
            mixed = jnp.dot(w, vn[rows, cols], preferred_element_type=F32) + b
            o_ref[rows, cols] = (u[rows, cols] * mixed).astype(o_ref.dtype)


def _mixer_c(z, ln_g, ln_b, w_s, b_s, tm=256):
    T = z.shape[0]
    return pl.pallas_call(
        _mixer_c_kernel,
        grid=(T // tm,),
        in_specs=[
            pl.BlockSpec((tm, 2 * C_WIDTH), lambda i: (i, 0)),
            pl.BlockSpec((1, C_WIDTH), lambda i: (0, 0)),
            pl.BlockSpec((1, C_WIDTH), lambda i: (0, 0)),
            pl.BlockSpec((C_GROUPS, C_BLOCK, C_BLOCK), lambda i: (0, 0, 0)),
            pl.BlockSpec((C_BLOCK, C_GROUPS), lambda i: (0, 0)),
        ],
        out_specs=pl.BlockSpec((tm, C_WIDTH), lambda i: (i, 0)),
        out_shape=jax.ShapeDtypeStruct((T, C_WIDTH), BF16),
        compiler_params=_params("parallel"),
        name="mixer_c",
    )(z, ln_g.reshape(1, -1), ln_b.reshape(1, -1), w_s, b_s.T)


def _branch_kernel(ya_ref, yb_ref, yc_ref, wb_ref, ga_ref, gb_ref, gc_ref, o_ref):
    acc = ga_ref[...].astype(F32) * jnp.dot(ya_ref[...], wb_ref[0], preferred_element_type=F32)
    acc += gb_ref[...].astype(F32) * jnp.dot(yb_ref[...], wb_ref[1], preferred_element_type=F32)
    acc += gc_ref[...].astype(F32) * jnp.dot(yc_ref[...], wb_ref[2], preferred_element_type=F32)
    o_ref[...] = acc.astype(o_ref.dtype)


def _branch_merge(ya, yb, yc, wb, gates, tm=1024, tn=512):
    T, W = ya.shape
    D = wb.shape[2]
    tm = min(tm, T)
    nj = D // tn
    y_spec = pl.BlockSpec((tm, W), lambda i, j: (i, 0))
    return pl.pallas_call(
        _branch_kernel,
        grid=(T // tm, nj),
        in_specs=[
            y_spec, y_spec, y_spec,
            pl.BlockSpec((3, W, tn), lambda i, j: (0, 0, j)),
            pl.BlockSpec((tm, tn), lambda i, j: (i, j)),
            pl.BlockSpec((tm, tn), lambda i, j: (i, nj + j)),
            pl.BlockSpec((tm, tn), lambda i, j: (i, 2 * nj + j)),
        ],
        out_specs=pl.BlockSpec((tm, tn), lambda i, j: (i, j)),
        out_shape=jax.ShapeDtypeStruct((T, D), BF16),
        compiler_params=_params("parallel", "arbitrary"),
        name="branch_merge",
    )(ya, yb, yc, wb, gates, gates, gates)


def _route_kernel(x_ref, g_ref, sc_ref, sh_ref, wrh_ref, wrl_ref, rb_ref,
                  hb_ref, hpk_ref, eidx_ref, ew_ref, ch_ref):
    tm = x_ref.shape[0]
    h = _modulated_norm(x_ref[...], g_ref[...], sc_ref[...], sh_ref[...])
    hb = h.astype(BF16)
    hb_ref[...] = hb
    hpk_ref[...] = _pack_halves(h)
    h_lo = (h - hb.astype(F32)).astype(BF16)
    dn = (((1,), (1,)), ((), ()))
    logits = (lax.dot_general(wrh_ref[...], hb, dn, preferred_element_type=F32)
              + lax.dot_general(wrh_ref[...], h_lo, dn, preferred_element_type=F32)
              + lax.dot_general(wrl_ref[...], hb, dn, preferred_element_type=F32))
    scores = jax.nn.sigmoid(logits)
    sel = scores + rb_ref[...]
    neg = -jnp.inf
    n_groups = N_EXPERTS // EXPERT_GROUP

    i8 = lax.broadcasted_iota(I32, (EXPERT_GROUP, tm), 0)
    group_scores = []
    for g in range(n_groups):
        blk = sel[g * EXPERT_GROUP:(g + 1) * EXPERT_GROUP, :]
        m1 = jnp.max(blk, axis=0, keepdims=True)
        first = jnp.min(jnp.where(blk == m1, i8, EXPERT_GROUP), axis=0, keepdims=True)
        m2 = jnp.max(jnp.where(i8 == first, neg, blk), axis=0, keepdims=True)
        group_scores.append(m1 + m2)
    gs = jnp.concatenate(group_scores, axis=0)

    ig = lax.broadcasted_iota(I32, (n_groups, tm), 0)
    gsel = jnp.zeros((n_groups, tm), F32)
    for _ in range(TOPK_GROUPS):
        m = jnp.max(gs, axis=0, keepdims=True)
        first = jnp.min(jnp.where(gs == m, ig, n_groups), axis=0, keepdims=True)
        hit = ig == first
        gsel = jnp.where(hit, 1.0, gsel)
        gs = jnp.where(hit, neg, gs)

    masked = jnp.concatenate(
        [jnp.where(gsel[g:g + 1, :] > 0.0, sel[g * EXPERT_GROUP:(g + 1) * EXPERT_GROUP, :], neg)
         for g in range(n_groups)], axis=0)

    ie = lax.broadcasted_iota(I32, (N_EXPERTS, tm), 0)
    chosen = jnp.zeros((N_EXPERTS, tm), F32)
    idxs, ws = [], []
    for _ in range(TOP_K):
        m = jnp.max(masked, axis=0, keepdims=True)
        first = jnp.min(jnp.where(masked == m, ie, N_EXPERTS), axis=0, keepdims=True)
        hit = ie == first
        ws.append(jnp.sum(jnp.where(hit, scores, 0.0), axis=0, keepdims=True))
        idxs.append(first)
        chosen = jnp.where(hit, 1.0, chosen)
        masked = jnp.where(hit, neg, masked)
    w = jnp.concatenate(ws, axis=0)
    denom = jnp.sum(w, axis=0, keepdims=True) + 1e-20
    ew_ref[...] = w / denom * ROUTED_SCALE
    eidx_ref[...] = jnp.concatenate(idxs, axis=0)
    ch_ref[...] = chosen


def _route(x, g, mod3, row_sc, row_sh, w_router, router_bias, S, tm=256):
    T, D = x.shape
    per = S // tm
    wr_t = w_router.T
    wr_hi = wr_t.astype(BF16)
    wr_lo = (wr_t - wr_hi.astype(F32)).astype(BF16)
    row = lambda i: (i, 0)
    col = lambda i: (0, i)
    fixed = lambda i: (0, 0)
    return pl.pallas_call(
        _route_kernel,
        grid=(T // tm,),
        in_specs=[
            pl.BlockSpec((tm, D), row),
            pl.BlockSpec((1, D), fixed),
            pl.BlockSpec((None, 1, D), lambda i: (row_sc(i // per), 0, 0)),
            pl.BlockSpec((None, 1, D), lambda i: (row_sh(i // per), 0, 0)),
            pl.BlockSpec((N_EXPERTS, D), fixed),
            pl.BlockSpec((N_EXPERTS, D), fixed),
            pl.BlockSpec((N_EXPERTS, 1), fixed),
        ],
        out_specs=[
            pl.BlockSpec((tm, D), row),
            pl.BlockSpec((tm, HALF_D), row),
            pl.BlockSpec((TOP_K, tm), col),
            pl.BlockSpec((TOP_K, tm), col),
            pl.BlockSpec((N_EXPERTS, tm), col),
        ],
        out_shape=[
            jax.ShapeDtypeStruct((T, D), BF16),
            jax.ShapeDtypeStruct((T, HALF_D), U32),
            jax.ShapeDtypeStruct((TOP_K, T), I32),
            jax.ShapeDtypeStruct((TOP_K, T), F32),
            jax.ShapeDtypeStruct((N_EXPERTS, T), F32),
        ],
        compiler_params=_params("parallel"),
        name="norm_route",
    )(x, g.reshape(1, D), mod3, mod3, wr_hi, wr_lo, router_bias.reshape(N_EXPERTS, 1))


def _dispatch_tables(eidx, chosen, n_tiles):
    T = eidx.shape[1]
    sel = chosen.T.astype(I32)
    counts = jnp.sum(sel, axis=0)
    tiles_e = (counts + EXPERT_TILE - 1) // EXPERT_TILE
    tile_end = jnp.cumsum(tiles_e)
    row_off = (tile_end - tiles_e) * EXPERT_TILE
    rank = jnp.cumsum(sel, axis=0) - sel
    e_tk = eidx.T
    dest = row_off[e_tk] + jnp.take_along_axis(rank, e_tk, axis=1)
    tok = jnp.broadcast_to(jnp.arange(T, dtype=I32)[:, None], dest.shape)
    row_token = jnp.zeros((n_tiles * EXPERT_TILE,), I32).at[dest.reshape(-1)].set(tok.reshape(-1))
    tile_expert = jnp.searchsorted(tile_end, jnp.arange(n_tiles, dtype=I32), side="right")
    tile_expert = jnp.minimum(tile_expert, N_EXPERTS - 1).astype(I32)
    n_used = tile_end[-1:].astype(I32)
    return row_token.reshape(n_tiles, 1, EXPERT_TILE), tile_expert, n_used, dest.astype(I32)


def _expert_kernel(te_ref, nu_ref, idx_ref, idx_next_ref, h_hbm, wgu_ref, wd_ref, y_ref, xbuf, sem):
    i = pl.program_id(0)
    n_used = nu_ref[0]
    slot = i % 2

    def gather(idx, dst_slot):
        def body(r, carry):
            pltpu.make_async_copy(h_hbm.at[pl.ds(idx[0, r], 1)], xbuf.at[dst_slot, pl.ds(r, 1)],
                                  sem.at[dst_slot]).start()
            return carry
        lax.fori_loop(0, EXPERT_TILE, body, 0, unroll=GATHER_UNROLL)

    @pl.when(i == 0)
    def _():
        gather(idx_ref, 0)

    @pl.when(i + 1 < n_used)
    def _():
        gather(idx_next_ref, 1 - slot)

    @pl.when(i < n_used)
    def _():
        pltpu.make_async_copy(h_hbm.at[pl.ds(0, EXPERT_TILE)], xbuf.at[slot], sem.at[slot]).wait()
        x_lo, x_hi = _unpack_halves(xbuf[slot])
        gu = jnp.dot(x_lo.astype(BF16), wgu_ref[:HALF_D, :], preferred_element_type=F32)
        gu += jnp.dot(x_hi.astype(BF16), wgu_ref[HALF_D:, :], preferred_element_type=F32)
        gate = gu[:, :D_EXPERT]
        act = (gate * jax.nn.sigmoid(gate) * gu[:, D_EXPERT:]).astype(BF16)
        y = jnp.dot(act, wd_ref[...], preferred_element_type=F32)
        y_ref[...] = _pack_halves(y)

    @pl.when(i >= n_used)
    def _():
        y_ref[...] = jnp.zeros_like(y_ref)


def _experts(hpk, row_token, tile_expert, n_used, wgu, wd):
    n_tiles = row_token.shape[0]
    tm = EXPERT_TILE
    grid_spec = pltpu.PrefetchScalarGridSpec(
        num_scalar_prefetch=2,
        grid=(n_tiles,),
        in_specs=[
            pl.BlockSpec((None, 1, tm), lambda i, te, nu: (i, 0, 0), memory_space=pltpu.SMEM),
            pl.BlockSpec((None, 1, tm), lambda i, te, nu: (jnp.minimum(i + 1, n_tiles - 1), 0, 0),
                         memory_space=pltpu.SMEM),
            pl.BlockSpec(memory_space=pl.ANY),
            pl.BlockSpec((None, D_MODEL, 2 * D_EXPERT), lambda i, te, nu: (te[i], 0, 0)),
            pl.BlockSpec((None, D_EXPERT, D_MODEL), lambda i, te, nu: (te[i], 0, 0)),
        ],
        out_specs=pl.BlockSpec((tm, HALF_D), lambda i, te, nu: (i, 0)),
        scratch_shapes=[pltpu.VMEM((2, tm, HALF_D), U32), pltpu.SemaphoreType.DMA((2,))],
    )
    return pl.pallas_call(
        _expert_kernel,
        grid_spec=grid_spec,
        out_shape=jax.ShapeDtypeStruct((n_tiles * tm, HALF_D), U32),
        compiler_params=_params("arbitrary"),
        name="experts",
    )(tile_expert, n_used, row_token, row_token, hpk, wgu, wd)


def _combine_kernel(pos_ref, pos_next_ref, x_ref, hb_ref, w8_ref, g2_ref, wsgu_ref, wsd_ref, fg_ref, y_hbm,
                    o_ref, ybuf, sem, *, final_norm):
    i = pl.program_id(0)
    n = pl.num_programs(0)
    slot = i % 2
    tm = COMBINE_TILE

    def gather(pos, dst_slot):
        for k in range(TOP_K):
            def body(r, carry, k=k):
                pltpu.make_async_copy(y_hbm.at[pl.ds(pos[0, k * tm + r], 1)],
                                      ybuf.at[dst_slot, k, pl.ds(r, 1)], sem.at[dst_slot]).start()
                return carry
            lax.fori_loop(0, tm, body, 0, unroll=GATHER_UNROLL)

    @pl.when(i == 0)
    def _():
        gather(pos_ref, 0)

    @pl.when(i + 1 < n)
    def _():
        gather(pos_next_ref, 1 - slot)

    hb = hb_ref[...]
    gu = jnp.dot(hb, wsgu_ref[...], preferred_element_type=F32)
    gate = gu[:, :D_EXPERT]
    act = (gate * jax.nn.sigmoid(gate) * gu[:, D_EXPERT:]).astype(BF16)
    shared = jnp.dot(act, wsd_ref[...], preferred_element_type=F32)

    for k in range(TOP_K):
        pltpu.make_async_copy(y_hbm.at[pl.ds(0, tm)], ybuf.at[slot, k], sem.at[slot]).wait()

    w8 = w8_ref[...]
    lo = jnp.zeros((tm, HALF_D), F32)
    hi = jnp.zeros((tm, HALF_D), F32)
    for k in range(TOP_K):
        y_lo, y_hi = _unpack_halves(ybuf[slot, k])
        wk = w8[:, k:k + 1]
        lo += wk * y_lo
        hi += wk * y_hi
    moe = jnp.concatenate([lo, hi], axis=1)
    out = x_ref[...] + g2_ref[...] * (shared + moe)
    if final_norm:
        out = out * lax.rsqrt(jnp.mean(out * out, axis=-1, keepdims=True) + EPS) * fg_ref[...]
    o_ref[...] = out


def _combine(x, hb, w8, mod3, row_g, pos, y, wsgu, wsd, final_g, final_norm, S):
    T, D = x.shape
    tm = COMBINE_TILE
    per = S // tm
    n = T // tm
    row = lambda i: (i, 0)
    fixed = lambda i: (0, 0)
    return pl.pallas_call(
        functools.partial(_combine_kernel, final_norm=final_norm),
        grid=(n,),
        in_specs=[
            pl.BlockSpec((None, 1, TOP_K * tm), lambda i: (i, 0, 0), memory_space=pltpu.SMEM),
            pl.BlockSpec((None, 1, TOP_K * tm), lambda i: (jnp.minimum(i + 1, n - 1), 0, 0),
                         memory_space=pltpu.SMEM),
            pl.BlockSpec((tm, D), row),
            pl.BlockSpec((tm, D), row),
            pl.BlockSpec((tm, TOP_K), row),
            pl.BlockSpec((None, 1, D), lambda i: (row_g(i // per), 0, 0)),
            pl.BlockSpec(wsgu.shape, fixed),
            pl.BlockSpec(wsd.shape, fixed),
            pl.BlockSpec((1, D), fixed),
            pl.BlockSpec(memory_space=pl.ANY),
        ],
        out_specs=pl.BlockSpec((tm, D), row),
        out_shape=jax.ShapeDtypeStruct((T, D), F32),
        scratch_shapes=[pltpu.VMEM((2, TOP_K, tm, HALF_D), U32), pltpu.SemaphoreType.DMA((2,))],
        compiler_params=_params("arbitrary"),
        name="moe_combine",
    )(pos, pos, x, hb, w8, mod3, wsgu, wsd, final_g.reshape(1, D), y)


def _swap_halves(w):
    half = w.shape[-1] // 2
    return jnp.concatenate([w[..., half:], w[..., :half]], axis=-1)


def _split_w_in(w_in):
    a_cols = 3 * A_WIDTH
    b_cols = B_Q_LORA + B_KV_LORA + B_ROPE
    c_cols = 2 * C_WIDTH
    wa = w_in[:, :a_cols].astype(BF16)
    wb_raw = w_in[:, a_cols:a_cols + b_cols]
    kpe = wb_raw[:, B_Q_LORA + B_KV_LORA:]
    zeros = jnp.zeros_like(kpe)
    wb = jnp.concatenate([wb_raw[:, :B_Q_LORA + B_KV_LORA], kpe, zeros, _swap_halves(kpe), zeros],
                         axis=1).astype(BF16)
    wc = w_in[:, a_cols + b_cols:a_cols + b_cols + c_cols].astype(BF16)
    wg = w_in[:, a_cols + b_cols + c_cols:].astype(BF16)
    return wa, wb, wc, wg


def _mla_weights(w_uq, w_ukv):
    r = w_uq.shape[0]
    wq = w_uq.reshape(r, B_HEADS, B_NOPE + B_ROPE)
    nope, pe = wq[..., :B_NOPE], wq[..., B_NOPE:]
    zeros = jnp.zeros_like(pe)
    main = jnp.concatenate([nope, pe, zeros], axis=-1).reshape(r, B_HEADS * B_QK_PAD)
    swapped = jnp.concatenate([_swap_halves(pe), zeros], axis=-1).reshape(r, B_HEADS * LANE)
    wq_ext = jnp.concatenate([main, swapped], axis=1).astype(BF16)
    rk = w_ukv.shape[0]
    wkv = w_ukv.reshape(rk, B_HEADS, B_NOPE + B_VDIM)
    wkv_ext = jnp.concatenate([wkv[..., :B_NOPE].reshape(rk, -1), wkv[..., B_NOPE:].reshape(rk, -1)],
                              axis=1).astype(BF16)
    return wq_ext, wkv_ext


def kernel(x, c, positions, rel_bias, norm_mix, norm_ffn, w_mod, b_mod, w_in, mla_q_norm, mla_w_uq, mla_kv_norm, mla_w_ukv, gmlp_ln_g, gmlp_ln_b, gmlp_w_s, gmlp_b_s, w_branch, w_out, w_router, router_bias, we_gate, we_up, we_down, ws_gate, ws_up, ws_down, final_norm):
    B, S, D = x.shape
    T = B * S
    L = w_mod.shape[0]
    xt = x.reshape(T, D)

    mod = _modulation(c, w_mod, b_mod)
    mod3 = mod.reshape(L * B * 6, 1, D)
    cos, sin = _rope_tables(positions)
    a_bias = _mixer_a_bias(rel_bias)
    n_tiles = (T * TOP_K) // EXPERT_TILE + N_EXPERTS

    for l in range(L):
        row = lambda k: (lambda b, l=l, k=k: (l * B + b) * 6 + k)
        wa, wb, wc, wg = _split_w_in(w_in[l])
        wq_ext, wkv_ext = _mla_weights(mla_w_uq[l], mla_w_ukv[l])

        h = _norm_mod(xt, norm_mix[l], mod3, row(1), row(0), S)
        qkv = _matmul(h, wa, BF16, name="proj_a")
        bproj = _matmul(h, wb, F32, name="proj_b")
        z = _matmul(h, wc, BF16, act="gelu", name="proj_c")
        gates = _matmul(h, wg, BF16, act="sigmoid", name="proj_gates")

        y_a = _mixer_a(qkv, a_bias, B, S)
        qp, kp, v = _mla_prep(bproj, cos, sin, mla_q_norm[l], mla_kv_norm[l], wq_ext, wkv_ext)
        y_b = _mla_attention(qp, kp, v, B, S)
        y_c = _mixer_c(z, gmlp_ln_g[l], gmlp_ln_b[l], gmlp_w_s[l], gmlp_b_s[l])

        merged = _branch_merge(y_a, y_b, y_c, w_branch[l].astype(BF16), gates)
        xt = _matmul_residual(merged, w_out[l].astype(BF16), xt, mod3, row(2), S)

        hb, hpk, eidx, ew, chosen = _route(xt, norm_ffn[l], mod3, row(4), row(3), w_router[l], router_bias[l], S)
        row_token, tile_expert, n_used, dest = _dispatch_tables(eidx, chosen, n_tiles)
        wgu = jnp.concatenate([we_gate[l].astype(BF16), we_up[l].astype(BF16)], axis=2)
        y = _experts(hpk, row_token, tile_expert, n_used, wgu, we_down[l].astype(BF16))
        pos = dest.reshape(T // COMBINE_TILE, COMBINE_TILE, TOP_K).transpose(0, 2, 1)
        pos = pos.reshape(T // COMBINE_TILE, 1, TOP_K * COMBINE_TILE)
        wsgu = jnp.concatenate([ws_gate[l], ws_up[l]], axis=1).astype(BF16)
        xt = _combine(xt, hb, ew.T, mod3, row(5), pos, y, wsgu, ws_down[l].astype(BF16),
                      final_norm, l == L - 1, S)

    return xt.reshape(B, S, D)
```

```python
import functools

import jax
import jax.numpy as jnp
from jax import lax
from jax.experimental import pallas as pl
from jax.experimental.pallas import tpu as pltpu

F32 = jnp.float32
BF16 = jnp.bfloat16
U32 = jnp.uint32
I32 = jnp.int32

EPS = 1e-6
NEG_INF = -1e30

D_MODEL = 2048
HALF_D = D_MODEL // 2
CHUNK = 64

A_HEADS = 16
A_HEAD_DIM = 64
A_WIDTH = A_HEADS * A_HEAD_DIM
A_LEFT_CHUNKS = 8
MAX_REL = 128
A_QBLOCK = 256
A_KBLOCKS = 3

B_HEADS = 8
B_NOPE = 128
B_ROPE = 64
B_VDIM = 128
B_Q_LORA = 512
B_KV_LORA = 256
B_QK_PAD = 256
ROPE_THETA = 10000.0
B_TQ = 1024
B_TK = 512

C_BLOCK = 128
C_GROUPS = 8
C_WIDTH = 1024

N_EXPERTS = 64
EXPERT_GROUP = 8
TOP_K = 8
TOPK_GROUPS = 4
D_EXPERT = 512
ROUTED_SCALE = 2.5
EXPERT_TILE = 256
COMBINE_TILE = 128
GATHER_UNROLL = 8

LANE = 128
V7X_VMEM_LIMIT = 56 * 1024 * 1024


def _params(*sem):
    return pltpu.CompilerParams(dimension_semantics=sem, vmem_limit_bytes=V7X_VMEM_LIMIT)


def _pack_halves(v):
    w = v.shape[1] // 2
    bits = lax.bitcast_convert_type(v.astype(BF16).astype(F32), U32)
    return (bits[:, :w] >> 16) | bits[:, w:]


def _unpack_halves(p):
    lo = lax.bitcast_convert_type(p << 16, F32)
    hi = lax.bitcast_convert_type(p & jnp.uint32(0xFFFF0000), F32)
    return lo, hi


def _mod_kernel(c_ref, w_ref, b_ref, o_ref):
    c = c_ref[...]
    ca = (c * jax.nn.sigmoid(c)).astype(BF16)
    o_ref[...] = jnp.dot(ca, w_ref[...].astype(BF16), preferred_element_type=F32) + b_ref[...]


def _modulation(c, w_mod, b_mod):
    L, D, N = w_mod.shape
    B = c.shape[0]
    tn = 512
    return pl.pallas_call(
        _mod_kernel,
        grid=(L, N // tn),
        in_specs=[
            pl.BlockSpec((B, D), lambda l, j: (0, 0)),
            pl.BlockSpec((None, D, tn), lambda l, j: (l, 0, j)),
            pl.BlockSpec((None, 1, tn), lambda l, j: (l, 0, j)),
        ],
        out_specs=pl.BlockSpec((None, B, tn), lambda l, j: (l, 0, j)),
        out_shape=jax.ShapeDtypeStruct((L, B, N), F32),
        compiler_params=_params("parallel", "arbitrary"),
        name="modulation",
    )(c, w_mod, b_mod.reshape(L, 1, N))


def _modulated_norm(x, g, sc, sh):
    r = lax.rsqrt(jnp.mean(x * x, axis=-1, keepdims=True) + EPS)
    return (x * r * g) * (1.0 + sc) + sh


def _norm_mod_kernel(x_ref, g_ref, sc_ref, sh_ref, o_ref):
    o_ref[...] = _modulated_norm(x_ref[...], g_ref[...], sc_ref[...], sh_ref[...]).astype(o_ref.dtype)


def _norm_mod(x, g, mod3, row_sc, row_sh, S, tm=512):
    T, D = x.shape
    per = S // tm
    return pl.pallas_call(
        _norm_mod_kernel,
        grid=(T // tm,),
        in_specs=[
            pl.BlockSpec((tm, D), lambda i: (i, 0)),
            pl.BlockSpec((1, D), lambda i: (0, 0)),
            pl.BlockSpec((None, 1, D), lambda i: (row_sc(i // per), 0, 0)),
            pl.BlockSpec((None, 1, D), lambda i: (row_sh(i // per), 0, 0)),
        ],
        out_specs=pl.BlockSpec((tm, D), lambda i: (i, 0)),
        out_shape=jax.ShapeDtypeStruct((T, D), BF16),
        compiler_params=_params("parallel"),
        name="norm_mod",
    )(x, g.reshape(1, D), mod3, mod3)


def _mm_kernel(a_ref, w_ref, o_ref, *, act):
    acc = jnp.dot(a_ref[...], w_ref[...], preferred_element_type=F32)
    if act == "gelu":
        acc = jax.nn.gelu(acc)
    elif act == "sigmoid":
        acc = jax.nn.sigmoid(acc)
    o_ref[...] = acc.astype(o_ref.dtype)


def _matmul(a, w, out_dtype, act=None, tm=1024, tn=512, name="matmul"):
    M, K = a.shape
    N = w.shape[1]
    tm = min(tm, M)
    return pl.pallas_call(
        functools.partial(_mm_kernel, act=act),
        grid=(M // tm, N // tn),
        in_specs=[
            pl.BlockSpec((tm, K), lambda i, j: (i, 0)),
            pl.BlockSpec((K, tn), lambda i, j: (0, j)),
        ],
        out_specs=pl.BlockSpec((tm, tn), lambda i, j: (i, j)),
        out_shape=jax.ShapeDtypeStruct((M, N), out_dtype),
        compiler_params=_params("parallel", "arbitrary"),
        name=name,
    )(a, w)


def _mm_residual_kernel(a_ref, w_ref, x_ref, g_ref, o_ref):
    acc = jnp.dot(a_ref[...], w_ref[...], preferred_element_type=F32)
    o_ref[...] = x_ref[...] + g_ref[...] * acc


def _matmul_residual(a, w, x, mod3, row_g, S, tm=1024, tn=512):
    M, K = a.shape
    N = w.shape[1]
    tm = min(tm, S)
    per = S // tm
    return pl.pallas_call(
        _mm_residual_kernel,
        grid=(M // tm, N // tn),
        in_specs=[
            pl.BlockSpec((tm, K), lambda i, j: (i, 0)),
            pl.BlockSpec((K, tn), lambda i, j: (0, j)),
            pl.BlockSpec((tm, tn), lambda i, j: (i, j)),
            pl.BlockSpec((None, 1, tn), lambda i, j: (row_g(i // per), 0, j)),
        ],
        out_specs=pl.BlockSpec((tm, tn), lambda i, j: (i, j)),
        out_shape=jax.ShapeDtypeStruct((M, N), F32),
        compiler_params=_params("parallel", "arbitrary"),
        name="out_proj_residual",
    )(a, w, x, mod3)


def _mixer_a_kernel(q_ref, k0_ref, k1_ref, k2_ref, v0_ref, v1_ref, v2_ref, bias_ref, o_ref, *, scale):
    i = pl.program_id(2)
    qb = A_QBLOCK
    kw = A_KBLOCKS * qb
    q = q_ref[...]
    k = jnp.concatenate([k0_ref[...], k1_ref[...], k2_ref[...]], axis=0)
    v = jnp.concatenate([v0_ref[...], v1_ref[...], v2_ref[...]], axis=0)
    lane = lax.broadcasted_iota(I32, (qb, LANE), 1)
    kpos = (i - (A_KBLOCKS - 1)) * qb + lax.broadcasted_iota(I32, (1, kw), 1)
    in_seq = kpos >= 0
    outs = []
    for hh in range(2):
        head_lanes = (lane < A_HEAD_DIM) if hh == 0 else (lane >= A_HEAD_DIM)
        qm = jnp.where(head_lanes, q, jnp.zeros_like(q)) * jnp.asarray(scale, q.dtype)
        s = lax.dot_general(qm, k, (((1,), (1,)), ((), ())), preferred_element_type=F32)
        s = s + bias_ref[hh]
        s = jnp.where(in_seq, s, NEG_INF)
        m = jnp.max(s, axis=-1, keepdims=True)
        p = jnp.exp(s - m)
        l = jnp.sum(p, axis=-1, keepdims=True)
        o = jnp.dot(p.astype(BF16), v, preferred_element_type=F32)
        outs.append(o / l)
    o_ref[...] = jnp.where(lane < A_HEAD_DIM, outs[0], outs[1]).astype(o_ref.dtype)


def _mixer_a_bias(rel_bias):
    qb = A_QBLOCK
    kw = A_KBLOCKS * qb
    qi = jnp.arange(qb)[:, None]
    kj = jnp.arange(kw)[None, :]
    d = qi + (A_KBLOCKS - 1) * qb - kj
    qc = qi // CHUNK
    kc = kj // CHUNK
    first = (A_KBLOCKS - 1) * qb // CHUNK - A_LEFT_CHUNKS
    in_band = (kc >= qc + first) & (kc <= qc + first + A_LEFT_CHUNKS)
    idx = jnp.clip(d, -MAX_REL, MAX_REL) + MAX_REL
    bias = rel_bias[:, idx].astype(F32)
    return jnp.where(in_band[None], bias, NEG_INF)


def _mixer_a(qkv, bias, B, S):
    T = qkv.shape[0]
    qb = A_QBLOCK
    nq = S // qb
    pairs = A_HEADS // 2

    def kv_spec(j, col0):
        return pl.BlockSpec(
            (qb, LANE), lambda p, b, i: (b * nq + jnp.maximum(i - (A_KBLOCKS - 1) + j, 0), col0 + p))

    return pl.pallas_call(
        functools.partial(_mixer_a_kernel, scale=A_HEAD_DIM ** -0.5),
        grid=(pairs, B, nq),
        in_specs=[
            pl.BlockSpec((qb, LANE), lambda p, b, i: (b * nq + i, p)),
            kv_spec(0, pairs), kv_spec(1, pairs), kv_spec(2, pairs),
            kv_spec(0, 2 * pairs), kv_spec(1, 2 * pairs), kv_spec(2, 2 * pairs),
            pl.BlockSpec((2, qb, A_KBLOCKS * qb), lambda p, b, i: (p, 0, 0)),
        ],
        out_specs=pl.BlockSpec((qb, LANE), lambda p, b, i: (b * nq + i, p)),
        out_shape=jax.ShapeDtypeStruct((T, A_WIDTH), BF16),
        compiler_params=_params("parallel", "parallel", "arbitrary"),
        name="mixer_a",
    )(qkv, qkv, qkv, qkv, qkv, qkv, qkv, bias)


def _rope_table_kernel(pos_ref, inv_ref, c_ref, s_ref):
    ang = pos_ref[...].astype(F32) * inv_ref[...]
    lane = lax.broadcasted_iota(I32, ang.shape, 1)
    first_half = (lane % B_ROPE) < (B_ROPE // 2)
    c_ref[...] = jnp.cos(ang)
    sn = jnp.sin(ang)
    s_ref[...] = jnp.where(first_half, -sn, sn)


def _rope_tables(positions, tm=512):
    T = positions.size
    half = B_ROPE // 2
    inv = ROPE_THETA ** (-jnp.arange(half, dtype=F32) / half)
    inv = jnp.tile(inv, LANE // half).reshape(1, LANE)
    return pl.pallas_call(
        _rope_table_kernel,
        grid=(T // tm,),
        in_specs=[pl.BlockSpec((tm, 1), lambda i: (i, 0)), pl.BlockSpec((1, LANE), lambda i: (0, 0))],
        out_specs=[pl.BlockSpec((tm, LANE), lambda i: (i, 0))] * 2,
        out_shape=[jax.ShapeDtypeStruct((T, LANE), F32)] * 2,
        compiler_params=_params("parallel"),
        name="rope_tables",
    )(positions.reshape(T, 1), inv)


def _mla_prep_kernel(bp_ref, c_ref, s_ref, qn_ref, kvn_ref, wq_ref, wk_ref, wvt_ref, q_out, k_out, vt_out, *, scale):
    bp = bp_ref[...]
    cq = bp[:, :B_Q_LORA]
    ckv = bp[:, B_Q_LORA:B_Q_LORA + B_KV_LORA]
    kpe = bp[:, B_Q_LORA + B_KV_LORA:B_Q_LORA + B_KV_LORA + LANE]
    kpe_sw = bp[:, B_Q_LORA + B_KV_LORA + LANE:]
    cqn = cq * lax.rsqrt(jnp.mean(cq * cq, axis=-1, keepdims=True) + EPS) * qn_ref[...]
    ckvn = ckv * lax.rsqrt(jnp.mean(ckv * ckv, axis=-1, keepdims=True) + EPS) * kvn_ref[...]
    qa = jnp.dot(cqn.astype(BF16), wq_ref[...], preferred_element_type=F32)
    ckvb = ckvn.astype(BF16)
    kn = jnp.dot(ckvb, wk_ref[...], preferred_element_type=F32)
    vt_out[...] = lax.dot_general(wvt_ref[...], ckvb, (((1,), (1,)), ((), ())),
                                  preferred_element_type=F32).astype(BF16)
    cos = c_ref[...]
    sin = s_ref[...]
    k_rot = (kpe * cos + kpe_sw * sin).astype(BF16)
    sw0 = B_HEADS * B_QK_PAD
    for h in range(B_HEADS):
        c0 = h * B_QK_PAD
        q_out[:, c0:c0 + B_NOPE] = (qa[:, c0:c0 + B_NOPE] * scale).astype(BF16)
        pe = qa[:, c0 + B_NOPE:c0 + B_QK_PAD]
        pe_sw = qa[:, sw0 + h * LANE:sw0 + (h + 1) * LANE]
        q_out[:, c0 + B_NOPE:c0 + B_QK_PAD] = ((pe * cos + pe_sw * sin) * scale).astype(BF16)
        k_out[:, c0:c0 + B_NOPE] = kn[:, h * B_NOPE:(h + 1) * B_NOPE].astype(BF16)
        k_out[:, c0 + B_NOPE:c0 + B_QK_PAD] = k_rot


def _mla_prep(bproj, cos, sin, q_norm, kv_norm, wq, wk, wvt, tm=256):
    T = bproj.shape[0]
    scale = (B_NOPE + B_ROPE) ** -0.5
    qk_w = B_HEADS * B_QK_PAD
    v_w = B_HEADS * B_VDIM
    row = lambda i: (i, 0)
    fixed = lambda i: (0, 0)
    return pl.pallas_call(
        functools.partial(_mla_prep_kernel, scale=scale),
        grid=(T // tm,),
        in_specs=[
            pl.BlockSpec((tm, bproj.shape[1]), row),
            pl.BlockSpec((tm, LANE), row),
            pl.BlockSpec((tm, LANE), row),
            pl.BlockSpec((1, B_Q_LORA), fixed),
            pl.BlockSpec((1, B_KV_LORA), fixed),
            pl.BlockSpec(wq.shape, fixed),
            pl.BlockSpec(wk.shape, fixed),
            pl.BlockSpec(wvt.shape, fixed),
        ],
        out_specs=[pl.BlockSpec((tm, qk_w), row), pl.BlockSpec((tm, qk_w), row),
                   pl.BlockSpec((v_w, tm), lambda i: (0, i))],
        out_shape=[jax.ShapeDtypeStruct((T, qk_w), BF16), jax.ShapeDtypeStruct((T, qk_w), BF16),
                   jax.ShapeDtypeStruct((v_w, T), BF16)],
        compiler_params=_params("parallel"),
        name="mla_prep",
    )(bproj, cos, sin, q_norm.reshape(1, -1), kv_norm.reshape(1, -1), wq, wk, wvt)


def _mla_attn_kernel(qt_ref, kt_ref, q_ref, k_ref, vt_ref, o_ref, m_sc, l_sc, acc_sc, *, tq, tk):
    p = pl.program_id(2)
    qi = qt_ref[p]
    kj = kt_ref[p]

    @pl.when(kj == 0)
    def _():
        m_sc[...] = jnp.full_like(m_sc, NEG_INF)
        l_sc[...] = jnp.zeros_like(l_sc)
        acc_sc[...] = jnp.zeros_like(acc_sc)

    def online_softmax_step(s):
        m_prev = m_sc[...]
        m_new = jnp.maximum(m_prev, jnp.max(s, axis=0, keepdims=True))
        alpha = jnp.exp(m_prev - m_new)
        pr = jnp.exp(s - m_new)
        l_sc[...] = alpha * l_sc[...] + jnp.sum(pr, axis=0, keepdims=True)
        acc_sc[...] = alpha * acc_sc[...] + jnp.dot(vt_ref[...], pr.astype(BF16), preferred_element_type=F32)
        m_sc[...] = m_new

    def scores():
        return lax.dot_general(k_ref[...], q_ref[...], (((1,), (1,)), ((), ())), preferred_element_type=F32)

    k_end = (kj + 1) * tk
    crosses_diagonal = k_end > qi * tq

    @pl.when(jnp.logical_not(crosses_diagonal))
    def _():
        online_softmax_step(scores())

    @pl.when(crosses_diagonal)
    def _():
        s = scores()
        kc = (kj * tk + lax.broadcasted_iota(I32, s.shape, 0)) // CHUNK
        qc = (qi * tq + lax.broadcasted_iota(I32, s.shape, 1)) // CHUNK
        online_softmax_step(jnp.where(kc <= qc, s, NEG_INF))

    @pl.when(k_end == (qi + 1) * tq)
    def _():
        o_ref[...] = (acc_sc[...] / l_sc[...]).T.astype(o_ref.dtype)


def _mla_attention(qp, kp, vt, B, S):
    T = qp.shape[0]
    tq = min(B_TQ, S)
    tk = min(B_TK, S)
    nq = S // tq
    nk = S // tk
    pairs = [(qi, kj) for qi in range(nq) for kj in range((qi + 1) * tq // tk)]
    qt = jnp.asarray([p[0] for p in pairs], I32)
    kt = jnp.asarray([p[1] for p in pairs], I32)
    grid_spec = pltpu.PrefetchScalarGridSpec(
        num_scalar_prefetch=2,
        grid=(B, B_HEADS, len(pairs)),
        in_specs=[
            pl.BlockSpec((tq, B_QK_PAD), lambda b, h, p, qt, kt: (b * nq + qt[p], h)),
            pl.BlockSpec((tk, B_QK_PAD), lambda b, h, p, qt, kt: (b * nk + kt[p], h)),
            pl.BlockSpec((B_VDIM, tk), lambda b, h, p, qt, kt: (h, b * nk + kt[p])),
        ],
        out_specs=pl.BlockSpec((tq, B_VDIM), lambda b, h, p, qt, kt: (b * nq + qt[p], h)),
        scratch_shapes=[pltpu.VMEM((1, tq), F32), pltpu.VMEM((1, tq), F32), pltpu.VMEM((B_VDIM, tq), F32)],
    )
    return pl.pallas_call(
        functools.partial(_mla_attn_kernel, tq=tq, tk=tk),
        grid_spec=grid_spec,
        out_shape=jax.ShapeDtypeStruct((T, B_HEADS * B_VDIM), BF16),
        compiler_params=_params("parallel", "parallel", "arbitrary"),
        name="mla_attention",
    )(qt, kt, qp, kp, vt)


def _mixer_c_kernel(z_ref, lg_ref, lb_ref, ws_ref, bs_ref, o_ref):
    tm = z_ref.shape[0]
    u = z_ref[:, :C_WIDTH].astype(F32)
    v = z_ref[:, C_WIDTH:].astype(F32)
    mu = jnp.mean(v, axis=-1, keepdims=True)
    var = jnp.mean(jnp.square(v - mu), axis=-1, keepdims=True)
    vn = ((v - mu) * lax.rsqrt(var + EPS) * lg_ref[...] + lb_ref[...]).astype(BF16)
    row = lax.broadcasted_iota(I32, (C_BLOCK, C_BLOCK), 0)
    col = lax.broadcasted_iota(I32, (C_BLOCK, C_BLOCK), 1)
    causal = col <= row
    gd = C_WIDTH // C_GROUPS
    for g in range(C_GROUPS):
        w = jnp.where(causal, ws_ref[g], 0.0).astype(BF16)
        b = bs_ref[:, g:g + 1]
        for n in range(tm // C_BLOCK):
            rows = slice(n * C_BLOCK, (n + 1) * C_BLOCK)
            cols = slice(g * gd, (g + 1) * gd)
            mixed = jnp.dot(w, vn[rows, cols], preferred_element_type=F32) + b
            o_ref[rows, cols] = (u[rows, cols] * mixed).astype(o_ref.dtype)


def _mixer_c(z, ln_g, ln_b, w_s, b_s, tm=256):
    T = z.shape[0]
    return pl.pallas_call(
        _mixer_c_kernel,
        grid=(T // tm,),
        in_specs=[
            pl.BlockSpec((tm, 2 * C_WIDTH), lambda i: (i, 0)),
            pl.BlockSpec((1, C_WIDTH), lambda i: (0, 0)),
            pl.BlockSpec((1, C_WIDTH), lambda i: (0, 0)),
            pl.BlockSpec((C_GROUPS, C_BLOCK, C_BLOCK), lambda i: (0, 0, 0)),
            pl.BlockSpec((C_BLOCK, C_GROUPS), lambda i: (0, 0)),
        ],
        out_specs=pl.BlockSpec((tm, C_WIDTH), lambda i: (i, 0)),
        out_shape=jax.ShapeDtypeStruct((T, C_WIDTH), BF16),
        compiler_params=_params("parallel"),
        name="mixer_c",
    )(z, ln_g.reshape(1, -1), ln_b.reshape(1, -1), w_s, b_s.T)


def _branch_kernel(ya_ref, yb_ref, yc_ref, wb_ref, ga_ref, gb_ref, gc_ref, o_ref):
    acc = ga_ref[...].astype(F32) * jnp.dot(ya_ref[...], wb_ref[0], preferred_element_type=F32)
    acc += gb_ref[...].astype(F32) * jnp.dot(yb_ref[...], wb_ref[1], preferred_element_type=F32)
    acc += gc_ref[...].astype(F32) * jnp.dot(yc_ref[...], wb_ref[2], preferred_element_type=F32)
    o_ref[...] = acc.astype(o_ref.dtype)


def _branch_merge(ya, yb, yc, wb, gates, tm=1024, tn=512):
    T, W = ya.shape
    D = wb.shape[2]
    tm = min(tm, T)
    nj = D // tn
    y_spec = pl.BlockSpec((tm, W), lambda i, j: (i, 0))
    return pl.pallas_call(
        _branch_kernel,
        grid=(T // tm, nj),
        in_specs=[
            y_spec, y_spec, y_spec,
            pl.BlockSpec((3, W, tn), lambda i, j: (0, 0, j)),
            pl.BlockSpec((tm, tn), lambda i, j: (i, j)),
            pl.BlockSpec((tm, tn), lambda i, j: (i, nj + j)),
            pl.BlockSpec((tm, tn), lambda i, j: (i, 2 * nj + j)),
        ],
        out_specs=pl.BlockSpec((tm, tn), lambda i, j: (i, j)),
        out_shape=jax.ShapeDtypeStruct((T, D), BF16),
        compiler_params=_params("parallel", "arbitrary"),
        name="branch_merge",
    )(ya, yb, yc, wb, gates, gates, gates)


def _route_kernel(x_ref, g_ref, sc_ref, sh_ref, wrh_ref, wrl_ref, rb_ref,
                  hb_ref, hpk_ref, eidx_ref, ew_ref, ch_ref):
    tm = x_ref.shape[0]
    h = _modulated_norm(x_ref[...], g_ref[...], sc_ref[...], sh_ref[...])
    hb = h.astype(BF16)
    hb_ref[...] = hb
    hpk_ref[...] = _pack_halves(h)
    h_lo = (h - hb.astype(F32)).astype(BF16)
    dn = (((1,), (1,)), ((), ()))
    logits = (lax.dot_general(wrh_ref[...], hb, dn, preferred_element_type=F32)
              + lax.dot_general(wrh_ref[...], h_lo, dn, preferred_element_type=F32)
              + lax.dot_general(wrl_ref[...], hb, dn, preferred_element_type=F32))
    scores = jax.nn.sigmoid(logits)
    sel = scores + rb_ref[...]
    neg = -jnp.inf
    n_groups = N_EXPERTS // EXPERT_GROUP

    i8 = lax.broadcasted_iota(I32, (EXPERT_GROUP, tm), 0)
    group_scores = []
    for g in range(n_groups):
        blk = sel[g * EXPERT_GROUP:(g + 1) * EXPERT_GROUP, :]
        m1 = jnp.max(blk, axis=0, keepdims=True)
        first = jnp.min(jnp.where(blk == m1, i8, EXPERT_GROUP), axis=0, keepdims=True)
        m2 = jnp.max(jnp.where(i8 == first, neg, blk), axis=0, keepdims=True)
        group_scores.append(m1 + m2)
    gs = jnp.concatenate(group_scores, axis=0)

    ig = lax.broadcasted_iota(I32, (n_groups, tm), 0)
    gsel = jnp.zeros((n_groups, tm), F32)
    for _ in range(TOPK_GROUPS):
        m = jnp.max(gs, axis=0, keepdims=True)
        first = jnp.min(jnp.where(gs == m, ig, n_groups), axis=0, keepdims=True)
        hit = ig == first
        gsel = jnp.where(hit, 1.0, gsel)
        gs = jnp.where(hit, neg, gs)

    masked = jnp.concatenate(
        [jnp.where(gsel[g:g + 1, :] > 0.0, sel[g * EXPERT_GROUP:(g + 1) * EXPERT_GROUP, :], neg)
         for g in range(n_groups)], axis=0)

    ie = lax.broadcasted_iota(I32, (N_EXPERTS, tm), 0)
    chosen = jnp.zeros((N_EXPERTS, tm), F32)
    idxs, ws = [], []
    for _ in range(TOP_K):
        m = jnp.max(masked, axis=0, keepdims=True)
        first = jnp.min(jnp.where(masked == m, ie, N_EXPERTS), axis=0, keepdims=True)
        hit = ie == first
        ws.append(jnp.sum(jnp.where(hit, scores, 0.0), axis=0, keepdims=True))
        idxs.append(first)
        chosen = jnp.where(hit, 1.0, chosen)
        masked = jnp.where(hit, neg, masked)
    w = jnp.concatenate(ws, axis=0)
    denom = jnp.sum(w, axis=0, keepdims=True) + 1e-20
    ew_ref[...] = w / denom * ROUTED_SCALE
    eidx_ref[...] = jnp.concatenate(idxs, axis=0)
    ch_ref[...] = chosen


def _route(x, g, mod3, row_sc, row_sh, w_router, router_bias, S, tm=256):
    T, D = x.shape
    per = S // tm
    wr_t = w_router.T
    wr_hi = wr_t.astype(BF16)
    wr_lo = (wr_t - wr_hi.astype(F32)).astype(BF16)
    row = lambda i: (i, 0)
    col = lambda i: (0, i)
    fixed = lambda i: (0, 0)
    return pl.pallas_call(
        _route_kernel,
        grid=(T // tm,),
        in_specs=[
            pl.BlockSpec((tm, D), row),
            pl.BlockSpec((1, D), fixed),
            pl.BlockSpec((None, 1, D), lambda i: (row_sc(i // per), 0, 0)),
            pl.BlockSpec((None, 1, D), lambda i: (row_sh(i // per), 0, 0)),
            pl.BlockSpec((N_EXPERTS, D), fixed),
            pl.BlockSpec((N_EXPERTS, D), fixed),
            pl.BlockSpec((N_EXPERTS, 1), fixed),
        ],
        out_specs=[
            pl.BlockSpec((tm, D), row),
            pl.BlockSpec((tm, HALF_D), row),
            pl.BlockSpec((TOP_K, tm), col),
            pl.BlockSpec((TOP_K, tm), col),
            pl.BlockSpec((N_EXPERTS, tm), col),
        ],
        out_shape=[
            jax.ShapeDtypeStruct((T, D), BF16),
            jax.ShapeDtypeStruct((T, HALF_D), U32),
            jax.ShapeDtypeStruct((TOP_K, T), I32),
            jax.ShapeDtypeStruct((TOP_K, T), F32),
            jax.ShapeDtypeStruct((N_EXPERTS, T), F32),
        ],
        compiler_params=_params("parallel"),
        name="norm_route",
    )(x, g.reshape(1, D), mod3, mod3, wr_hi, wr_lo, router_bias.reshape(N_EXPERTS, 1))


def _dispatch_tables(eidx, chosen, n_tiles):
    T = eidx.shape[1]
    sel = chosen.T.astype(I32)
    counts = jnp.sum(sel, axis=0)
    tiles_e = (counts + EXPERT_TILE - 1) // EXPERT_TILE
    tile_end = jnp.cumsum(tiles_e)
    row_off = (tile_end - tiles_e) * EXPERT_TILE
    blk = 256
    selb = sel.reshape(T // blk, blk, N_EXPERTS)
    block_tot = jnp.sum(selb, axis=1)
    block_off = jnp.cumsum(block_tot, axis=0) - block_tot
    strict_lower = (jnp.arange(blk)[:, None] > jnp.arange(blk)[None, :]).astype(F32)
    in_block = jnp.einsum("ij,bje->bie", strict_lower, selb.astype(F32)).astype(I32)
    rank = (in_block + block_off[:, None, :]).reshape(T, N_EXPERTS)
    e_tk = eidx.T
    onehot = e_tk[:, :, None] == jnp.arange(N_EXPERTS, dtype=I32)[None, None, :]
    dest = jnp.sum(jnp.where(onehot, (row_off[None, :] + rank)[:, None, :], 0), axis=2)
    tok = jnp.broadcast_to(jnp.arange(T, dtype=I32)[:, None], dest.shape)
    row_token = jnp.zeros((n_tiles * EXPERT_TILE,), I32).at[dest.reshape(-1)].set(tok.reshape(-1))
    tile_expert = jnp.searchsorted(tile_end, jnp.arange(n_tiles, dtype=I32), side="right")
    tile_expert = jnp.minimum(tile_expert, N_EXPERTS - 1).astype(I32)
    n_used = tile_end[-1:].astype(I32)
    return row_token.reshape(n_tiles, 1, EXPERT_TILE), tile_expert, n_used, dest.astype(I32)


def _expert_kernel(te_ref, nu_ref, idx_ref, idx_next_ref, h_hbm, wg_ref, wu_ref, wd_ref, y_ref,
                   xbuf, wgu_bf, wd_bf, sem):
    i = pl.program_id(0)
    n_used = nu_ref[0]
    slot = i % 2

    @pl.when((i == 0) | (te_ref[i] != te_ref[jnp.maximum(i - 1, 0)]))
    def _():
        wgu_bf[:, :D_EXPERT] = wg_ref[...].astype(BF16)
        wgu_bf[:, D_EXPERT:] = wu_ref[...].astype(BF16)
        wd_bf[...] = wd_ref[...].astype(BF16)

    def gather(idx, dst_slot):
        def body(r, carry):
            pltpu.make_async_copy(h_hbm.at[pl.ds(idx[0, r], 1)], xbuf.at[dst_slot, pl.ds(r, 1)],
                                  sem.at[dst_slot]).start()
            return carry
        lax.fori_loop(0, EXPERT_TILE, body, 0, unroll=GATHER_UNROLL)

    @pl.when(i == 0)
    def _():
        gather(idx_ref, 0)

    @pl.when(i + 1 < n_used)
    def _():
        gather(idx_next_ref, 1 - slot)

    @pl.when(i < n_used)
    def _():
        pltpu.make_async_copy(h_hbm.at[pl.ds(0, EXPERT_TILE)], xbuf.at[slot], sem.at[slot]).wait()
        x_lo, x_hi = _unpack_halves(xbuf[slot])
        gu = jnp.dot(x_lo.astype(BF16), wgu_bf[:HALF_D, :], preferred_element_type=F32)
        gu += jnp.dot(x_hi.astype(BF16), wgu_bf[HALF_D:, :], preferred_element_type=F32)
        gate = gu[:, :D_EXPERT]
        act = (gate * jax.nn.sigmoid(gate) * gu[:, D_EXPERT:]).astype(BF16)
        y = jnp.dot(act, wd_bf[...], preferred_element_type=F32)
        y_ref[...] = _pack_halves(y)

    @pl.when(i >= n_used)
    def _():
        y_ref[...] = jnp.zeros_like(y_ref)


def _experts(hpk, row_token, tile_expert, n_used, we_gate, we_up, we_down, layer):
    n_tiles = row_token.shape[0]
    tm = EXPERT_TILE
    expert_block = lambda i, te, nu: (layer, te[i], 0, 0)
    grid_spec = pltpu.PrefetchScalarGridSpec(
        num_scalar_prefetch=2,
        grid=(n_tiles,),
        in_specs=[
            pl.BlockSpec((None, 1, tm), lambda i, te, nu: (i, 0, 0), memory_space=pltpu.SMEM),
            pl.BlockSpec((None, 1, tm), lambda i, te, nu: (jnp.minimum(i + 1, n_tiles - 1), 0, 0),
                         memory_space=pltpu.SMEM),
            pl.BlockSpec(memory_space=pl.ANY),
            pl.BlockSpec((None, None, D_MODEL, D_EXPERT), expert_block),
            pl.BlockSpec((None, None, D_MODEL, D_EXPERT), expert_block),
            pl.BlockSpec((None, None, D_EXPERT, D_MODEL), expert_block),
        ],
        out_specs=pl.BlockSpec((tm, HALF_D), lambda i, te, nu: (i, 0)),
        scratch_shapes=[pltpu.VMEM((2, tm, HALF_D), U32),
                        pltpu.VMEM((D_MODEL, 2 * D_EXPERT), BF16),
                        pltpu.VMEM((D_EXPERT, D_MODEL), BF16),
                        pltpu.SemaphoreType.DMA((2,))],
    )
    return pl.pallas_call(
        _expert_kernel,
        grid_spec=grid_spec,
        out_shape=jax.ShapeDtypeStruct((n_tiles * tm, HALF_D), U32),
        compiler_params=_params("arbitrary"),
        name="experts",
    )(tile_expert, n_used, row_token, row_token, hpk, we_gate, we_up, we_down)


def _combine_kernel(pos_ref, pos_next_ref, x_ref, hb_ref, w8_ref, g2_ref, wsgu_ref, wsd_ref, fg_ref, y_hbm,
                    o_ref, ybuf, sem, *, final_norm):
    i = pl.program_id(0)
    n = pl.num_programs(0)
    slot = i % 2
    tm = COMBINE_TILE

    def gather(pos, dst_slot):
        for k in range(TOP_K):
            def body(r, carry, k=k):
                pltpu.make_async_copy(y_hbm.at[pl.ds(pos[0, k * tm + r], 1)],
                                      ybuf.at[dst_slot, k, pl.ds(r, 1)], sem.at[dst_slot]).start()
                return carry
            lax.fori_loop(0, tm, body, 0, unroll=GATHER_UNROLL)

    @pl.when(i == 0)
    def _():
        gather(pos_ref, 0)

    @pl.when(i + 1 < n)
    def _():
        gather(pos_next_ref, 1 - slot)

    hb = hb_ref[...]
    gu = jnp.dot(hb, wsgu_ref[...], preferred_element_type=F32)
    gate = gu[:, :D_EXPERT]
    act = (gate * jax.nn.sigmoid(gate) * gu[:, D_EXPERT:]).astype(BF16)
    shared = jnp.dot(act, wsd_ref[...], preferred_element_type=F32)

    for k in range(TOP_K):
        pltpu.make_async_copy(y_hbm.at[pl.ds(0, tm)], ybuf.at[slot, k], sem.at[slot]).wait()

    w8 = w8_ref[...]
    lo = jnp.zeros((tm, HALF_D), F32)
    hi = jnp.zeros((tm, HALF_D), F32)
    for k in range(TOP_K):
        y_lo, y_hi = _unpack_halves(ybuf[slot, k])
        wk = w8[:, k:k + 1]
        lo += wk * y_lo
        hi += wk * y_hi
    moe = jnp.concatenate([lo, hi], axis=1)
    out = x_ref[...] + g2_ref[...] * (shared + moe)
    if final_norm:
        out = out * lax.rsqrt(jnp.mean(out * out, axis=-1, keepdims=True) + EPS) * fg_ref[...]
    o_ref[...] = out


def _combine(x, hb, w8, mod3, row_g, pos, y, wsgu, wsd, final_g, final_norm, S):
    T, D = x.shape
    tm = COMBINE_TILE
    per = S // tm
    n = T // tm
    row = lambda i: (i, 0)
    fixed = lambda i: (0, 0)
    return pl.pallas_call(
        functools.partial(_combine_kernel, final_norm=final_norm),
        grid=(n,),
        in_specs=[
            pl.BlockSpec((None, 1, TOP_K * tm), lambda i: (i, 0, 0), memory_space=pltpu.SMEM),
            pl.BlockSpec((None, 1, TOP_K * tm), lambda i: (jnp.minimum(i + 1, n - 1), 0, 0),
                         memory_space=pltpu.SMEM),
            pl.BlockSpec((tm, D), row),
            pl.BlockSpec((tm, D), row),
            pl.BlockSpec((tm, TOP_K), row),
            pl.BlockSpec((None, 1, D), lambda i: (row_g(i // per), 0, 0)),
            pl.BlockSpec(wsgu.shape, fixed),
            pl.BlockSpec(wsd.shape, fixed),
            pl.BlockSpec((1, D), fixed),
            pl.BlockSpec(memory_space=pl.ANY),
        ],
        out_specs=pl.BlockSpec((tm, D), row),
        out_shape=jax.ShapeDtypeStruct((T, D), F32),
        scratch_shapes=[pltpu.VMEM((2, TOP_K, tm, HALF_D), U32), pltpu.SemaphoreType.DMA((2,))],
        compiler_params=_params("arbitrary"),
        name="moe_combine",
    )(pos, pos, x, hb, w8, mod3, wsgu, wsd, final_g.reshape(1, D), y)


def _swap_halves(w):
    half = w.shape[-1] // 2
    return jnp.concatenate([w[..., half:], w[..., :half]], axis=-1)


def _split_w_in(w_in):
    a_cols = 3 * A_WIDTH
    b_cols = B_Q_LORA + B_KV_LORA + B_ROPE
    c_cols = 2 * C_WIDTH
    wa = w_in[:, :a_cols].astype(BF16)
    wb_raw = w_in[:, a_cols:a_cols + b_cols]
    kpe = wb_raw[:, B_Q_LORA + B_KV_LORA:]
    zeros = jnp.zeros_like(kpe)
    wb = jnp.concatenate([wb_raw[:, :B_Q_LORA + B_KV_LORA], kpe, zeros, _swap_halves(kpe), zeros],
                         axis=1).astype(BF16)
    wc = w_in[:, a_cols + b_cols:a_cols + b_cols + c_cols].astype(BF16)
    wg = w_in[:, a_cols + b_cols + c_cols:].astype(BF16)
    return wa, wb, wc, wg


def _mla_weights(w_uq, w_ukv):
    r = w_uq.shape[0]
    wq = w_uq.reshape(r, B_HEADS, B_NOPE + B_ROPE)
    nope, pe = wq[..., :B_NOPE], wq[..., B_NOPE:]
    zeros = jnp.zeros_like(pe)
    main = jnp.concatenate([nope, pe, zeros], axis=-1).reshape(r, B_HEADS * B_QK_PAD)
    swapped = jnp.concatenate([_swap_halves(pe), zeros], axis=-1).reshape(r, B_HEADS * LANE)
    wq_ext = jnp.concatenate([main, swapped], axis=1).astype(BF16)
    rk = w_ukv.shape[0]
    wkv = w_ukv.reshape(rk, B_HEADS, B_NOPE + B_VDIM)
    wk = wkv[..., :B_NOPE].reshape(rk, -1).astype(BF16)
    wvt = wkv[..., B_NOPE:].reshape(rk, -1).T.astype(BF16)
    return wq_ext, wk, wvt


def kernel(x, c, positions, rel_bias, norm_mix, norm_ffn, w_mod, b_mod, w_in, mla_q_norm, mla_w_uq, mla_kv_norm, mla_w_ukv, gmlp_ln_g, gmlp_ln_b, gmlp_w_s, gmlp_b_s, w_branch, w_out, w_router, router_bias, we_gate, we_up, we_down, ws_gate, ws_up, ws_down, final_norm):
    B, S, D = x.shape
    T = B * S
    L = w_mod.shape[0]
    xt = x.reshape(T, D)

    mod = _modulation(c, w_mod, b_mod)
    mod3 = mod.reshape(L * B * 6, 1, D)
    cos, sin = _rope_tables(positions)
    a_bias = _mixer_a_bias(rel_bias)
    n_tiles = (T * TOP_K) // EXPERT_TILE + N_EXPERTS

    for l in range(L):
        row = lambda k: (lambda b, l=l, k=k: (l * B + b) * 6 + k)
        wa, wb, wc, wg = _split_w_in(w_in[l])
        wq_ext, wk_nope, wv_t = _mla_weights(mla_w_uq[l], mla_w_ukv[l])

        h = _norm_mod(xt, norm_mix[l], mod3, row(1), row(0), S)
        qkv = _matmul(h, wa, BF16, name="proj_a")
        bproj = _matmul(h, wb, F32, name="proj_b")
        z = _matmul(h, wc, BF16, act="gelu", name="proj_c")
        gates = _matmul(h, wg, BF16, act="sigmoid", name="proj_gates")

        y_a = _mixer_a(qkv, a_bias, B, S)
        qp, kp, vt = _mla_prep(bproj, cos, sin, mla_q_norm[l], mla_kv_norm[l], wq_ext, wk_nope, wv_t)
        y_b = _mla_attention(qp, kp, vt, B, S)
        y_c = _mixer_c(z, gmlp_ln_g[l], gmlp_ln_b[l], gmlp_w_s[l], gmlp_b_s[l])

        merged = _branch_merge(y_a, y_b, y_c, w_branch[l].astype(BF16), gates)
        xt = _matmul_residual(merged, w_out[l].astype(BF16), xt, mod3, row(2), S)

        hb, hpk, eidx, ew, chosen = _route(xt, norm_ffn[l], mod3, row(4), row(3), w_router[l], router_bias[l], S)
        row_token, tile_expert, n_used, dest = _dispatch_tables(eidx, chosen, n_tiles)
        y = _experts(hpk, row_token, tile_expert, n_used, we_gate, we_up, we_down, l)
        pos = dest.reshape(T // COMBINE_TILE, COMBINE_TILE, TOP_K).transpose(0, 2, 1)
        pos = pos.reshape(T // COMBINE_TILE, 1, TOP_K * COMBINE_TILE)
        wsgu = jnp.concatenate([ws_gate[l], ws_up[l]], axis=1).astype(BF16)
        xt = _combine(xt, hb, ew.T, mod3, row(5), pos, y, wsgu, ws_down[l].astype(BF16),
                      final_norm, l == L - 1, S)

    return xt.reshape(B, S, D)
```

```python
import functools

import jax
import jax.numpy as jnp
from jax import lax
from jax.experimental import pallas as pl
from jax.experimental.pallas import tpu as pltpu

F32 = jnp.float32
BF16 = jnp.bfloat16
U32 = jnp.uint32
I32 = jnp.int32

EPS = 1e-6
NEG_INF = -1e30

D_MODEL = 2048
HALF_D = D_MODEL // 2
CHUNK = 64

A_HEADS = 16
A_HEAD_DIM = 64
A_WIDTH = A_HEADS * A_HEAD_DIM
A_LEFT_CHUNKS = 8
MAX_REL = 128
A_QBLOCK = 256
A_KBLOCKS = 3

B_HEADS = 8
B_NOPE = 128
B_ROPE = 64
B_VDIM = 128
B_Q_LORA = 512
B_KV_LORA = 256
B_QK_PAD = 256
ROPE_THETA = 10000.0
B_TQ = 1024
B_TK = 512

C_BLOCK = 128
C_GROUPS = 8
C_WIDTH = 1024

N_EXPERTS = 64
EXPERT_GROUP = 8
TOP_K = 8
TOPK_GROUPS = 4
D_EXPERT = 512
ROUTED_SCALE = 2.5
EXPERT_TILE = 256
TOKEN_TILE = 256
ROW_ALIGN = 8
PERM_BLOCK = 256
COMBINE_BLOCK = 512

LANE = 128
V7X_VMEM_LIMIT = 56 * 1024 * 1024


def _params(*sem):
    return pltpu.CompilerParams(dimension_semantics=sem, vmem_limit_bytes=V7X_VMEM_LIMIT)


def _pack_halves(v):
    w = v.shape[1] // 2
    bits = lax.bitcast_convert_type(v.astype(BF16).astype(F32), U32)
    return (bits[:, :w] >> 16) | bits[:, w:]


def _unpack_halves(p):
    lo = lax.bitcast_convert_type(p << 16, F32)
    hi = lax.bitcast_convert_type(p & jnp.uint32(0xFFFF0000), F32)
    return lo, hi


def _mod_kernel(c_ref, w_ref, b_ref, o_ref):
    c = c_ref[...]
    ca = (c * jax.nn.sigmoid(c)).astype(BF16)
    o_ref[...] = jnp.dot(ca, w_ref[...].astype(BF16), preferred_element_type=F32) + b_ref[...]


def _modulation(c, w_mod, b_mod):
    L, D, N = w_mod.shape
    B = c.shape[0]
    tn = 512
    return pl.pallas_call(
        _mod_kernel,
        grid=(L, N // tn),
        in_specs=[
            pl.BlockSpec((B, D), lambda l, j: (0, 0)),
            pl.BlockSpec((None, D, tn), lambda l, j: (l, 0, j)),
            pl.BlockSpec((None, 1, tn), lambda l, j: (l, 0, j)),
        ],
        out_specs=pl.BlockSpec((None, B, tn), lambda l, j: (l, 0, j)),
        out_shape=jax.ShapeDtypeStruct((L, B, N), F32),
        compiler_params=_params("parallel", "arbitrary"),
        name="modulation",
    )(c, w_mod, b_mod.reshape(L, 1, N))


def _modulated_norm(x, g, sc, sh):
    r = lax.rsqrt(jnp.mean(x * x, axis=-1, keepdims=True) + EPS)
    return (x * r * g) * (1.0 + sc) + sh


def _norm_mod_kernel(x_ref, g_ref, sc_ref, sh_ref, o_ref):
    o_ref[...] = _modulated_norm(x_ref[...], g_ref[...], sc_ref[...], sh_ref[...]).astype(o_ref.dtype)


def _norm_mod(x, g, mod3, row_sc, row_sh, S, tm=512):
    T, D = x.shape
    per = S // tm
    return pl.pallas_call(
        _norm_mod_kernel,
        grid=(T // tm,),
        in_specs=[
            pl.BlockSpec((tm, D), lambda i: (i, 0)),
            pl.BlockSpec((1, D), lambda i: (0, 0)),
            pl.BlockSpec((None, 1, D), lambda i: (row_sc(i // per), 0, 0)),
            pl.BlockSpec((None, 1, D), lambda i: (row_sh(i // per), 0, 0)),
        ],
        out_specs=pl.BlockSpec((tm, D), lambda i: (i, 0)),
        out_shape=jax.ShapeDtypeStruct((T, D), BF16),
        compiler_params=_params("parallel"),
        name="norm_mod",
    )(x, g.reshape(1, D), mod3, mod3)


def _mm_kernel(a_ref, w_ref, o_ref, *, act):
    acc = jnp.dot(a_ref[...], w_ref[...], preferred_element_type=F32)
    if act == "gelu":
        acc = jax.nn.gelu(acc)
    elif act == "sigmoid":
        acc = jax.nn.sigmoid(acc)
    o_ref[...] = acc.astype(o_ref.dtype)


def _matmul(a, w, out_dtype, act=None, tm=1024, tn=1024, name="matmul"):
    M, K = a.shape
    N = w.shape[1]
    tm = min(tm, M)
    return pl.pallas_call(
        functools.partial(_mm_kernel, act=act),
        grid=(M // tm, N // tn),
        in_specs=[
            pl.BlockSpec((tm, K), lambda i, j: (i, 0)),
            pl.BlockSpec((K, tn), lambda i, j: (0, j)),
        ],
        out_specs=pl.BlockSpec((tm, tn), lambda i, j: (i, j)),
        out_shape=jax.ShapeDtypeStruct((M, N), out_dtype),
        compiler_params=_params("parallel", "arbitrary"),
        name=name,
    )(a, w)


def _mm_residual_kernel(a_ref, w_ref, x_ref, g_ref, o_ref):
    acc = jnp.dot(a_ref[...], w_ref[...], preferred_element_type=F32)
    o_ref[...] = x_ref[...] + g_ref[...] * acc


def _matmul_residual(a, w, x, mod3, row_g, S, tm=1024, tn=1024):
    M, K = a.shape
    N = w.shape[1]
    tm = min(tm, S)
    per = S // tm
    return pl.pallas_call(
        _mm_residual_kernel,
        grid=(M // tm, N // tn),
        in_specs=[
            pl.BlockSpec((tm, K), lambda i, j: (i, 0)),
            pl.BlockSpec((K, tn), lambda i, j: (0, j)),
            pl.BlockSpec((tm, tn), lambda i, j: (i, j)),
            pl.BlockSpec((None, 1, tn), lambda i, j: (row_g(i // per), 0, j)),
        ],
        out_specs=pl.BlockSpec((tm, tn), lambda i, j: (i, j)),
        out_shape=jax.ShapeDtypeStruct((M, N), F32),
        compiler_params=_params("parallel", "arbitrary"),
        name="out_proj_residual",
    )(a, w, x, mod3)


def _mixer_a_kernel(q_ref, k0_ref, k1_ref, k2_ref, v0_ref, v1_ref, v2_ref, bias_ref, o_ref, *, scale):
    i = pl.program_id(2)
    qb = A_QBLOCK
    kw = A_KBLOCKS * qb
    q = q_ref[...]
    k = jnp.concatenate([k0_ref[...], k1_ref[...], k2_ref[...]], axis=0)
    v = jnp.concatenate([v0_ref[...], v1_ref[...], v2_ref[...]], axis=0)
    lane = lax.broadcasted_iota(I32, (qb, LANE), 1)
    kpos = (i - (A_KBLOCKS - 1)) * qb + lax.broadcasted_iota(I32, (1, kw), 1)
    in_seq = kpos >= 0
    outs = []
    for hh in range(2):
        head_lanes = (lane < A_HEAD_DIM) if hh == 0 else (lane >= A_HEAD_DIM)
        qm = jnp.where(head_lanes, q, jnp.zeros_like(q)) * jnp.asarray(scale, q.dtype)
        s = lax.dot_general(qm, k, (((1,), (1,)), ((), ())), preferred_element_type=F32)
        s = s + bias_ref[hh]
        s = jnp.where(in_seq, s, NEG_INF)
        m = jnp.max(s, axis=-1, keepdims=True)
        p = jnp.exp(s - m)
        l = jnp.sum(p, axis=-1, keepdims=True)
        o = jnp.dot(p.astype(BF16), v, preferred_element_type=F32)
        outs.append(o / l)
    o_ref[...] = jnp.where(lane < A_HEAD_DIM, outs[0], outs[1]).astype(o_ref.dtype)


def _mixer_a_bias(rel_bias):
    qb = A_QBLOCK
    kw = A_KBLOCKS * qb
    qi = jnp.arange(qb)[:, None]
    kj = jnp.arange(kw)[None, :]
    qc = qi // CHUNK
    kc = kj // CHUNK
    first = (A_KBLOCKS - 1) * qb // CHUNK - A_LEFT_CHUNKS
    in_band = (kc >= qc + first) & (kc <= qc + first + A_LEFT_CHUNKS)
    period = kw + qb
    m = jnp.arange(period)
    k_minus_q = jnp.where(m < kw, m, m - period)
    d = (A_KBLOCKS - 1) * qb - k_minus_q
    line = rel_bias[:, jnp.clip(d, -MAX_REL, MAX_REL) + MAX_REL].astype(F32)
    heads = rel_bias.shape[0]
    flat = jnp.broadcast_to(line[:, None, :], (heads, qb, period)).reshape(heads, qb * period)
    bias = flat[:, :qb * (period - 1)].reshape(heads, qb, period - 1)[:, :, :kw]
    return jnp.where(in_band[None], bias, NEG_INF)


def _mixer_a(qkv, bias, B, S):
    T = qkv.shape[0]
    qb = A_QBLOCK
    nq = S // qb
    pairs = A_HEADS // 2

    def kv_spec(j, col0):
        return pl.BlockSpec(
            (qb, LANE), lambda p, b, i: (b * nq + jnp.maximum(i - (A_KBLOCKS - 1) + j, 0), col0 + p))

    return pl.pallas_call(
        functools.partial(_mixer_a_kernel, scale=A_HEAD_DIM ** -0.5),
        grid=(pairs, B, nq),
        in_specs=[
            pl.BlockSpec((qb, LANE), lambda p, b, i: (b * nq + i, p)),
            kv_spec(0, pairs), kv_spec(1, pairs), kv_spec(2, pairs),
            kv_spec(0, 2 * pairs), kv_spec(1, 2 * pairs), kv_spec(2, 2 * pairs),
            pl.BlockSpec((2, qb, A_KBLOCKS * qb), lambda p, b, i: (p, 0, 0)),
        ],
        out_specs=pl.BlockSpec((qb, LANE), lambda p, b, i: (b * nq + i, p)),
        out_shape=jax.ShapeDtypeStruct((T, A_WIDTH), BF16),
        compiler_params=_params("parallel", "parallel", "arbitrary"),
        name="mixer_a",
    )(qkv, qkv, qkv, qkv, qkv, qkv, qkv, bias)


def _rope_table_kernel(pos_ref, inv_ref, c_ref, s_ref):
    ang = pos_ref[...].astype(F32) * inv_ref[...]
    lane = lax.broadcasted_iota(I32, ang.shape, 1)
    first_half = (lane % B_ROPE) < (B_ROPE // 2)
    c_ref[...] = jnp.cos(ang)
    sn = jnp.sin(ang)
    s_ref[...] = jnp.where(first_half, -sn, sn)


def _rope_tables(positions, tm=512):
    T = positions.size
    half = B_ROPE // 2
    inv = ROPE_THETA ** (-jnp.arange(half, dtype=F32) / half)
    inv = jnp.tile(inv, LANE // half).reshape(1, LANE)
    return pl.pallas_call(
        _rope_table_kernel,
        grid=(T // tm,),
        in_specs=[pl.BlockSpec((tm, 1), lambda i: (i, 0)), pl.BlockSpec((1, LANE), lambda i: (0, 0))],
        out_specs=[pl.BlockSpec((tm, LANE), lambda i: (i, 0))] * 2,
        out_shape=[jax.ShapeDtypeStruct((T, LANE), F32)] * 2,
        compiler_params=_params("parallel"),
        name="rope_tables",
    )(positions.reshape(T, 1), inv)


def _mla_prep_kernel(bp_ref, c_ref, s_ref, qn_ref, kvn_ref, wq_ref, wk_ref, wvt_ref, q_out, k_out, vt_out, *, scale):
    bp = bp_ref[...]
    cq = bp[:, :B_Q_LORA]
    ckv = bp[:, B_Q_LORA:B_Q_LORA + B_KV_LORA]
    kpe = bp[:, B_Q_LORA + B_KV_LORA:B_Q_LORA + B_KV_LORA + LANE]
    kpe_sw = bp[:, B_Q_LORA + B_KV_LORA + LANE:]
    cqn = cq * lax.rsqrt(jnp.mean(cq * cq, axis=-1, keepdims=True) + EPS) * qn_ref[...]
    ckvn = ckv * lax.rsqrt(jnp.mean(ckv * ckv, axis=-1, keepdims=True) + EPS) * kvn_ref[...]
    qa = jnp.dot(cqn.astype(BF16), wq_ref[...], preferred_element_type=F32)
    ckvb = ckvn.astype(BF16)
    kn = jnp.dot(ckvb, wk_ref[...], preferred_element_type=F32)
    vt_out[...] = lax.dot_general(wvt_ref[...], ckvb, (((1,), (1,)), ((), ())),
                                  preferred_element_type=F32).astype(BF16)
    cos = c_ref[...]
    sin = s_ref[...]
    k_rot = (kpe * cos + kpe_sw * sin).astype(BF16)
    sw0 = B_HEADS * B_QK_PAD
    for h in range(B_HEADS):
        c0 = h * B_QK_PAD
        q_out[:, c0:c0 + B_NOPE] = (qa[:, c0:c0 + B_NOPE] * scale).astype(BF16)
        pe = qa[:, c0 + B_NOPE:c0 + B_QK_PAD]
        pe_sw = qa[:, sw0 + h * LANE:sw0 + (h + 1) * LANE]
        q_out[:, c0 + B_NOPE:c0 + B_QK_PAD] = ((pe * cos + pe_sw * sin) * scale).astype(BF16)
        k_out[:, c0:c0 + B_NOPE] = kn[:, h * B_NOPE:(h + 1) * B_NOPE].astype(BF16)
        k_out[:, c0 + B_NOPE:c0 + B_QK_PAD] = k_rot


def _mla_prep(bproj, cos, sin, q_norm, kv_norm, wq, wk, wvt, tm=256):
    T = bproj.shape[0]
    scale = (B_NOPE + B_ROPE) ** -0.5
    qk_w = B_HEADS * B_QK_PAD
    v_w = B_HEADS * B_VDIM
    row = lambda i: (i, 0)
    fixed = lambda i: (0, 0)
    return pl.pallas_call(
        functools.partial(_mla_prep_kernel, scale=scale),
        grid=(T // tm,),
        in_specs=[
            pl.BlockSpec((tm, bproj.shape[1]), row),
            pl.BlockSpec((tm, LANE), row),
            pl.BlockSpec((tm, LANE), row),
            pl.BlockSpec((1, B_Q_LORA), fixed),
            pl.BlockSpec((1, B_KV_LORA), fixed),
            pl.BlockSpec(wq.shape, fixed),
            pl.BlockSpec(wk.shape, fixed),
            pl.BlockSpec(wvt.shape, fixed),
        ],
        out_specs=[pl.BlockSpec((tm, qk_w), row), pl.BlockSpec((tm, qk_w), row),
                   pl.BlockSpec((v_w, tm), lambda i: (0, i))],
        out_shape=[jax.ShapeDtypeStruct((T, qk_w), BF16), jax.ShapeDtypeStruct((T, qk_w), BF16),
                   jax.ShapeDtypeStruct((v_w, T), BF16)],
        compiler_params=_params("parallel"),
        name="mla_prep",
    )(bproj, cos, sin, q_norm.reshape(1, -1), kv_norm.reshape(1, -1), wq, wk, wvt)


def _mla_attn_kernel(qt_ref, kt_ref, q_ref, k_ref, vt_ref, o_ref, m_sc, l_sc, acc_sc, *, tq, tk):
    p = pl.program_id(2)
    qi = qt_ref[p]
    kj = kt_ref[p]

    @pl.when(kj == 0)
    def _():
        m_sc[...] = jnp.full_like(m_sc, NEG_INF)
        l_sc[...] = jnp.zeros_like(l_sc)
        acc_sc[...] = jnp.zeros_like(acc_sc)

    def online_softmax_step(s):
        m_prev = m_sc[...]
        m_new = jnp.maximum(m_prev, jnp.max(s, axis=0, keepdims=True))
        alpha = jnp.exp(m_prev - m_new)
        pr = jnp.exp(s - m_new)
        l_sc[...] = alpha * l_sc[...] + jnp.sum(pr, axis=0, keepdims=True)
        acc_sc[...] = alpha * acc_sc[...] + jnp.dot(vt_ref[...], pr.astype(BF16), preferred_element_type=F32)
        m_sc[...] = m_new

    def scores():
        return lax.dot_general(k_ref[...], q_ref[...], (((1,), (1,)), ((), ())), preferred_element_type=F32)

    k_end = (kj + 1) * tk
    crosses_diagonal = k_end > qi * tq

    @pl.when(jnp.logical_not(crosses_diagonal))
    def _():
        online_softmax_step(scores())

    @pl.when(crosses_diagonal)
    def _():
        s = scores()
        kc = (kj * tk + lax.broadcasted_iota(I32, s.shape, 0)) // CHUNK
        qc = (qi * tq + lax.broadcasted_iota(I32, s.shape, 1)) // CHUNK
        online_softmax_step(jnp.where(kc <= qc, s, NEG_INF))

    @pl.when(k_end == (qi + 1) * tq)
    def _():
        o_ref[...] = (acc_sc[...] / l_sc[...]).T.astype(o_ref.dtype)


def _mla_attention(qp, kp, vt, B, S):
    T = qp.shape[0]
    tq = min(B_TQ, S)
    tk = min(B_TK, S)
    nq = S // tq
    nk = S // tk
    pairs = [(qi, kj) for qi in range(nq) for kj in range((qi + 1) * tq // tk)]
    qt = jnp.asarray([p[0] for p in pairs], I32)
    kt = jnp.asarray([p[1] for p in pairs], I32)
    grid_spec = pltpu.PrefetchScalarGridSpec(
        num_scalar_prefetch=2,
        grid=(B, B_HEADS, len(pairs)),
        in_specs=[
            pl.BlockSpec((tq, B_QK_PAD), lambda b, h, p, qt, kt: (b * nq + qt[p], h)),
            pl.BlockSpec((tk, B_QK_PAD), lambda b, h, p, qt, kt: (b * nk + kt[p], h)),
            pl.BlockSpec((B_VDIM, tk), lambda b, h, p, qt, kt: (h, b * nk + kt[p])),
        ],
        out_specs=pl.BlockSpec((tq, B_VDIM), lambda b, h, p, qt, kt: (b * nq + qt[p], h)),
        scratch_shapes=[pltpu.VMEM((1, tq), F32), pltpu.VMEM((1, tq), F32), pltpu.VMEM((B_VDIM, tq), F32)],
    )
    return pl.pallas_call(
        functools.partial(_mla_attn_kernel, tq=tq, tk=tk),
        grid_spec=grid_spec,
        out_shape=jax.ShapeDtypeStruct((T, B_HEADS * B_VDIM), BF16),
        compiler_params=_params("parallel", "parallel", "arbitrary"),
        name="mla_attention",
    )(qt, kt, qp, kp, vt)


def _mixer_c_kernel(z_ref, lg_ref, lb_ref, ws_ref, bs_ref, o_ref):
    tm = z_ref.shape[0]
    u = z_ref[:, :C_WIDTH].astype(F32)
    v = z_ref[:, C_WIDTH:].astype(F32)
    mu = jnp.mean(v, axis=-1, keepdims=True)
    var = jnp.mean(jnp.square(v - mu), axis=-1, keepdims=True)
    vn = ((v - mu) * lax.rsqrt(var + EPS) * lg_ref[...] + lb_ref[...]).astype(BF16)
    row = lax.broadcasted_iota(I32, (C_BLOCK, C_BLOCK), 0)
    col = lax.broadcasted_iota(I32, (C_BLOCK, C_BLOCK), 1)
    causal = col <= row
    gd = C_WIDTH // C_GROUPS
    for g in range(C_GROUPS):
        w = jnp.where(causal, ws_ref[g], 0.0).astype(BF16)
        b = bs_ref[:, g:g + 1]
        for n in range(tm // C_BLOCK):
            rows = slice(n * C_BLOCK, (n + 1) * C_BLOCK)
            cols = slice(g * gd, (g + 1) * gd)
            mixed = jnp.dot(w, vn[rows, cols], preferred_element_type=F32) + b
            o_ref[rows, cols] = (u[rows, cols] * mixed).astype(o_ref.dtype)


def _mixer_c(z, ln_g, ln_b, w_s, b_s, tm=256):
    T = z.shape[0]
    return pl.pallas_call(
        _mixer_c_kernel,
        grid=(T // tm,),
        in_specs=[
            pl.BlockSpec((tm, 2 * C_WIDTH), lambda i: (i, 0)),
            pl.BlockSpec((1, C_WIDTH), lambda i: (0, 0)),
            pl.BlockSpec((1, C_WIDTH), lambda i: (0, 0)),
            pl.BlockSpec((C_GROUPS, C_BLOCK, C_BLOCK), lambda i: (0, 0, 0)),
            pl.BlockSpec((C_BLOCK, C_GROUPS), lambda i: (0, 0)),
        ],
        out_specs=pl.BlockSpec((tm, C_WIDTH), lambda i: (i, 0)),
        out_shape=jax.ShapeDtypeStruct((T, C_WIDTH), BF16),
        compiler_params=_params("parallel"),
        name="mixer_c",
    )(z, ln_g.reshape(1, -1), ln_b.reshape(1, -1), w_s, b_s.T)


def _branch_kernel(ya_ref, yb_ref, yc_ref, wb_ref, ga_ref, gb_ref, gc_ref, o_ref):
    acc = ga_ref[...].astype(F32) * jnp.dot(ya_ref[...], wb_ref[0], preferred_element_type=F32)
    acc += gb_ref[...].astype(F32) * jnp.dot(yb_ref[...], wb_ref[1], preferred_element_type=F32)
    acc += gc_ref[...].astype(F32) * jnp.dot(yc_ref[...], wb_ref[2], preferred_element_type=F32)
    o_ref[...] = acc.astype(o_ref.dtype)


def _branch_merge(ya, yb, yc, wb, gates, tm=1024, tn=512):
    T, W = ya.shape
    D = wb.shape[2]
    tm = min(tm, T)
    nj = D // tn
    y_spec = pl.BlockSpec((tm, W), lambda i, j: (i, 0))
    return pl.pallas_call(
        _branch_kernel,
        grid=(T // tm, nj),
        in_specs=[
            y_spec, y_spec, y_spec,
            pl.BlockSpec((3, W, tn), lambda i, j: (0, 0, j)),
            pl.BlockSpec((tm, tn), lambda i, j: (i, j)),
            pl.BlockSpec((tm, tn), lambda i, j: (i, nj + j)),
            pl.BlockSpec((tm, tn), lambda i, j: (i, 2 * nj + j)),
        ],
        out_specs=pl.BlockSpec((tm, tn), lambda i, j: (i, j)),
        out_shape=jax.ShapeDtypeStruct((T, D), BF16),
        compiler_params=_params("parallel", "arbitrary"),
        name="branch_merge",
    )(ya, yb, yc, wb, gates, gates, gates)


def _route_kernel(x_ref, g_ref, sc_ref, sh_ref, wrh_ref, wrl_ref, rb_ref,
                  hb_ref, ew_ref, lpos_ref, cnt_ref):
    tm = x_ref.shape[0]
    h = _modulated_norm(x_ref[...], g_ref[...], sc_ref[...], sh_ref[...])
    hb = h.astype(BF16)
    hb_ref[...] = hb
    h_lo = (h - hb.astype(F32)).astype(BF16)
    dn = (((1,), (1,)), ((), ()))
    logits = (lax.dot_general(wrh_ref[...], hb, dn, preferred_element_type=F32)
              + lax.dot_general(wrh_ref[...], h_lo, dn, preferred_element_type=F32)
              + lax.dot_general(wrl_ref[...], hb, dn, preferred_element_type=F32))
    scores = jax.nn.sigmoid(logits)
    sel = scores + rb_ref[...]
    neg = -jnp.inf
    n_groups = N_EXPERTS // EXPERT_GROUP

    i8 = lax.broadcasted_iota(I32, (EXPERT_GROUP, tm), 0)
    group_scores = []
    for g in range(n_groups):
        blk = sel[g * EXPERT_GROUP:(g + 1) * EXPERT_GROUP, :]
        m1 = jnp.max(blk, axis=0, keepdims=True)
        first = jnp.min(jnp.where(blk == m1, i8, EXPERT_GROUP), axis=0, keepdims=True)
        m2 = jnp.max(jnp.where(i8 == first, neg, blk), axis=0, keepdims=True)
        group_scores.append(m1 + m2)
    gs = jnp.concatenate(group_scores, axis=0)

    ig = lax.broadcasted_iota(I32, (n_groups, tm), 0)
    gsel = jnp.zeros((n_groups, tm), F32)
    for _ in range(TOPK_GROUPS):
        m = jnp.max(gs, axis=0, keepdims=True)
        first = jnp.min(jnp.where(gs == m, ig, n_groups), axis=0, keepdims=True)
        hit = ig == first
        gsel = jnp.where(hit, 1.0, gsel)
        gs = jnp.where(hit, neg, gs)

    masked = jnp.concatenate(
        [jnp.where(gsel[g:g + 1, :] > 0.0, sel[g * EXPERT_GROUP:(g + 1) * EXPERT_GROUP, :], neg)
         for g in range(n_groups)], axis=0)

    ie = lax.broadcasted_iota(I32, (N_EXPERTS, tm), 0)
    chosen = jnp.zeros((N_EXPERTS, tm), F32)
    hits, ws = [], []
    for _ in range(TOP_K):
        m = jnp.max(masked, axis=0, keepdims=True)
        first = jnp.min(jnp.where(masked == m, ie, N_EXPERTS), axis=0, keepdims=True)
        hit = ie == first
        ws.append(jnp.sum(jnp.where(hit, scores, 0.0), axis=0, keepdims=True))
        hits.append(hit)
        chosen = jnp.where(hit, 1.0, chosen)
        masked = jnp.where(hit, neg, masked)
    w = jnp.concatenate(ws, axis=0)
    denom = jnp.sum(w, axis=0, keepdims=True) + 1e-20
    ew_ref[...] = w / denom * ROUTED_SCALE

    s_idx = lax.broadcasted_iota(I32, (tm, tm), 0)
    t_idx = lax.broadcasted_iota(I32, (tm, tm), 1)
    earlier = jnp.where(s_idx < t_idx, 1.0, 0.0).astype(BF16)
    local_rank = jnp.dot(chosen.astype(BF16), earlier, preferred_element_type=F32)
    n_e = jnp.sum(chosen, axis=1, keepdims=True)
    run = jnp.floor((n_e + (ROW_ALIGN - 1.0)) * (1.0 / ROW_ALIGN)) * ROW_ALIGN
    run_b = jnp.broadcast_to(run, (N_EXPERTS, LANE))
    e_row = lax.broadcasted_iota(I32, (N_EXPERTS, N_EXPERTS), 0)
    e_col = lax.broadcasted_iota(I32, (N_EXPERTS, N_EXPERTS), 1)
    before = jnp.where(e_col < e_row, 1.0, 0.0).astype(BF16)
    run_start = jnp.dot(before, run_b.astype(BF16), preferred_element_type=F32)
    base = run_start[:, :1] + local_rank
    lpos = [jnp.sum(jnp.where(hit, base, 0.0), axis=0, keepdims=True) for hit in hits]
    lpos_ref[...] = jnp.concatenate(lpos, axis=0).astype(I32)
    cnt_ref[...] = run_b.astype(I32)


def _route(x, g, mod3, row_sc, row_sh, w_router, router_bias, S):
    T, D = x.shape
    tm = min(TOKEN_TILE, S)
    per = S // tm
    wr_t = w_router.T
    wr_hi = wr_t.astype(BF16)
    wr_lo = (wr_t - wr_hi.astype(F32)).astype(BF16)
    row = lambda i: (i, 0)
    col = lambda i: (0, i)
    fixed = lambda i: (0, 0)
    return pl.pallas_call(
        _route_kernel,
        grid=(T // tm,),
        in_specs=[
            pl.BlockSpec((tm, D), row),
            pl.BlockSpec((1, D), fixed),
            pl.BlockSpec((None, 1, D), lambda i: (row_sc(i // per), 0, 0)),
            pl.BlockSpec((None, 1, D), lambda i: (row_sh(i // per), 0, 0)),
            pl.BlockSpec((N_EXPERTS, D), fixed),
            pl.BlockSpec((N_EXPERTS, D), fixed),
            pl.BlockSpec((N_EXPERTS, 1), fixed),
        ],
        out_specs=[
            pl.BlockSpec((tm, D), row),
            pl.BlockSpec((TOP_K, tm), col),
            pl.BlockSpec((TOP_K, tm), col),
            pl.BlockSpec((N_EXPERTS, LANE), col),
        ],
        out_shape=[
            jax.ShapeDtypeStruct((T, D), BF16),
            jax.ShapeDtypeStruct((TOP_K, T), F32),
            jax.ShapeDtypeStruct((TOP_K, T), I32),
            jax.ShapeDtypeStruct((N_EXPERTS, (T // tm) * LANE), I32),
        ],
        compiler_params=_params("parallel"),
        name="norm_route",
    )(x, g.reshape(1, D), mod3, mod3, wr_hi, wr_lo, router_bias.reshape(N_EXPERTS, 1))


def _moe_tables(cnt, n_tiles):
    runs = cnt[:, ::LANE].T
    total = jnp.sum(runs, axis=0)
    region = (total + EXPERT_TILE - 1) // EXPERT_TILE * EXPERT_TILE
    region_end = jnp.cumsum(region)
    region_start = region_end - region
    dst = region_start[None, :] + jnp.cumsum(runs, axis=0) - runs
    src = jnp.cumsum(runs, axis=1) - runs
    tile_end = region_end // EXPERT_TILE
    tile_expert = jnp.sum(tile_end[None, :] <= jnp.arange(n_tiles, dtype=I32)[:, None], axis=1)
    tile_expert = jnp.minimum(tile_expert, N_EXPERTS - 1).astype(I32)
    as_i32 = lambda a: a.reshape(-1).astype(I32)
    gap_start = jnp.concatenate([region_start + total, region_end[-1:]])
    gap_rows = jnp.concatenate([region - total, n_tiles * EXPERT_TILE - region_end[-1:]])
    return dict(dst=as_i32(dst), src=as_i32(src), chunks=as_i32(runs // ROW_ALIGN),
                gap_start=as_i32(gap_start), gap_chunks=as_i32(gap_rows // ROW_ALIGN),
                tile_expert=tile_expert, n_used=as_i32(tile_end[-1:]))


def _run_copies(i, src_ref, dst_ref, chunks_ref, make_copy, wait):
    for e in range(N_EXPERTS):
        s0 = src_ref[i * N_EXPERTS + e]
        d0 = dst_ref[i * N_EXPERTS + e]

        def body(c, carry, s0=s0, d0=d0):
            cp = make_copy(pl.multiple_of(s0 + c * ROW_ALIGN, ROW_ALIGN), pl.multiple_of(d0 + c * ROW_ALIGN, ROW_ALIGN))
            cp.wait() if wait else cp.start()
            return carry
        lax.fori_loop(0, chunks_ref[i * N_EXPERTS + e], body, 0)


def _dispatch_kernel(src_ref, dst_ref, chunks_ref, ts_ref, tc_ref, hb_ref, lp_ref, xs_hbm, xloc, zrows, sem):
    i = pl.program_id(0)
    tt = hb_ref.shape[0]

    @pl.when(i == 0)
    def _():
        zrows[...] = jnp.zeros_like(zrows)
        for wait in (False, True):
            for e in range(N_EXPERTS + 1):
                def body(c, carry, e=e, wait=wait):
                    cp = pltpu.make_async_copy(
                        zrows, xs_hbm.at[pl.ds(pl.multiple_of(ts_ref[e] + c * ROW_ALIGN, ROW_ALIGN), ROW_ALIGN)], sem)
                    cp.wait() if wait else cp.start()
                    return carry
                lax.fori_loop(0, tc_ref[e], body, 0)

    lp = lp_ref[...]
    hbv = hb_ref[...]

    def order_block(rb, carry):
        r0 = pl.multiple_of(rb * PERM_BLOCK, PERM_BLOCK)
        rows = r0 + lax.broadcasted_iota(I32, (PERM_BLOCK, tt), 0)
        onehot = jnp.zeros((PERM_BLOCK, tt), F32)
        for k in range(TOP_K):
            onehot = jnp.where(rows == lp[k:k + 1, :], 1.0, onehot)
        xb = jnp.dot(onehot.astype(BF16), hbv, preferred_element_type=F32)
        xloc[pl.ds(r0, PERM_BLOCK), :] = _pack_halves(xb)
        return carry
    lax.fori_loop(0, xloc.shape[0] // PERM_BLOCK, order_block, 0)

    def make_copy(s, d):
        return pltpu.make_async_copy(xloc.at[pl.ds(s, ROW_ALIGN)], xs_hbm.at[pl.ds(d, ROW_ALIGN)], sem)
    _run_copies(i, src_ref, dst_ref, chunks_ref, make_copy, wait=False)
    _run_copies(i, src_ref, dst_ref, chunks_ref, make_copy, wait=True)


def _local_rows(tt):
    rows = TOP_K * tt + N_EXPERTS * ROW_ALIGN
    return (rows + COMBINE_BLOCK - 1) // COMBINE_BLOCK * COMBINE_BLOCK


def _dispatch(hb, lpos, tables, n_rows, S):
    T, D = hb.shape
    tt = min(TOKEN_TILE, S)
    grid_spec = pltpu.PrefetchScalarGridSpec(
        num_scalar_prefetch=5,
        grid=(T // tt,),
        in_specs=[
            pl.BlockSpec((tt, D), lambda i, *_: (i, 0)),
            pl.BlockSpec((TOP_K, tt), lambda i, *_: (0, i)),
        ],
        out_specs=pl.BlockSpec(memory_space=pl.ANY),
        scratch_shapes=[pltpu.VMEM((_local_rows(tt), HALF_D), U32), pltpu.VMEM((ROW_ALIGN, HALF_D), U32),
                        pltpu.SemaphoreType.DMA(())],
    )
    return pl.pallas_call(
        _dispatch_kernel,
        grid_spec=grid_spec,
        out_shape=jax.ShapeDtypeStruct((n_rows, HALF_D), U32),
        compiler_params=_params("arbitrary"),
        name="moe_dispatch",
    )(tables["src"], tables["dst"], tables["chunks"], tables["gap_start"], tables["gap_chunks"], hb, lpos)


def _expert_kernel(te_ref, nu_ref, x_ref, wg_ref, wu_ref, wd_ref, y_ref, wgu_bf, wd_bf):
    i = pl.program_id(0)

    @pl.when(i < nu_ref[0])
    def _():
        @pl.when((i == 0) | (te_ref[i] != te_ref[jnp.maximum(i - 1, 0)]))
        def _():
            wgu_bf[:, :D_EXPERT] = wg_ref[...].astype(BF16)
            wgu_bf[:, D_EXPERT:] = wu_ref[...].astype(BF16)
            wd_bf[...] = wd_ref[...].astype(BF16)

        x_lo, x_hi = _unpack_halves(x_ref[...])
        gu = jnp.dot(x_lo.astype(BF16), wgu_bf[:HALF_D, :], preferred_element_type=F32)
        gu += jnp.dot(x_hi.astype(BF16), wgu_bf[HALF_D:, :], preferred_element_type=F32)
        gate = gu[:, :D_EXPERT]
        act = (gate * jax.nn.sigmoid(gate) * gu[:, D_EXPERT:]).astype(BF16)
        y = jnp.dot(act, wd_bf[...], preferred_element_type=F32)
        y_ref[...] = _pack_halves(y)

    @pl.when(i >= nu_ref[0])
    def _():
        y_ref[...] = jnp.zeros_like(y_ref)


def _experts(xs, tables, we_gate, we_up, we_down, layer):
    tm = EXPERT_TILE
    n_tiles = xs.shape[0] // tm
    used = lambda i, nu: jnp.minimum(i, nu[0] - 1)
    expert_block = lambda i, te, nu: (layer, te[used(i, nu)], 0, 0)
    row_block = lambda i, te, nu: (used(i, nu), 0)
    grid_spec = pltpu.PrefetchScalarGridSpec(
        num_scalar_prefetch=2,
        grid=(n_tiles,),
        in_specs=[
            pl.BlockSpec((tm, HALF_D), row_block),
            pl.BlockSpec((None, None, D_MODEL, D_EXPERT), expert_block),
            pl.BlockSpec((None, None, D_MODEL, D_EXPERT), expert_block),
            pl.BlockSpec((None, None, D_EXPERT, D_MODEL), expert_block),
        ],
        out_specs=pl.BlockSpec((tm, HALF_D), lambda i, te, nu: (i, 0)),
        scratch_shapes=[pltpu.VMEM((D_MODEL, 2 * D_EXPERT), BF16), pltpu.VMEM((D_EXPERT, D_MODEL), BF16)],
    )
    return pl.pallas_call(
        _expert_kernel,
        grid_spec=grid_spec,
        out_shape=jax.ShapeDtypeStruct(xs.shape, U32),
        compiler_params=_params("arbitrary"),
        name="experts",
    )(tables["tile_expert"], tables["n_used"], xs, we_gate, we_up, we_down)


def _shared_expert_kernel(h_ref, wgu_ref, wd_ref, o_ref):
    gu = jnp.dot(h_ref[...], wgu_ref[...], preferred_element_type=F32)
    gate = gu[:, :D_EXPERT]
    act = (gate * jax.nn.sigmoid(gate) * gu[:, D_EXPERT:]).astype(BF16)
    o_ref[...] = jnp.dot(act, wd_ref[...], preferred_element_type=F32).astype(o_ref.dtype)


def _shared_expert(hb, wsgu, wsd, tm=512):
    T, D = hb.shape
    return pl.pallas_call(
        _shared_expert_kernel,
        grid=(T // tm,),
        in_specs=[pl.BlockSpec((tm, D), lambda i: (i, 0)), pl.BlockSpec(wsgu.shape, lambda i: (0, 0)),
                  pl.BlockSpec(wsd.shape, lambda i: (0, 0))],
        out_specs=pl.BlockSpec((tm, D), lambda i: (i, 0)),
        out_shape=jax.ShapeDtypeStruct((T, D), BF16),
        compiler_params=_params("parallel"),
        name="shared_expert",
    )(hb, wsgu, wsd)


def _combine_kernel(src_ref, dst_ref, chunks_ref, lp_ref, ew_ref, x_ref, sh_ref, g2_ref, fg_ref, ys_hbm,
                    o_ref, yloc, acc, sem, *, final_norm):
    i = pl.program_id(0)
    tt = x_ref.shape[0]

    @pl.when(i == 0)
    def _():
        yloc[...] = jnp.zeros_like(yloc)

    def make_copy(s, d):
        return pltpu.make_async_copy(ys_hbm.at[pl.ds(d, ROW_ALIGN)], yloc.at[pl.ds(s, ROW_ALIGN)], sem)
    _run_copies(i, src_ref, dst_ref, chunks_ref, make_copy, wait=False)
    _run_copies(i, src_ref, dst_ref, chunks_ref, make_copy, wait=True)

    lp = lp_ref[...]
    ew = ew_ref[...]
    acc[...] = jnp.zeros_like(acc)

    def sum_block(rb, carry):
        r0 = pl.multiple_of(rb * COMBINE_BLOCK, COMBINE_BLOCK)
        rows = r0 + lax.broadcasted_iota(I32, (COMBINE_BLOCK, tt), 0)
        wmat = jnp.zeros((COMBINE_BLOCK, tt), F32)
        for k in range(TOP_K):
            wmat = jnp.where(rows == lp[k:k + 1, :], ew[k:k + 1, :], wmat)
        w_row = jnp.sum(wmat, axis=1, keepdims=True)
        onehot = jnp.where(wmat != 0.0, 1.0, 0.0).astype(BF16)
        y_lo, y_hi = _unpack_halves(yloc[pl.ds(r0, COMBINE_BLOCK), :])
        dn = (((0,), (0,)), ((), ()))
        acc[:, :HALF_D] += lax.dot_general(onehot, (w_row * y_lo).astype(BF16), dn, preferred_element_type=F32)
        acc[:, HALF_D:] += lax.dot_general(onehot, (w_row * y_hi).astype(BF16), dn, preferred_element_type=F32)
        return carry
    lax.fori_loop(0, yloc.shape[0] // COMBINE_BLOCK, sum_block, 0)

    out = x_ref[...] + g2_ref[...] * (sh_ref[...].astype(F32) + acc[...])
    if final_norm:
        out = out * lax.rsqrt(jnp.mean(out * out, axis=-1, keepdims=True) + EPS) * fg_ref[...]
    o_ref[...] = out


def _combine(x, shared, lpos, ew, mod3, row_g, tables, ys, final_g, final_norm, S):
    T, D = x.shape
    tt = min(TOKEN_TILE, S)
    per = S // tt
    row = lambda i, *_: (i, 0)
    col = lambda i, *_: (0, i)
    grid_spec = pltpu.PrefetchScalarGridSpec(
        num_scalar_prefetch=3,
        grid=(T // tt,),
        in_specs=[
            pl.BlockSpec((TOP_K, tt), col),
            pl.BlockSpec((TOP_K, tt), col),
            pl.BlockSpec((tt, D), row),
            pl.BlockSpec((tt, D), row),
            pl.BlockSpec((None, 1, D), lambda i, *_: (row_g(i // per), 0, 0)),
            pl.BlockSpec((1, D), lambda i, *_: (0, 0)),
            pl.BlockSpec(memory_space=pl.ANY),
        ],
        out_specs=pl.BlockSpec((tt, D), row),
        scratch_shapes=[pltpu.VMEM((_local_rows(tt), HALF_D), U32), pltpu.VMEM((tt, D), F32),
                        pltpu.SemaphoreType.DMA(())],
    )
    return pl.pallas_call(
        functools.partial(_combine_kernel, final_norm=final_norm),
        grid_spec=grid_spec,
        out_shape=jax.ShapeDtypeStruct((T, D), F32),
        compiler_params=_params("arbitrary"),
        name="moe_combine",
    )(tables["src"], tables["dst"], tables["chunks"], lpos, ew, x, shared, mod3, final_g.reshape(1, D), ys)


def _swap_halves(w):
    half = w.shape[-1] // 2
    return jnp.concatenate([w[..., half:], w[..., :half]], axis=-1)


def _split_w_in(w_in):
    a_cols = 3 * A_WIDTH
    b_cols = B_Q_LORA + B_KV_LORA + B_ROPE
    c_cols = 2 * C_WIDTH
    wa = w_in[:, :a_cols].astype(BF16)
    wb_raw = w_in[:, a_cols:a_cols + b_cols]
    kpe = wb_raw[:, B_Q_LORA + B_KV_LORA:]
    zeros = jnp.zeros_like(kpe)
    wb = jnp.concatenate([wb_raw[:, :B_Q_LORA + B_KV_LORA], kpe, zeros, _swap_halves(kpe), zeros],
                         axis=1).astype(BF16)
    wc = w_in[:, a_cols + b_cols:a_cols + b_cols + c_cols].astype(BF16)
    wg = w_in[:, a_cols + b_cols + c_cols:].astype(BF16)
    return wa, wb, wc, wg


def _mla_weights(w_uq, w_ukv):
    r = w_uq.shape[0]
    wq = w_uq.reshape(r, B_HEADS, B_NOPE + B_ROPE)
    nope, pe = wq[..., :B_NOPE], wq[..., B_NOPE:]
    zeros = jnp.zeros_like(pe)
    main = jnp.concatenate([nope, pe, zeros], axis=-1).reshape(r, B_HEADS * B_QK_PAD)
    swapped = jnp.concatenate([_swap_halves(pe), zeros], axis=-1).reshape(r, B_HEADS * LANE)
    wq_ext = jnp.concatenate([main, swapped], axis=1).astype(BF16)
    rk = w_ukv.shape[0]
    wkv = w_ukv.reshape(rk, B_HEADS, B_NOPE + B_VDIM)
    wk = wkv[..., :B_NOPE].reshape(rk, -1).astype(BF16)
    wvt = wkv[..., B_NOPE:].reshape(rk, -1).T.astype(BF16)
    return wq_ext, wk, wvt


def kernel(x, c, positions, rel_bias, norm_mix, norm_ffn, w_mod, b_mod, w_in, mla_q_norm, mla_w_uq, mla_kv_norm, mla_w_ukv, gmlp_ln_g, gmlp_ln_b, gmlp_w_s, gmlp_b_s, w_branch, w_out, w_router, router_bias, we_gate, we_up, we_down, ws_gate, ws_up, ws_down, final_norm):
    B, S, D = x.shape
    T = B * S
    L = w_mod.shape[0]
    xt = x.reshape(T, D)

    mod = _modulation(c, w_mod, b_mod)
    mod3 = mod.reshape(L * B * 6, 1, D)
    cos, sin = _rope_tables(positions)
    a_bias = _mixer_a_bias(rel_bias)
    n_runs = (T // min(TOKEN_TILE, S)) * N_EXPERTS
    n_tiles = pl.cdiv(T * TOP_K + n_runs * (ROW_ALIGN - 1), EXPERT_TILE) + N_EXPERTS

    for l in range(L):
        row = lambda k: (lambda b, l=l, k=k: (l * B + b) * 6 + k)
        wa, wb, wc, wg = _split_w_in(w_in[l])
        wq_ext, wk_nope, wv_t = _mla_weights(mla_w_uq[l], mla_w_ukv[l])

        h = _norm_mod(xt, norm_mix[l], mod3, row(1), row(0), S)
        qkv = _matmul(h, wa, BF16, name="proj_a")
        bproj = _matmul(h, wb, F32, name="proj_b")
        z = _matmul(h, wc, BF16, act="gelu", name="proj_c")
        gates = _matmul(h, wg, BF16, act="sigmoid", name="proj_gates")

        y_a = _mixer_a(qkv, a_bias, B, S)
        qp, kp, vt = _mla_prep(bproj, cos, sin, mla_q_norm[l], mla_kv_norm[l], wq_ext, wk_nope, wv_t)
        y_b = _mla_attention(qp, kp, vt, B, S)
        y_c = _mixer_c(z, gmlp_ln_g[l], gmlp_ln_b[l], gmlp_w_s[l], gmlp_b_s[l])

        merged = _branch_merge(y_a, y_b, y_c, w_branch[l].astype(BF16), gates)
        xt = _matmul_residual(merged, w_out[l].astype(BF16), xt, mod3, row(2), S)

        hb, ew, lpos, cnt = _route(xt, norm_ffn[l], mod3, row(4), row(3), w_router[l], router_bias[l], S)
        tables = _moe_tables(cnt, n_tiles)
        xs = _dispatch(hb, lpos, tables, n_tiles * EXPERT_TILE, S)
        ys = _experts(xs, tables, we_gate, we_up, we_down, l)
        wsgu = jnp.concatenate([ws_gate[l], ws_up[l]], axis=1).astype(BF16)
        shared = _shared_expert(hb, wsgu, ws_down[l].astype(BF16))
        xt = _combine(xt, shared, lpos, ew, mod3, row(5), tables, ys, final_norm, l == L - 1, S)

    return xt.reshape(B, S, D)
```

```python
import functools

import jax
import jax.numpy as jnp
from jax import lax
from jax.experimental import pallas as pl
from jax.experimental.pallas import tpu as pltpu

F32 = jnp.float32
BF16 = jnp.bfloat16
U32 = jnp.uint32
I32 = jnp.int32

EPS = 1e-6
NEG_INF = -1e30

D_MODEL = 2048
HALF_D = D_MODEL // 2
CHUNK = 64

A_HEADS = 16
A_HEAD_DIM = 64
A_WIDTH = A_HEADS * A_HEAD_DIM
A_LEFT_CHUNKS = 8
MAX_REL = 128
A_QBLOCK = 256
A_KBLOCKS = 3

B_HEADS = 8
B_NOPE = 128
B_ROPE = 64
B_VDIM = 128
B_Q_LORA = 512
B_KV_LORA = 256
B_QK_PAD = 256
B_VT_ROWS = B_VDIM + 16
LOG2_E = 1.4426950408889634
ROPE_THETA = 10000.0
B_TQ = 1024
B_TK = 1024

C_BLOCK = 128
C_GROUPS = 8
C_WIDTH = 1024

N_EXPERTS = 64
EXPERT_GROUP = 8
TOP_K = 8
TOPK_GROUPS = 4
D_EXPERT = 512
ROUTED_SCALE = 2.5
EXPERT_TILE = 512
TOKEN_TILE = 256
ROW_ALIGN = 8
PERM_BLOCK = 256
COMBINE_BLOCK = 512

LANE = 128
V7X_VMEM_LIMIT = 56 * 1024 * 1024


def _params(*sem):
    return pltpu.CompilerParams(dimension_semantics=sem, vmem_limit_bytes=V7X_VMEM_LIMIT)


def _pack_halves(v):
    w = v.shape[1] // 2
    bits = lax.bitcast_convert_type(v.astype(BF16).astype(F32), U32)
    return (bits[:, :w] >> 16) | bits[:, w:]


def _unpack_halves(p):
    lo = lax.bitcast_convert_type(p << 16, F32)
    hi = lax.bitcast_convert_type(p & jnp.uint32(0xFFFF0000), F32)
    return lo, hi


def _mod_kernel(c_ref, w_ref, b_ref, o_ref):
    c = c_ref[...]
    ca = (c * jax.nn.sigmoid(c)).astype(BF16)
    o_ref[...] = jnp.dot(ca, w_ref[...].astype(BF16), preferred_element_type=F32) + b_ref[...]


def _modulation(c, w_mod, b_mod):
    L, D, N = w_mod.shape
    B = c.shape[0]
    tn = 512
    return pl.pallas_call(
        _mod_kernel,
        grid=(L, N // tn),
        in_specs=[
            pl.BlockSpec((B, D), lambda l, j: (0, 0)),
            pl.BlockSpec((None, D, tn), lambda l, j: (l, 0, j)),
            pl.BlockSpec((None, 1, tn), lambda l, j: (l, 0, j)),
        ],
        out_specs=pl.BlockSpec((None, B, tn), lambda l, j: (l, 0, j)),
        out_shape=jax.ShapeDtypeStruct((L, B, N), F32),
        compiler_params=_params("parallel", "arbitrary"),
        name="modulation",
    )(c, w_mod, b_mod.reshape(L, 1, N))


def _modulated_norm(x, g, sc, sh):
    r = lax.rsqrt(jnp.mean(x * x, axis=-1, keepdims=True) + EPS)
    return (x * r * g) * (1.0 + sc) + sh


def _norm_mod_kernel(x_ref, g_ref, sc_ref, sh_ref, o_ref):
    o_ref[...] = _modulated_norm(x_ref[...], g_ref[...], sc_ref[...], sh_ref[...]).astype(o_ref.dtype)


def _norm_mod(x, g, mod3, row_sc, row_sh, S, tm=512):
    T, D = x.shape
    per = S // tm
    return pl.pallas_call(
        _norm_mod_kernel,
        grid=(T // tm,),
        in_specs=[
            pl.BlockSpec((tm, D), lambda i: (i, 0)),
            pl.BlockSpec((1, D), lambda i: (0, 0)),
            pl.BlockSpec((None, 1, D), lambda i: (row_sc(i // per), 0, 0)),
            pl.BlockSpec((None, 1, D), lambda i: (row_sh(i // per), 0, 0)),
        ],
        out_specs=pl.BlockSpec((tm, D), lambda i: (i, 0)),
        out_shape=jax.ShapeDtypeStruct((T, D), BF16),
        compiler_params=_params("parallel"),
        name="norm_mod",
    )(x, g.reshape(1, D), mod3, mod3)


def _mm_kernel(a_ref, w_ref, o_ref, *, act):
    acc = jnp.dot(a_ref[...], w_ref[...], preferred_element_type=F32)
    if act == "gelu":
        acc = jax.nn.gelu(acc)
    elif act == "sigmoid":
        acc = jax.nn.sigmoid(acc)
    o_ref[...] = acc.astype(o_ref.dtype)


def _matmul(a, w, out_dtype, act=None, tm=1024, tn=1024, name="matmul"):
    M, K = a.shape
    N = w.shape[1]
    tm = min(tm, M)
    return pl.pallas_call(
        functools.partial(_mm_kernel, act=act),
        grid=(M // tm, N // tn),
        in_specs=[
            pl.BlockSpec((tm, K), lambda i, j: (i, 0)),
            pl.BlockSpec((K, tn), lambda i, j: (0, j)),
        ],
        out_specs=pl.BlockSpec((tm, tn), lambda i, j: (i, j)),
        out_shape=jax.ShapeDtypeStruct((M, N), out_dtype),
        compiler_params=_params("parallel", "arbitrary"),
        name=name,
    )(a, w)


def _mm_residual_kernel(a_ref, w_ref, x_ref, g_ref, o_ref):
    acc = jnp.dot(a_ref[...], w_ref[...], preferred_element_type=F32)
    o_ref[...] = x_ref[...] + g_ref[...] * acc


def _matmul_residual(a, w, x, mod3, row_g, S, tm=1024, tn=1024):
    M, K = a.shape
    N = w.shape[1]
    tm = min(tm, S)
    per = S // tm
    return pl.pallas_call(
        _mm_residual_kernel,
        grid=(M // tm, N // tn),
        in_specs=[
            pl.BlockSpec((tm, K), lambda i, j: (i, 0)),
            pl.BlockSpec((K, tn), lambda i, j: (0, j)),
            pl.BlockSpec((tm, tn), lambda i, j: (i, j)),
            pl.BlockSpec((None, 1, tn), lambda i, j: (row_g(i // per), 0, j)),
        ],
        out_specs=pl.BlockSpec((tm, tn), lambda i, j: (i, j)),
        out_shape=jax.ShapeDtypeStruct((M, N), F32),
        compiler_params=_params("parallel", "arbitrary"),
        name="out_proj_residual",
    )(a, w, x, mod3)


def _mixer_a_kernel(q_ref, k0_ref, k1_ref, k2_ref, v0_ref, v1_ref, v2_ref, bias_ref, o_ref, *, scale):
    i = pl.program_id(2)
    qb = A_QBLOCK
    kw = A_KBLOCKS * qb
    q = q_ref[...]
    k = jnp.concatenate([k0_ref[...], k1_ref[...], k2_ref[...]], axis=0)
    v = jnp.concatenate([v0_ref[...], v1_ref[...], v2_ref[...]], axis=0)
    lane = lax.broadcasted_iota(I32, (qb, LANE), 1)
    kpos = (i - (A_KBLOCKS - 1)) * qb + lax.broadcasted_iota(I32, (1, kw), 1)
    in_seq = kpos >= 0
    outs = []
    for hh in range(2):
        head_lanes = (lane < A_HEAD_DIM) if hh == 0 else (lane >= A_HEAD_DIM)
        qm = jnp.where(head_lanes, q, jnp.zeros_like(q)) * jnp.asarray(scale, q.dtype)
        s = lax.dot_general(qm, k, (((1,), (1,)), ((), ())), preferred_element_type=F32)
        s = s + bias_ref[hh]
        s = jnp.where(in_seq, s, NEG_INF)
        m = jnp.max(s, axis=-1, keepdims=True)
        p = jnp.exp(s - m)
        l = jnp.sum(p, axis=-1, keepdims=True)
        o = jnp.dot(p.astype(BF16), v, preferred_element_type=F32)
        outs.append(o / l)
    o_ref[...] = jnp.where(lane < A_HEAD_DIM, outs[0], outs[1]).astype(o_ref.dtype)


def _mixer_a_bias(rel_bias):
    qb = A_QBLOCK
    kw = A_KBLOCKS * qb
    qi = jnp.arange(qb)[:, None]
    kj = jnp.arange(kw)[None, :]
    qc = qi // CHUNK
    kc = kj // CHUNK
    first = (A_KBLOCKS - 1) * qb // CHUNK - A_LEFT_CHUNKS
    in_band = (kc >= qc + first) & (kc <= qc + first + A_LEFT_CHUNKS)
    period = kw + qb
    m = jnp.arange(period)
    k_minus_q = jnp.where(m < kw, m, m - period)
    d = (A_KBLOCKS - 1) * qb - k_minus_q
    line = rel_bias[:, jnp.clip(d, -MAX_REL, MAX_REL) + MAX_REL].astype(F32)
    heads = rel_bias.shape[0]
    flat = jnp.broadcast_to(line[:, None, :], (heads, qb, period)).reshape(heads, qb * period)
    bias = flat[:, :qb * (period - 1)].reshape(heads, qb, period - 1)[:, :, :kw]
    return jnp.where(in_band[None], bias, NEG_INF)


def _mixer_a(qkv, bias, B, S):
    T = qkv.shape[0]
    qb = A_QBLOCK
    nq = S // qb
    pairs = A_HEADS // 2

    def kv_spec(j, col0):
        return pl.BlockSpec(
            (qb, LANE), lambda p, b, i: (b * nq + jnp.maximum(i - (A_KBLOCKS - 1) + j, 0), col0 + p))

    return pl.pallas_call(
        functools.partial(_mixer_a_kernel, scale=A_HEAD_DIM ** -0.5),
        grid=(pairs, B, nq),
        in_specs=[
            pl.BlockSpec((qb, LANE), lambda p, b, i: (b * nq + i, p)),
            kv_spec(0, pairs), kv_spec(1, pairs), kv_spec(2, pairs),
            kv_spec(0, 2 * pairs), kv_spec(1, 2 * pairs), kv_spec(2, 2 * pairs),
            pl.BlockSpec((2, qb, A_KBLOCKS * qb), lambda p, b, i: (p, 0, 0)),
        ],
        out_specs=pl.BlockSpec((qb, LANE), lambda p, b, i: (b * nq + i, p)),
        out_shape=jax.ShapeDtypeStruct((T, A_WIDTH), BF16),
        compiler_params=_params("parallel", "parallel", "arbitrary"),
        name="mixer_a",
    )(qkv, qkv, qkv, qkv, qkv, qkv, qkv, bias)


def _rope_table_kernel(pos_ref, inv_ref, c_ref, s_ref):
    ang = pos_ref[...].astype(F32) * inv_ref[...]
    lane = lax.broadcasted_iota(I32, ang.shape, 1)
    first_half = (lane % B_ROPE) < (B_ROPE // 2)
    c_ref[...] = jnp.cos(ang)
    sn = jnp.sin(ang)
    s_ref[...] = jnp.where(first_half, -sn, sn)


def _rope_tables(positions, tm=512):
    T = positions.size
    half = B_ROPE // 2
    inv = ROPE_THETA ** (-jnp.arange(half, dtype=F32) / half)
    inv = jnp.tile(inv, LANE // half).reshape(1, LANE)
    return pl.pallas_call(
        _rope_table_kernel,
        grid=(T // tm,),
        in_specs=[pl.BlockSpec((tm, 1), lambda i: (i, 0)), pl.BlockSpec((1, LANE), lambda i: (0, 0))],
        out_specs=[pl.BlockSpec((tm, LANE), lambda i: (i, 0))] * 2,
        out_shape=[jax.ShapeDtypeStruct((T, LANE), F32)] * 2,
        compiler_params=_params("parallel"),
        name="rope_tables",
    )(positions.reshape(T, 1), inv)


def _mla_prep_kernel(bp_ref, c_ref, s_ref, qn_ref, kvn_ref, wq_ref, wk_ref, wvt_ref, q_out, k_out, vt_out, *, scale):
    bp = bp_ref[...]
    cq = bp[:, :B_Q_LORA]
    ckv = bp[:, B_Q_LORA:B_Q_LORA + B_KV_LORA]
    kpe = bp[:, B_Q_LORA + B_KV_LORA:B_Q_LORA + B_KV_LORA + LANE]
    kpe_sw = bp[:, B_Q_LORA + B_KV_LORA + LANE:]
    cqn = cq * lax.rsqrt(jnp.mean(cq * cq, axis=-1, keepdims=True) + EPS) * qn_ref[...]
    ckvn = ckv * lax.rsqrt(jnp.mean(ckv * ckv, axis=-1, keepdims=True) + EPS) * kvn_ref[...]
    qa = jnp.dot(cqn.astype(BF16), wq_ref[...], preferred_element_type=F32)
    ckvb = ckvn.astype(BF16)
    kn = jnp.dot(ckvb, wk_ref[...], preferred_element_type=F32)
    vt = lax.dot_general(wvt_ref[...], ckvb, (((1,), (1,)), ((), ())), preferred_element_type=F32).astype(BF16)
    ones_rows = jnp.ones((B_VT_ROWS - B_VDIM, vt.shape[1]), BF16)
    for h in range(B_HEADS):
        vt_out[h * B_VT_ROWS:h * B_VT_ROWS + B_VDIM, :] = vt[h * B_VDIM:(h + 1) * B_VDIM, :]
        vt_out[h * B_VT_ROWS + B_VDIM:(h + 1) * B_VT_ROWS, :] = ones_rows
    cos = c_ref[...]
    sin = s_ref[...]
    k_rot = (kpe * cos + kpe_sw * sin).astype(BF16)
    sw0 = B_HEADS * B_QK_PAD
    for h in range(B_HEADS):
        c0 = h * B_QK_PAD
        q_out[:, c0:c0 + B_NOPE] = (qa[:, c0:c0 + B_NOPE] * scale).astype(BF16)
        pe = qa[:, c0 + B_NOPE:c0 + B_QK_PAD]
        pe_sw = qa[:, sw0 + h * LANE:sw0 + (h + 1) * LANE]
        q_out[:, c0 + B_NOPE:c0 + B_QK_PAD] = ((pe * cos + pe_sw * sin) * scale).astype(BF16)
        k_out[:, c0:c0 + B_NOPE] = kn[:, h * B_NOPE:(h + 1) * B_NOPE].astype(BF16)
        k_out[:, c0 + B_NOPE:c0 + B_QK_PAD] = k_rot


def _mla_prep(bproj, cos, sin, q_norm, kv_norm, wq, wk, wvt, tm=256):
    T = bproj.shape[0]
    scale = (B_NOPE + B_ROPE) ** -0.5 * LOG2_E
    qk_w = B_HEADS * B_QK_PAD
    v_w = B_HEADS * B_VT_ROWS
    row = lambda i: (i, 0)
    fixed = lambda i: (0, 0)
    return pl.pallas_call(
        functools.partial(_mla_prep_kernel, scale=scale),
        grid=(T // tm,),
        in_specs=[
            pl.BlockSpec((tm, bproj.shape[1]), row),
            pl.BlockSpec((tm, LANE), row),
            pl.BlockSpec((tm, LANE), row),
            pl.BlockSpec((1, B_Q_LORA), fixed),
            pl.BlockSpec((1, B_KV_LORA), fixed),
            pl.BlockSpec(wq.shape, fixed),
            pl.BlockSpec(wk.shape, fixed),
            pl.BlockSpec(wvt.shape, fixed),
        ],
        out_specs=[pl.BlockSpec((tm, qk_w), row), pl.BlockSpec((tm, qk_w), row),
                   pl.BlockSpec((v_w, tm), lambda i: (0, i))],
        out_shape=[jax.ShapeDtypeStruct((T, qk_w), BF16), jax.ShapeDtypeStruct((T, qk_w), BF16),
                   jax.ShapeDtypeStruct((v_w, T), BF16)],
        compiler_params=_params("parallel"),
        name="mla_prep",
    )(bproj, cos, sin, q_norm.reshape(1, -1), kv_norm.reshape(1, -1), wq, wk, wvt)


def _mla_attn_kernel(qt_ref, kt_ref, q_ref, k_ref, vt_ref, o_ref, m_sc, acc_sc, *, tq, tk):
    p = pl.program_id(2)
    qi = qt_ref[p]
    kj = kt_ref[p]

    @pl.when(kj == 0)
    def _():
        m_sc[...] = jnp.full_like(m_sc, NEG_INF)
        acc_sc[...] = jnp.zeros_like(acc_sc)

    def online_softmax_step(s):
        m_prev = m_sc[...]
        m_new = jnp.maximum(m_prev, jnp.max(s, axis=0, keepdims=True))
        alpha = jnp.exp2(m_prev - m_new)
        pr = jnp.exp2((s - m_new).astype(BF16))
        acc_sc[...] = alpha * acc_sc[...] + jnp.dot(vt_ref[...], pr, preferred_element_type=F32)
        m_sc[...] = m_new

    def scores():
        return lax.dot_general(k_ref[...], q_ref[...], (((1,), (1,)), ((), ())), preferred_element_type=F32)

    k_end = (kj + 1) * tk
    crosses_diagonal = k_end > qi * tq

    @pl.when(jnp.logical_not(crosses_diagonal))
    def _():
        online_softmax_step(scores())

    @pl.when(crosses_diagonal)
    def _():
        s = scores()
        kc = (kj * tk + lax.broadcasted_iota(I32, s.shape, 0)) // CHUNK
        qc = (qi * tq + lax.broadcasted_iota(I32, s.shape, 1)) // CHUNK
        online_softmax_step(jnp.where(kc <= qc, s, NEG_INF))

    @pl.when(k_end == (qi + 1) * tq)
    def _():
        row_sum = acc_sc[B_VDIM:B_VDIM + 1, :]
        o_ref[...] = (acc_sc[:B_VDIM, :] / row_sum).T.astype(o_ref.dtype)


def _mla_attention(qp, kp, vt, B, S):
    T = qp.shape[0]
    tq = min(B_TQ, S)
    tk = min(B_TK, S)
    nq = S // tq
    nk = S // tk
    pairs = [(qi, kj) for qi in range(nq) for kj in range((qi + 1) * tq // tk)]
    qt = jnp.asarray([p[0] for p in pairs], I32)
    kt = jnp.asarray([p[1] for p in pairs], I32)
    grid_spec = pltpu.PrefetchScalarGridSpec(
        num_scalar_prefetch=2,
        grid=(B, B_HEADS, len(pairs)),
        in_specs=[
            pl.BlockSpec((tq, B_QK_PAD), lambda b, h, p, qt, kt: (b * nq + qt[p], h)),
            pl.BlockSpec((tk, B_QK_PAD), lambda b, h, p, qt, kt: (b * nk + kt[p], h)),
            pl.BlockSpec((B_VT_ROWS, tk), lambda b, h, p, qt, kt: (h, b * nk + kt[p])),
        ],
        out_specs=pl.BlockSpec((tq, B_VDIM), lambda b, h, p, qt, kt: (b * nq + qt[p], h)),
        scratch_shapes=[pltpu.VMEM((1, tq), F32), pltpu.VMEM((B_VT_ROWS, tq), F32)],
    )
    return pl.pallas_call(
        functools.partial(_mla_attn_kernel, tq=tq, tk=tk),
        grid_spec=grid_spec,
        out_shape=jax.ShapeDtypeStruct((T, B_HEADS * B_VDIM), BF16),
        compiler_params=_params("parallel", "parallel", "arbitrary"),
        name="mla_attention",
    )(qt, kt, qp, kp, vt)


def _mixer_c_kernel(z_ref, lg_ref, lb_ref, ws_ref, bs_ref, o_ref):
    tm = z_ref.shape[0]
    u = z_ref[:, :C_WIDTH].astype(F32)
    v = z_ref[:, C_WIDTH:].astype(F32)
    mu = jnp.mean(v, axis=-1, keepdims=True)
    var = jnp.mean(jnp.square(v - mu), axis=-1, keepdims=True)
    vn = ((v - mu) * lax.rsqrt(var + EPS) * lg_ref[...] + lb_ref[...]).astype(BF16)
    row = lax.broadcasted_iota(I32, (C_BLOCK, C_BLOCK), 0)
    col = lax.broadcasted_iota(I32, (C_BLOCK, C_BLOCK), 1)
    causal = col <= row
    gd = C_WIDTH // C_GROUPS
    for g in range(C_GROUPS):
        w = jnp.where(causal, ws_ref[g], 0.0).astype(BF16)
        b = bs_ref[:, g:g + 1]
        for n in range(tm // C_BLOCK):
            rows = slice(n * C_BLOCK, (n + 1) * C_BLOCK)
            cols = slice(g * gd, (g + 1) * gd)
            mixed = jnp.dot(w, vn[rows, cols], preferred_element_type=F32) + b
            o_ref[rows, cols] = (u[rows, cols] * mixed).astype(o_ref.dtype)


def _mixer_c(z, ln_g, ln_b, w_s, b_s, tm=256):
    T = z.shape[0]
    return pl.pallas_call(
        _mixer_c_kernel,
        grid=(T // tm,),
        in_specs=[
            pl.BlockSpec((tm, 2 * C_WIDTH), lambda i: (i, 0)),
            pl.BlockSpec((1, C_WIDTH), lambda i: (0, 0)),
            pl.BlockSpec((1, C_WIDTH), lambda i: (0, 0)),
            pl.BlockSpec((C_GROUPS, C_BLOCK, C_BLOCK), lambda i: (0, 0, 0)),
            pl.BlockSpec((C_BLOCK, C_GROUPS), lambda i: (0, 0)),
        ],
        out_specs=pl.BlockSpec((tm, C_WIDTH), lambda i: (i, 0)),
        out_shape=jax.ShapeDtypeStruct((T, C_WIDTH), BF16),
        compiler_params=_params("parallel"),
        name="mixer_c",
    )(z, ln_g.reshape(1, -1), ln_b.reshape(1, -1), w_s, b_s.T)


def _branch_kernel(ya_ref, yb_ref, yc_ref, wb_ref, ga_ref, gb_ref, gc_ref, o_ref):
    acc = ga_ref[...].astype(F32) * jnp.dot(ya_ref[...], wb_ref[0], preferred_element_type=F32)
    acc += gb_ref[...].astype(F32) * jnp.dot(yb_ref[...], wb_ref[1], preferred_element_type=F32)
    acc += gc_ref[...].astype(F32) * jnp.dot(yc_ref[...], wb_ref[2], preferred_element_type=F32)
    o_ref[...] = acc.astype(o_ref.dtype)


def _branch_merge(ya, yb, yc, wb, gates, tm=1024, tn=512):
    T, W = ya.shape
    D = wb.shape[2]
    tm = min(tm, T)
    nj = D // tn
    y_spec = pl.BlockSpec((tm, W), lambda i, j: (i, 0))
    return pl.pallas_call(
        _branch_kernel,
        grid=(T // tm, nj),
        in_specs=[
            y_spec, y_spec, y_spec,
            pl.BlockSpec((3, W, tn), lambda i, j: (0, 0, j)),
            pl.BlockSpec((tm, tn), lambda i, j: (i, j)),
            pl.BlockSpec((tm, tn), lambda i, j: (i, nj + j)),
            pl.BlockSpec((tm, tn), lambda i, j: (i, 2 * nj + j)),
        ],
        out_specs=pl.BlockSpec((tm, tn), lambda i, j: (i, j)),
        out_shape=jax.ShapeDtypeStruct((T, D), BF16),
        compiler_params=_params("parallel", "arbitrary"),
        name="branch_merge",
    )(ya, yb, yc, wb, gates, gates, gates)


def _route_kernel(x_ref, g_ref, sc_ref, sh_ref, wrh_ref, wrl_ref, rb_ref,
                  hb_ref, ew_ref, lpos_ref, cnt_ref):
    tm = x_ref.shape[0]
    h = _modulated_norm(x_ref[...], g_ref[...], sc_ref[...], sh_ref[...])
    hb = h.astype(BF16)
    hb_ref[...] = hb
    h_lo = (h - hb.astype(F32)).astype(BF16)
    dn = (((1,), (1,)), ((), ()))
    logits = (lax.dot_general(wrh_ref[...], hb, dn, preferred_element_type=F32)
              + lax.dot_general(wrh_ref[...], h_lo, dn, preferred_element_type=F32)
              + lax.dot_general(wrl_ref[...], hb, dn, preferred_element_type=F32))
    scores = jax.nn.sigmoid(logits)
    sel = scores + rb_ref[...]
    neg = -jnp.inf
    n_groups = N_EXPERTS // EXPERT_GROUP

    i8 = lax.broadcasted_iota(I32, (EXPERT_GROUP, tm), 0)
    group_scores = []
    for g in range(n_groups):
        blk = sel[g * EXPERT_GROUP:(g + 1) * EXPERT_GROUP, :]
        m1 = jnp.max(blk, axis=0, keepdims=True)
        first = jnp.min(jnp.where(blk == m1, i8, EXPERT_GROUP), axis=0, keepdims=True)
        m2 = jnp.max(jnp.where(i8 == first, neg, blk), axis=0, keepdims=True)
        group_scores.append(m1 + m2)
    gs = jnp.concatenate(group_scores, axis=0)

    ig = lax.broadcasted_iota(I32, (n_groups, tm), 0)
    gsel = jnp.zeros((n_groups, tm), F32)
    for _ in range(TOPK_GROUPS):
        m = jnp.max(gs, axis=0, keepdims=True)
        first = jnp.min(jnp.where(gs == m, ig, n_groups), axis=0, keepdims=True)
        hit = ig == first
        gsel = jnp.where(hit, 1.0, gsel)
        gs = jnp.where(hit, neg, gs)

    masked = jnp.concatenate(
        [jnp.where(gsel[g:g + 1, :] > 0.0, sel[g * EXPERT_GROUP:(g + 1) * EXPERT_GROUP, :], neg)
         for g in range(n_groups)], axis=0)

    ie = lax.broadcasted_iota(I32, (N_EXPERTS, tm), 0)
    chosen = jnp.zeros((N_EXPERTS, tm), F32)
    hits, ws = [], []
    for _ in range(TOP_K):
        m = jnp.max(masked, axis=0, keepdims=True)
        first = jnp.min(jnp.where(masked == m, ie, N_EXPERTS), axis=0, keepdims=True)
        hit = ie == first
        ws.append(jnp.sum(jnp.where(hit, scores, 0.0), axis=0, keepdims=True))
        hits.append(hit)
        chosen = jnp.where(hit, 1.0, chosen)
        masked = jnp.where(hit, neg, masked)
    w = jnp.concatenate(ws, axis=0)
    denom = jnp.sum(w, axis=0, keepdims=True) + 1e-20
    ew_ref[...] = w / denom * ROUTED_SCALE

    s_idx = lax.broadcasted_iota(I32, (tm, tm), 0)
    t_idx = lax.broadcasted_iota(I32, (tm, tm), 1)
    earlier = jnp.where(s_idx < t_idx, 1.0, 0.0).astype(BF16)
    local_rank = jnp.dot(chosen.astype(BF16), earlier, preferred_element_type=F32)
    n_e = jnp.sum(chosen, axis=1, keepdims=True)
    run = jnp.floor((n_e + (ROW_ALIGN - 1.0)) * (1.0 / ROW_ALIGN)) * ROW_ALIGN
    run_b = jnp.broadcast_to(run, (N_EXPERTS, LANE))
    e_row = lax.broadcasted_iota(I32, (N_EXPERTS, N_EXPERTS), 0)
    e_col = lax.broadcasted_iota(I32, (N_EXPERTS, N_EXPERTS), 1)
    before = jnp.where(e_col < e_row, 1.0, 0.0).astype(BF16)
    run_start = jnp.dot(before, run_b.astype(BF16), preferred_element_type=F32)
    base = run_start[:, :1] + local_rank
    lpos = [jnp.sum(jnp.where(hit, base, 0.0), axis=0, keepdims=True) for hit in hits]
    lpos_ref[...] = jnp.concatenate(lpos, axis=0).astype(I32)
    cnt_ref[...] = run_b.astype(I32)


def _route(x, g, mod3, row_sc, row_sh, w_router, router_bias, S):
    T, D = x.shape
    tm = min(TOKEN_TILE, S)
    per = S // tm
    wr_t = w_router.T
    wr_hi = wr_t.astype(BF16)
    wr_lo = (wr_t - wr_hi.astype(F32)).astype(BF16)
    row = lambda i: (i, 0)
    col = lambda i: (0, i)
    fixed = lambda i: (0, 0)
    return pl.pallas_call(
        _route_kernel,
        grid=(T // tm,),
        in_specs=[
            pl.BlockSpec((tm, D), row),
            pl.BlockSpec((1, D), fixed),
            pl.BlockSpec((None, 1, D), lambda i: (row_sc(i // per), 0, 0)),
            pl.BlockSpec((None, 1, D), lambda i: (row_sh(i // per), 0, 0)),
            pl.BlockSpec((N_EXPERTS, D), fixed),
            pl.BlockSpec((N_EXPERTS, D), fixed),
            pl.BlockSpec((N_EXPERTS, 1), fixed),
        ],
        out_specs=[
            pl.BlockSpec((tm, D), row),
            pl.BlockSpec((TOP_K, tm), col),
            pl.BlockSpec((TOP_K, tm), col),
            pl.BlockSpec((N_EXPERTS, LANE), col),
        ],
        out_shape=[
            jax.ShapeDtypeStruct((T, D), BF16),
            jax.ShapeDtypeStruct((TOP_K, T), F32),
            jax.ShapeDtypeStruct((TOP_K, T), I32),
            jax.ShapeDtypeStruct((N_EXPERTS, (T // tm) * LANE), I32),
        ],
        compiler_params=_params("parallel"),
        name="norm_route",
    )(x, g.reshape(1, D), mod3, mod3, wr_hi, wr_lo, router_bias.reshape(N_EXPERTS, 1))


def _moe_tables(cnt, n_tiles):
    runs = cnt[:, ::LANE].T
    total = jnp.sum(runs, axis=0)
    region = (total + EXPERT_TILE - 1) // EXPERT_TILE * EXPERT_TILE
    region_end = jnp.cumsum(region)
    region_start = region_end - region
    dst = region_start[None, :] + jnp.cumsum(runs, axis=0) - runs
    src = jnp.cumsum(runs, axis=1) - runs
    tile_end = region_end // EXPERT_TILE
    tile_expert = jnp.sum(tile_end[None, :] <= jnp.arange(n_tiles, dtype=I32)[:, None], axis=1)
    tile_expert = jnp.minimum(tile_expert, N_EXPERTS - 1).astype(I32)
    as_i32 = lambda a: a.reshape(-1).astype(I32)
    gap_start = jnp.concatenate([region_start + total, region_end[-1:]])
    gap_rows = jnp.concatenate([region - total, n_tiles * EXPERT_TILE - region_end[-1:]])
    return dict(dst=as_i32(dst), src=as_i32(src), chunks=as_i32(runs // ROW_ALIGN),
                gap_start=as_i32(gap_start), gap_chunks=as_i32(gap_rows // ROW_ALIGN),
                tile_expert=tile_expert, n_used=as_i32(tile_end[-1:]))


def _run_copies(i, src_ref, dst_ref, chunks_ref, make_copy, wait):
    for e in range(N_EXPERTS):
        s0 = src_ref[i * N_EXPERTS + e]
        d0 = dst_ref[i * N_EXPERTS + e]

        def body(c, carry, s0=s0, d0=d0):
            cp = make_copy(pl.multiple_of(s0 + c * ROW_ALIGN, ROW_ALIGN), pl.multiple_of(d0 + c * ROW_ALIGN, ROW_ALIGN))
            cp.wait() if wait else cp.start()
            return carry
        lax.fori_loop(0, chunks_ref[i * N_EXPERTS + e], body, 0)


def _dispatch_kernel(src_ref, dst_ref, chunks_ref, ts_ref, tc_ref, hb_ref, lp_ref, xs_hbm, xloc, zrows, sem):
    i = pl.program_id(0)
    n = pl.num_programs(0)
    tt = hb_ref.shape[0]
    slot = i % 2

    @pl.when(i == 0)
    def _():
        zrows[...] = jnp.zeros_like(zrows)
        for wait in (False, True):
            for e in range(N_EXPERTS + 1):
                def body(c, carry, e=e, wait=wait):
                    cp = pltpu.make_async_copy(
                        zrows, xs_hbm.at[pl.ds(pl.multiple_of(ts_ref[e] + c * ROW_ALIGN, ROW_ALIGN), ROW_ALIGN)],
                        sem.at[0])
                    cp.wait() if wait else cp.start()
                    return carry
                lax.fori_loop(0, tc_ref[e], body, 0)

    lp = lp_ref[...]
    hbv = hb_ref[...]
    last_run = i * N_EXPERTS + N_EXPERTS - 1
    tile_rows = src_ref[last_run] + chunks_ref[last_run] * ROW_ALIGN

    def order_block(rb, carry):
        r0 = pl.multiple_of(rb * PERM_BLOCK, PERM_BLOCK)
        rows = r0 + lax.broadcasted_iota(I32, (PERM_BLOCK, tt), 0)
        onehot = jnp.zeros((PERM_BLOCK, tt), F32)
        for k in range(TOP_K):
            onehot = jnp.where(rows == lp[k:k + 1, :], 1.0, onehot)
        xb = jnp.dot(onehot.astype(BF16), hbv, preferred_element_type=F32)
        xloc[slot, pl.ds(r0, PERM_BLOCK), :] = _pack_halves(xb)
        return carry
    lax.fori_loop(0, pl.cdiv(tile_rows, PERM_BLOCK), order_block, 0)

    def copies_from(buf_slot):
        def make_copy(s, d):
            return pltpu.make_async_copy(xloc.at[buf_slot, pl.ds(s, ROW_ALIGN)], xs_hbm.at[pl.ds(d, ROW_ALIGN)],
                                         sem.at[buf_slot])
        return make_copy
    _run_copies(i, src_ref, dst_ref, chunks_ref, copies_from(slot), wait=False)

    @pl.when(i > 0)
    def _():
        _run_copies(i - 1, src_ref, dst_ref, chunks_ref, copies_from(1 - slot), wait=True)

    @pl.when(i == n - 1)
    def _():
        _run_copies(i, src_ref, dst_ref, chunks_ref, copies_from(slot), wait=True)


def _local_rows(tt):
    rows = TOP_K * tt + N_EXPERTS * ROW_ALIGN
    return (rows + COMBINE_BLOCK - 1) // COMBINE_BLOCK * COMBINE_BLOCK


def _dispatch(hb, lpos, tables, n_rows, S):
    T, D = hb.shape
    tt = min(TOKEN_TILE, S)
    grid_spec = pltpu.PrefetchScalarGridSpec(
        num_scalar_prefetch=5,
        grid=(T // tt,),
        in_specs=[
            pl.BlockSpec((tt, D), lambda i, *_: (i, 0)),
            pl.BlockSpec((TOP_K, tt), lambda i, *_: (0, i)),
        ],
        out_specs=pl.BlockSpec(memory_space=pl.ANY),
        scratch_shapes=[pltpu.VMEM((2, _local_rows(tt), HALF_D), U32), pltpu.VMEM((ROW_ALIGN, HALF_D), U32),
                        pltpu.SemaphoreType.DMA((2,))],
    )
    return pl.pallas_call(
        _dispatch_kernel,
        grid_spec=grid_spec,
        out_shape=jax.ShapeDtypeStruct((n_rows, HALF_D), U32),
        compiler_params=_params("arbitrary"),
        name="moe_dispatch",
    )(tables["src"], tables["dst"], tables["chunks"], tables["gap_start"], tables["gap_chunks"], hb, lpos)


def _expert_kernel(te_ref, nu_ref, x_ref, wg_ref, wu_ref, wd_ref, y_ref, wgu_bf, wd_bf):
    i = pl.program_id(0)

    @pl.when(i < nu_ref[0])
    def _():
        @pl.when((i == 0) | (te_ref[i] != te_ref[jnp.maximum(i - 1, 0)]))
        def _():
            wgu_bf[:, :D_EXPERT] = wg_ref[...].astype(BF16)
            wgu_bf[:, D_EXPERT:] = wu_ref[...].astype(BF16)
            wd_bf[...] = wd_ref[...].astype(BF16)

        x_lo, x_hi = _unpack_halves(x_ref[...])
        gu = jnp.dot(x_lo.astype(BF16), wgu_bf[:HALF_D, :], preferred_element_type=F32)
        gu += jnp.dot(x_hi.astype(BF16), wgu_bf[HALF_D:, :], preferred_element_type=F32)
        gate = gu[:, :D_EXPERT]
        act = (gate * jax.nn.sigmoid(gate) * gu[:, D_EXPERT:]).astype(BF16)
        y = jnp.dot(act, wd_bf[...], preferred_element_type=F32)
        y_ref[...] = _pack_halves(y)

    @pl.when(i >= nu_ref[0])
    def _():
        y_ref[...] = jnp.zeros_like(y_ref)


def _experts(xs, tables, we_gate, we_up, we_down, layer):
    tm = EXPERT_TILE
    n_tiles = xs.shape[0] // tm
    used = lambda i, nu: jnp.minimum(i, nu[0] - 1)
    expert_block = lambda i, te, nu: (layer, te[used(i, nu)], 0, 0)
    row_block = lambda i, te, nu: (used(i, nu), 0)
    grid_spec = pltpu.PrefetchScalarGridSpec(
        num_scalar_prefetch=2,
        grid=(n_tiles,),
        in_specs=[
            pl.BlockSpec((tm, HALF_D), row_block),
            pl.BlockSpec((None, None, D_MODEL, D_EXPERT), expert_block),
            pl.BlockSpec((None, None, D_MODEL, D_EXPERT), expert_block),
            pl.BlockSpec((None, None, D_EXPERT, D_MODEL), expert_block),
        ],
        out_specs=pl.BlockSpec((tm, HALF_D), lambda i, te, nu: (i, 0)),
        scratch_shapes=[pltpu.VMEM((D_MODEL, 2 * D_EXPERT), BF16), pltpu.VMEM((D_EXPERT, D_MODEL), BF16)],
    )
    return pl.pallas_call(
        _expert_kernel,
        grid_spec=grid_spec,
        out_shape=jax.ShapeDtypeStruct(xs.shape, U32),
        compiler_params=_params("arbitrary"),
        name="experts",
    )(tables["tile_expert"], tables["n_used"], xs, we_gate, we_up, we_down)


def _shared_expert_kernel(h_ref, wgu_ref, wd_ref, o_ref):
    gu = jnp.dot(h_ref[...], wgu_ref[...], preferred_element_type=F32)
    gate = gu[:, :D_EXPERT]
    act = (gate * jax.nn.sigmoid(gate) * gu[:, D_EXPERT:]).astype(BF16)
    o_ref[...] = jnp.dot(act, wd_ref[...], preferred_element_type=F32).astype(o_ref.dtype)


def _shared_expert(hb, wsgu, wsd, tm=512):
    T, D = hb.shape
    return pl.pallas_call(
        _shared_expert_kernel,
        grid=(T // tm,),
        in_specs=[pl.BlockSpec((tm, D), lambda i: (i, 0)), pl.BlockSpec(wsgu.shape, lambda i: (0, 0)),
                  pl.BlockSpec(wsd.shape, lambda i: (0, 0))],
        out_specs=pl.BlockSpec((tm, D), lambda i: (i, 0)),
        out_shape=jax.ShapeDtypeStruct((T, D), BF16),
        compiler_params=_params("parallel"),
        name="shared_expert",
    )(hb, wsgu, wsd)


def _combine_kernel(src_ref, dst_ref, chunks_ref, lp_ref, ew_ref, x_ref, sh_ref, g2_ref, fg_ref, ys_hbm,
                    o_ref, yloc, acc, sem, *, final_norm):
    i = pl.program_id(0)
    n = pl.num_programs(0)
    tt = x_ref.shape[0]
    slot = i % 2

    def copies_into(buf_slot):
        def make_copy(s, d):
            return pltpu.make_async_copy(ys_hbm.at[pl.ds(d, ROW_ALIGN)], yloc.at[buf_slot, pl.ds(s, ROW_ALIGN)],
                                         sem.at[buf_slot])
        return make_copy

    @pl.when(i == 0)
    def _():
        yloc[...] = jnp.zeros_like(yloc)
        _run_copies(i, src_ref, dst_ref, chunks_ref, copies_into(0), wait=False)

    _run_copies(i, src_ref, dst_ref, chunks_ref, copies_into(slot), wait=True)

    @pl.when(i + 1 < n)
    def _():
        _run_copies(i + 1, src_ref, dst_ref, chunks_ref, copies_into(1 - slot), wait=False)

    lp = lp_ref[...]
    ew = ew_ref[...]
    acc[...] = jnp.zeros_like(acc)
    last_run = i * N_EXPERTS + N_EXPERTS - 1
    tile_rows = src_ref[last_run] + chunks_ref[last_run] * ROW_ALIGN

    def sum_block(rb, carry):
        r0 = pl.multiple_of(rb * COMBINE_BLOCK, COMBINE_BLOCK)
        rows = r0 + lax.broadcasted_iota(I32, (COMBINE_BLOCK, tt), 0)
        wmat = jnp.zeros((COMBINE_BLOCK, tt), F32)
        for k in range(TOP_K):
            wmat = jnp.where(rows == lp[k:k + 1, :], ew[k:k + 1, :], wmat)
        w_row = jnp.sum(wmat, axis=1, keepdims=True)
        onehot = jnp.where(wmat != 0.0, 1.0, 0.0).astype(BF16)
        y_lo, y_hi = _unpack_halves(yloc[slot, pl.ds(r0, COMBINE_BLOCK), :])
        dn = (((0,), (0,)), ((), ()))
        acc[:, :HALF_D] += lax.dot_general(onehot, (w_row * y_lo).astype(BF16), dn, preferred_element_type=F32)
        acc[:, HALF_D:] += lax.dot_general(onehot, (w_row * y_hi).astype(BF16), dn, preferred_element_type=F32)
        return carry
    lax.fori_loop(0, pl.cdiv(tile_rows, COMBINE_BLOCK), sum_block, 0)

    out = x_ref[...] + g2_ref[...] * (sh_ref[...].astype(F32) + acc[...])
    if final_norm:
        out = out * lax.rsqrt(jnp.mean(out * out, axis=-1, keepdims=True) + EPS) * fg_ref[...]
    o_ref[...] = out


def _combine(x, shared, lpos, ew, mod3, row_g, tables, ys, final_g, final_norm, S):
    T, D = x.shape
    tt = min(TOKEN_TILE, S)
    per = S // tt
    row = lambda i, *_: (i, 0)
    col = lambda i, *_: (0, i)
    grid_spec = pltpu.PrefetchScalarGridSpec(
        num_scalar_prefetch=3,
        grid=(T // tt,),
        in_specs=[
            pl.BlockSpec((TOP_K, tt), col),
            pl.BlockSpec((TOP_K, tt), col),
            pl.BlockSpec((tt, D), row),
            pl.BlockSpec((tt, D), row),
            pl.BlockSpec((None, 1, D), lambda i, *_: (row_g(i // per), 0, 0)),
            pl.BlockSpec((1, D), lambda i, *_: (0, 0)),
            pl.BlockSpec(memory_space=pl.ANY),
        ],
        out_specs=pl.BlockSpec((tt, D), row),
        scratch_shapes=[pltpu.VMEM((2, _local_rows(tt), HALF_D), U32), pltpu.VMEM((tt, D), F32),
                        pltpu.SemaphoreType.DMA((2,))],
    )
    return pl.pallas_call(
        functools.partial(_combine_kernel, final_norm=final_norm),
        grid_spec=grid_spec,
        out_shape=jax.ShapeDtypeStruct((T, D), F32),
        compiler_params=_params("arbitrary"),
        name="moe_combine",
    )(tables["src"], tables["dst"], tables["chunks"], lpos, ew, x, shared, mod3, final_g.reshape(1, D), ys)


def _swap_halves(w):
    half = w.shape[-1] // 2
    return jnp.concatenate([w[..., half:], w[..., :half]], axis=-1)


def _split_w_in(w_in):
    a_cols = 3 * A_WIDTH
    b_cols = B_Q_LORA + B_KV_LORA + B_ROPE
    c_cols = 2 * C_WIDTH
    wa = w_in[:, :a_cols].astype(BF16)
    wb_raw = w_in[:, a_cols:a_cols + b_cols]
    kpe = wb_raw[:, B_Q_LORA + B_KV_LORA:]
    zeros = jnp.zeros_like(kpe)
    wb = jnp.concatenate([wb_raw[:, :B_Q_LORA + B_KV_LORA], kpe, zeros, _swap_halves(kpe), zeros],
                         axis=1).astype(BF16)
    wc = w_in[:, a_cols + b_cols:a_cols + b_cols + c_cols].astype(BF16)
    wg = w_in[:, a_cols + b_cols + c_cols:].astype(BF16)
    return wa, wb, wc, wg


def _mla_weights(w_uq, w_ukv):
    r = w_uq.shape[0]
    wq = w_uq.reshape(r, B_HEADS, B_NOPE + B_ROPE)
    nope, pe = wq[..., :B_NOPE], wq[..., B_NOPE:]
    zeros = jnp.zeros_like(pe)
    main = jnp.concatenate([nope, pe, zeros], axis=-1).reshape(r, B_HEADS * B_QK_PAD)
    swapped = jnp.concatenate([_swap_halves(pe), zeros], axis=-1).reshape(r, B_HEADS * LANE)
    wq_ext = jnp.concatenate([main, swapped], axis=1).astype(BF16)
    rk = w_ukv.shape[0]
    wkv = w_ukv.reshape(rk, B_HEADS, B_NOPE + B_VDIM)
    wk = wkv[..., :B_NOPE].reshape(rk, -1).astype(BF16)
    wvt = wkv[..., B_NOPE:].reshape(rk, -1).T.astype(BF16)
    return wq_ext, wk, wvt


def kernel(x, c, positions, rel_bias, norm_mix, norm_ffn, w_mod, b_mod, w_in, mla_q_norm, mla_w_uq, mla_kv_norm, mla_w_ukv, gmlp_ln_g, gmlp_ln_b, gmlp_w_s, gmlp_b_s, w_branch, w_out, w_router, router_bias, we_gate, we_up, we_down, ws_gate, ws_up, ws_down, final_norm):
    B, S, D = x.shape
    T = B * S
    L = w_mod.shape[0]
    xt = x.reshape(T, D)

    mod = _modulation(c, w_mod, b_mod)
    mod3 = mod.reshape(L * B * 6, 1, D)
    cos, sin = _rope_tables(positions)
    a_bias = _mixer_a_bias(rel_bias)
    n_runs = (T // min(TOKEN_TILE, S)) * N_EXPERTS
    n_tiles = pl.cdiv(T * TOP_K + n_runs * (ROW_ALIGN - 1), EXPERT_TILE) + N_EXPERTS

    for l in range(L):
        row = lambda k: (lambda b, l=l, k=k: (l * B + b) * 6 + k)
        wa, wb, wc, wg = _split_w_in(w_in[l])
        wq_ext, wk_nope, wv_t = _mla_weights(mla_w_uq[l], mla_w_ukv[l])

        h = _norm_mod(xt, norm_mix[l], mod3, row(1), row(0), S)
        qkv = _matmul(h, wa, BF16, name="proj_a")
        bproj = _matmul(h, wb, F32, name="proj_b")
        z = _matmul(h, wc, BF16, act="gelu", name="proj_c")
        gates = _matmul(h, wg, BF16, act="sigmoid", name="proj_gates")

        y_a = _mixer_a(qkv, a_bias, B, S)
        qp, kp, vt = _mla_prep(bproj, cos, sin, mla_q_norm[l], mla_kv_norm[l], wq_ext, wk_nope, wv_t)
        y_b = _mla_attention(qp, kp, vt, B, S)
        y_c = _mixer_c(z, gmlp_ln_g[l], gmlp_ln_b[l], gmlp_w_s[l], gmlp_b_s[l])

        merged = _branch_merge(y_a, y_b, y_c, w_branch[l].astype(BF16), gates)
        xt = _matmul_residual(merged, w_out[l].astype(BF16), xt, mod3, row(2), S)

        hb, ew, lpos, cnt = _route(xt, norm_ffn[l], mod3, row(4), row(3), w_router[l], router_bias[l], S)
        tables = _moe_tables(cnt, n_tiles)
        xs = _dispatch(hb, lpos, tables, n_tiles * EXPERT_TILE, S)
        ys = _experts(xs, tables, we_gate, we_up, we_down, l)
        wsgu = jnp.concatenate([ws_gate[l], ws_up[l]], axis=1).astype(BF16)
        shared = _shared_expert(hb, wsgu, ws_down[l].astype(BF16))
        xt = _combine(xt, shared, lpos, ew, mod3, row(5), tables, ys, final_norm, l == L - 1, S)

    return xt.reshape(B, S, D)
```

```python
import functools

import jax
import jax.numpy as jnp
from jax import lax
from jax.experimental import pallas as pl
from jax.experimental.pallas import tpu as pltpu

F32 = jnp.float32
BF16 = jnp.bfloat16
U32 = jnp.uint32
I32 = jnp.int32

EPS = 1e-6
NEG_INF = -1e30

D_MODEL = 2048
HALF_D = D_MODEL // 2
CHUNK = 64

A_HEADS = 16
A_HEAD_DIM = 64
A_WIDTH = A_HEADS * A_HEAD_DIM
A_LEFT_CHUNKS = 8
MAX_REL = 128
A_QBLOCK = 256
A_KBLOCKS = 3

B_HEADS = 8
B_NOPE = 128
B_ROPE = 64
B_VDIM = 128
B_Q_LORA = 512
B_KV_LORA = 256
B_QK_PAD = 256
B_VT_ROWS = B_VDIM + 16
LOG2_E = 1.4426950408889634
ROPE_THETA = 10000.0
B_TQ = 1024
B_TK = 1024

C_BLOCK = 128
C_GROUPS = 8
C_WIDTH = 1024

N_EXPERTS = 64
EXPERT_GROUP = 8
TOP_K = 8
TOPK_GROUPS = 4
D_EXPERT = 512
ROUTED_SCALE = 2.5
EXPERT_TILE = 512
TOKEN_TILE = 256
ROW_ALIGN = 8
PERM_BLOCK = 256
PERM_COLS = 256
COMBINE_BLOCK = 512

LANE = 128
V7X_VMEM_LIMIT = 56 * 1024 * 1024


def _params(*sem):
    return pltpu.CompilerParams(dimension_semantics=sem, vmem_limit_bytes=V7X_VMEM_LIMIT)


def _pack_halves(v):
    w = v.shape[1] // 2
    r = v.astype(BF16).astype(F32)
    return _pack_bf16_pair(r[:, :w], r[:, w:])


def _pack_bf16_pair(lo, hi):
    return (lax.bitcast_convert_type(lo, U32) >> 16) | lax.bitcast_convert_type(hi, U32)


def _unpack_halves(p):
    lo = lax.bitcast_convert_type(p << 16, F32)
    hi = lax.bitcast_convert_type(p & jnp.uint32(0xFFFF0000), F32)
    return lo, hi


def _mod_kernel(c_ref, w_ref, b_ref, o_ref):
    c = c_ref[...]
    ca = (c * jax.nn.sigmoid(c)).astype(BF16)
    o_ref[...] = jnp.dot(ca, w_ref[...].astype(BF16), preferred_element_type=F32) + b_ref[...]


def _modulation(c, w_mod, b_mod):
    L, D, N = w_mod.shape
    B = c.shape[0]
    tn = 512
    return pl.pallas_call(
        _mod_kernel,
        grid=(L, N // tn),
        in_specs=[
            pl.BlockSpec((B, D), lambda l, j: (0, 0)),
            pl.BlockSpec((None, D, tn), lambda l, j: (l, 0, j)),
            pl.BlockSpec((None, 1, tn), lambda l, j: (l, 0, j)),
        ],
        out_specs=pl.BlockSpec((None, B, tn), lambda l, j: (l, 0, j)),
        out_shape=jax.ShapeDtypeStruct((L, B, N), F32),
        compiler_params=_params("parallel", "arbitrary"),
        name="modulation",
    )(c, w_mod, b_mod.reshape(L, 1, N))


def _modulated_norm(x, g, sc, sh):
    r = lax.rsqrt(jnp.mean(x * x, axis=-1, keepdims=True) + EPS)
    return (x * r * g) * (1.0 + sc) + sh


def _norm_mod_kernel(x_ref, g_ref, sc_ref, sh_ref, o_ref):
    o_ref[...] = _modulated_norm(x_ref[...], g_ref[...], sc_ref[...], sh_ref[...]).astype(o_ref.dtype)


def _norm_mod(x, g, mod3, row_sc, row_sh, S, tm=512):
    T, D = x.shape
    per = S // tm
    return pl.pallas_call(
        _norm_mod_kernel,
        grid=(T // tm,),
        in_specs=[
            pl.BlockSpec((tm, D), lambda i: (i, 0)),
            pl.BlockSpec((1, D), lambda i: (0, 0)),
            pl.BlockSpec((None, 1, D), lambda i: (row_sc(i // per), 0, 0)),
            pl.BlockSpec((None, 1, D), lambda i: (row_sh(i // per), 0, 0)),
        ],
        out_specs=pl.BlockSpec((tm, D), lambda i: (i, 0)),
        out_shape=jax.ShapeDtypeStruct((T, D), BF16),
        compiler_params=_params("parallel"),
        name="norm_mod",
    )(x, g.reshape(1, D), mod3, mod3)


def _mm_kernel(a_ref, w_ref, o_ref, *, act):
    acc = jnp.dot(a_ref[...], w_ref[...], preferred_element_type=F32)
    if act == "gelu":
        acc = jax.nn.gelu(acc)
    elif act == "sigmoid":
        acc = jax.nn.sigmoid(acc)
    o_ref[...] = acc.astype(o_ref.dtype)


def _matmul(a, w, out_dtype, act=None, tm=1024, tn=1024, name="matmul"):
    M, K = a.shape
    N = w.shape[1]
    tm = min(tm, M)
    return pl.pallas_call(
        functools.partial(_mm_kernel, act=act),
        grid=(M // tm, N // tn),
        in_specs=[
            pl.BlockSpec((tm, K), lambda i, j: (i, 0)),
            pl.BlockSpec((K, tn), lambda i, j: (0, j)),
        ],
        out_specs=pl.BlockSpec((tm, tn), lambda i, j: (i, j)),
        out_shape=jax.ShapeDtypeStruct((M, N), out_dtype),
        compiler_params=_params("parallel", "arbitrary"),
        name=name,
    )(a, w)


def _mm_residual_kernel(a_ref, w_ref, x_ref, g_ref, o_ref):
    acc = jnp.dot(a_ref[...], w_ref[...], preferred_element_type=F32)
    o_ref[...] = x_ref[...] + g_ref[...] * acc


def _matmul_residual(a, w, x, mod3, row_g, S, tm=1024, tn=1024):
    M, K = a.shape
    N = w.shape[1]
    tm = min(tm, S)
    per = S // tm
    return pl.pallas_call(
        _mm_residual_kernel,
        grid=(M // tm, N // tn),
        in_specs=[
            pl.BlockSpec((tm, K), lambda i, j: (i, 0)),
            pl.BlockSpec((K, tn), lambda i, j: (0, j)),
            pl.BlockSpec((tm, tn), lambda i, j: (i, j)),
            pl.BlockSpec((None, 1, tn), lambda i, j: (row_g(i // per), 0, j)),
        ],
        out_specs=pl.BlockSpec((tm, tn), lambda i, j: (i, j)),
        out_shape=jax.ShapeDtypeStruct((M, N), F32),
        compiler_params=_params("parallel", "arbitrary"),
        name="out_proj_residual",
    )(a, w, x, mod3)


def _mixer_a_kernel(q_ref, k0_ref, k1_ref, k2_ref, v0_ref, v1_ref, v2_ref, bias_ref, o_ref, *, scale):
    i = pl.program_id(2)
    qb = A_QBLOCK
    kw = A_KBLOCKS * qb
    lane = lax.broadcasted_iota(I32, (qb, LANE), 1)

    def attend(mask_before_sequence):
        q = q_ref[...]
        k = jnp.concatenate([k0_ref[...], k1_ref[...], k2_ref[...]], axis=0)
        v = jnp.concatenate([v0_ref[...], v1_ref[...], v2_ref[...]], axis=0)
        ones_col = jnp.where(lax.broadcasted_iota(I32, (kw, LANE), 1) == 0, 1.0, 0.0).astype(v.dtype)
        v_ext = jnp.concatenate([v, ones_col], axis=1)
        outs = []
        for hh in range(2):
            head_lanes = (lane < A_HEAD_DIM) if hh == 0 else (lane >= A_HEAD_DIM)
            qm = jnp.where(head_lanes, q, jnp.zeros_like(q)) * jnp.asarray(scale, q.dtype)
            s = lax.dot_general(qm, k, (((1,), (1,)), ((), ())), preferred_element_type=F32)
            s = s + bias_ref[hh]
            if mask_before_sequence:
                kpos = (i - (A_KBLOCKS - 1)) * qb + lax.broadcasted_iota(I32, (1, kw), 1)
                s = jnp.where(kpos >= 0, s, NEG_INF)
            m = jnp.max(s, axis=-1, keepdims=True)
            p = jnp.exp((s - m).astype(BF16))
            o = jnp.dot(p, v_ext, preferred_element_type=F32)
            outs.append(o[:, :LANE] / o[:, LANE:LANE + 1])
        o_ref[...] = jnp.where(lane < A_HEAD_DIM, outs[0], outs[1]).astype(o_ref.dtype)

    @pl.when(i < A_KBLOCKS - 1)
    def _():
        attend(True)

    @pl.when(i >= A_KBLOCKS - 1)
    def _():
        attend(False)


def _mixer_a_bias(rel_bias):
    qb = A_QBLOCK
    kw = A_KBLOCKS * qb
    qi = jnp.arange(qb)[:, None]
    kj = jnp.arange(kw)[None, :]
    qc = qi // CHUNK
    kc = kj // CHUNK
    first = (A_KBLOCKS - 1) * qb // CHUNK - A_LEFT_CHUNKS
    in_band = (kc >= qc + first) & (kc <= qc + first + A_LEFT_CHUNKS)
    period = kw + qb
    m = jnp.arange(period)
    k_minus_q = jnp.where(m < kw, m, m - period)
    d = (A_KBLOCKS - 1) * qb - k_minus_q
    line = rel_bias[:, jnp.clip(d, -MAX_REL, MAX_REL) + MAX_REL].astype(F32)
    heads = rel_bias.shape[0]
    flat = jnp.broadcast_to(line[:, None, :], (heads, qb, period)).reshape(heads, qb * period)
    bias = flat[:, :qb * (period - 1)].reshape(heads, qb, period - 1)[:, :, :kw]
    return jnp.where(in_band[None], bias, NEG_INF)


def _mixer_a(qkv, bias, B, S):
    T = qkv.shape[0]
    qb = A_QBLOCK
    nq = S // qb
    pairs = A_HEADS // 2

    def kv_spec(j, col0):
        return pl.BlockSpec(
            (qb, LANE), lambda p, b, i: (b * nq + jnp.maximum(i - (A_KBLOCKS - 1) + j, 0), col0 + p))

    return pl.pallas_call(
        functools.partial(_mixer_a_kernel, scale=A_HEAD_DIM ** -0.5),
        grid=(pairs, B, nq),
        in_specs=[
            pl.BlockSpec((qb, LANE), lambda p, b, i: (b * nq + i, p)),
            kv_spec(0, pairs), kv_spec(1, pairs), kv_spec(2, pairs),
            kv_spec(0, 2 * pairs), kv_spec(1, 2 * pairs), kv_spec(2, 2 * pairs),
            pl.BlockSpec((2, qb, A_KBLOCKS * qb), lambda p, b, i: (p, 0, 0)),
        ],
        out_specs=pl.BlockSpec((qb, LANE), lambda p, b, i: (b * nq + i, p)),
        out_shape=jax.ShapeDtypeStruct((T, A_WIDTH), BF16),
        compiler_params=_params("parallel", "parallel", "arbitrary"),
        name="mixer_a",
    )(qkv, qkv, qkv, qkv, qkv, qkv, qkv, bias)


def _rope_table_kernel(pos_ref, inv_ref, c_ref, s_ref):
    ang = pos_ref[...].astype(F32) * inv_ref[...]
    lane = lax.broadcasted_iota(I32, ang.shape, 1)
    first_half = (lane % B_ROPE) < (B_ROPE // 2)
    c_ref[...] = jnp.cos(ang)
    sn = jnp.sin(ang)
    s_ref[...] = jnp.where(first_half, -sn, sn)


def _rope_tables(positions, tm=512):
    T = positions.size
    half = B_ROPE // 2
    inv = ROPE_THETA ** (-jnp.arange(half, dtype=F32) / half)
    inv = jnp.tile(inv, LANE // half).reshape(1, LANE)
    return pl.pallas_call(
        _rope_table_kernel,
        grid=(T // tm,),
        in_specs=[pl.BlockSpec((tm, 1), lambda i: (i, 0)), pl.BlockSpec((1, LANE), lambda i: (0, 0))],
        out_specs=[pl.BlockSpec((tm, LANE), lambda i: (i, 0))] * 2,
        out_shape=[jax.ShapeDtypeStruct((T, LANE), F32)] * 2,
        compiler_params=_params("parallel"),
        name="rope_tables",
    )(positions.reshape(T, 1), inv)


def _mla_prep_kernel(bp_ref, c_ref, s_ref, qn_ref, kvn_ref, wq_ref, wk_ref, wvt_ref, q_out, k_out, vt_out, *, scale):
    bp = bp_ref[...]
    cq = bp[:, :B_Q_LORA]
    ckv = bp[:, B_Q_LORA:B_Q_LORA + B_KV_LORA]
    kpe = bp[:, B_Q_LORA + B_KV_LORA:B_Q_LORA + B_KV_LORA + LANE]
    kpe_sw = bp[:, B_Q_LORA + B_KV_LORA + LANE:]
    cqn = cq * lax.rsqrt(jnp.mean(cq * cq, axis=-1, keepdims=True) + EPS) * qn_ref[...]
    ckvn = ckv * lax.rsqrt(jnp.mean(ckv * ckv, axis=-1, keepdims=True) + EPS) * kvn_ref[...]
    qa = jnp.dot(cqn.astype(BF16), wq_ref[...], preferred_element_type=F32)
    ckvb = ckvn.astype(BF16)
    kn = jnp.dot(ckvb, wk_ref[...], preferred_element_type=F32)
    vt = lax.dot_general(wvt_ref[...], ckvb, (((1,), (1,)), ((), ())), preferred_element_type=F32).astype(BF16)
    ones_rows = jnp.ones((B_VT_ROWS - B_VDIM, vt.shape[1]), BF16)
    for h in range(B_HEADS):
        vt_out[h * B_VT_ROWS:h * B_VT_ROWS + B_VDIM, :] = vt[h * B_VDIM:(h + 1) * B_VDIM, :]
        vt_out[h * B_VT_ROWS + B_VDIM:(h + 1) * B_VT_ROWS, :] = ones_rows
    cos = c_ref[...]
    sin = s_ref[...]
    k_rot = (kpe * cos + kpe_sw * sin).astype(BF16)
    sw0 = B_HEADS * B_QK_PAD
    for h in range(B_HEADS):
        c0 = h * B_QK_PAD
        q_out[:, c0:c0 + B_NOPE] = (qa[:, c0:c0 + B_NOPE] * scale).astype(BF16)
        pe = qa[:, c0 + B_NOPE:c0 + B_QK_PAD]
        pe_sw = qa[:, sw0 + h * LANE:sw0 + (h + 1) * LANE]
        q_out[:, c0 + B_NOPE:c0 + B_QK_PAD] = ((pe * cos + pe_sw * sin) * scale).astype(BF16)
        k_out[:, c0:c0 + B_NOPE] = kn[:, h * B_NOPE:(h + 1) * B_NOPE].astype(BF16)
        k_out[:, c0 + B_NOPE:c0 + B_QK_PAD] = k_rot


def _mla_prep(bproj, cos, sin, q_norm, kv_norm, wq, wk, wvt, tm=256):
    T = bproj.shape[0]
    scale = (B_NOPE + B_ROPE) ** -0.5 * LOG2_E
    qk_w = B_HEADS * B_QK_PAD
    v_w = B_HEADS * B_VT_ROWS
    row = lambda i: (i, 0)
    fixed = lambda i: (0, 0)
    return pl.pallas_call(
        functools.partial(_mla_prep_kernel, scale=scale),
        grid=(T // tm,),
        in_specs=[
            pl.BlockSpec((tm, bproj.shape[1]), row),
            pl.BlockSpec((tm, LANE), row),
            pl.BlockSpec((tm, LANE), row),
            pl.BlockSpec((1, B_Q_LORA), fixed),
            pl.BlockSpec((1, B_KV_LORA), fixed),
            pl.BlockSpec(wq.shape, fixed),
            pl.BlockSpec(wk.shape, fixed),
            pl.BlockSpec(wvt.shape, fixed),
        ],
        out_specs=[pl.BlockSpec((tm, qk_w), row), pl.BlockSpec((tm, qk_w), row),
                   pl.BlockSpec((v_w, tm), lambda i: (0, i))],
        out_shape=[jax.ShapeDtypeStruct((T, qk_w), BF16), jax.ShapeDtypeStruct((T, qk_w), BF16),
                   jax.ShapeDtypeStruct((v_w, T), BF16)],
        compiler_params=_params("parallel"),
        name="mla_prep",
    )(bproj, cos, sin, q_norm.reshape(1, -1), kv_norm.reshape(1, -1), wq, wk, wvt)


def _mla_attn_kernel(qt_ref, kt_ref, q_ref, k_ref, vt_ref, o_ref, m_sc, acc_sc, *, tq, tk):
    p = pl.program_id(2)
    qi = qt_ref[p]
    kj = kt_ref[p]

    @pl.when(kj == 0)
    def _():
        m_sc[...] = jnp.full_like(m_sc, NEG_INF)
        acc_sc[...] = jnp.zeros_like(acc_sc)

    def online_softmax_step(s):
        m_prev = m_sc[...]
        m_new = jnp.maximum(m_prev, jnp.max(s, axis=0, keepdims=True))
        alpha = jnp.exp2(m_prev - m_new)
        pr = jnp.exp2((s - m_new).astype(BF16))
        acc_sc[...] = alpha * acc_sc[...] + jnp.dot(vt_ref[...], pr, preferred_element_type=F32)
        m_sc[...] = m_new

    def scores():
        return lax.dot_general(k_ref[...], q_ref[...], (((1,), (1,)), ((), ())), preferred_element_type=F32)

    k_end = (kj + 1) * tk
    crosses_diagonal = k_end > qi * tq

    @pl.when(jnp.logical_not(crosses_diagonal))
    def _():
        online_softmax_step(scores())

    @pl.when(crosses_diagonal)
    def _():
        s = scores()
        kc = (kj * tk + lax.broadcasted_iota(I32, s.shape, 0)) // CHUNK
        qc = (qi * tq + lax.broadcasted_iota(I32, s.shape, 1)) // CHUNK
        online_softmax_step(jnp.where(kc <= qc, s, NEG_INF))

    @pl.when(k_end == (qi + 1) * tq)
    def _():
        row_sum = acc_sc[B_VDIM:B_VDIM + 1, :]
        o_ref[...] = (acc_sc[:B_VDIM, :] / row_sum).T.astype(o_ref.dtype)


def _mla_attention(qp, kp, vt, B, S):
    T = qp.shape[0]
    tq = min(B_TQ, S)
    tk = min(B_TK, S)
    nq = S // tq
    nk = S // tk
    pairs = [(qi, kj) for qi in range(nq) for kj in range((qi + 1) * tq // tk)]
    qt = jnp.asarray([p[0] for p in pairs], I32)
    kt = jnp.asarray([p[1] for p in pairs], I32)
    grid_spec = pltpu.PrefetchScalarGridSpec(
        num_scalar_prefetch=2,
        grid=(B, B_HEADS, len(pairs)),
        in_specs=[
            pl.BlockSpec((tq, B_QK_PAD), lambda b, h, p, qt, kt: (b * nq + qt[p], h)),
            pl.BlockSpec((tk, B_QK_PAD), lambda b, h, p, qt, kt: (b * nk + kt[p], h)),
            pl.BlockSpec((B_VT_ROWS, tk), lambda b, h, p, qt, kt: (h, b * nk + kt[p])),
        ],
        out_specs=pl.BlockSpec((tq, B_VDIM), lambda b, h, p, qt, kt: (b * nq + qt[p], h)),
        scratch_shapes=[pltpu.VMEM((1, tq), F32), pltpu.VMEM((B_VT_ROWS, tq), F32)],
    )
    return pl.pallas_call(
        functools.partial(_mla_attn_kernel, tq=tq, tk=tk),
        grid_spec=grid_spec,
        out_shape=jax.ShapeDtypeStruct((T, B_HEADS * B_VDIM), BF16),
        compiler_params=_params("parallel", "parallel", "arbitrary"),
        name="mla_attention",
    )(qt, kt, qp, kp, vt)


def _mixer_c_kernel(z_ref, lg_ref, lb_ref, ws_ref, bs_ref, o_ref):
    tm = z_ref.shape[0]
    u = z_ref[:, :C_WIDTH].astype(F32)
    v = z_ref[:, C_WIDTH:].astype(F32)
    mu = jnp.mean(v, axis=-1, keepdims=True)
    var = jnp.mean(jnp.square(v - mu), axis=-1, keepdims=True)
    vn = ((v - mu) * lax.rsqrt(var + EPS) * lg_ref[...] + lb_ref[...]).astype(BF16)
    row = lax.broadcasted_iota(I32, (C_BLOCK, C_BLOCK), 0)
    col = lax.broadcasted_iota(I32, (C_BLOCK, C_BLOCK), 1)
    causal = col <= row
    gd = C_WIDTH // C_GROUPS
    for g in range(C_GROUPS):
        w = jnp.where(causal, ws_ref[g], 0.0).astype(BF16)
        b = bs_ref[:, g:g + 1]
        for n in range(tm // C_BLOCK):
            rows = slice(n * C_BLOCK, (n + 1) * C_BLOCK)
            cols = slice(g * gd, (g + 1) * gd)
            mixed = jnp.dot(w, vn[rows, cols], preferred_element_type=F32) + b
            o_ref[rows, cols] = (u[rows, cols] * mixed).astype(o_ref.dtype)


def _mixer_c(z, ln_g, ln_b, w_s, b_s, tm=256):
    T = z.shape[0]
    return pl.pallas_call(
        _mixer_c_kernel,
        grid=(T // tm,),
        in_specs=[
            pl.BlockSpec((tm, 2 * C_WIDTH), lambda i: (i, 0)),
            pl.BlockSpec((1, C_WIDTH), lambda i: (0, 0)),
            pl.BlockSpec((1, C_WIDTH), lambda i: (0, 0)),
            pl.BlockSpec((C_GROUPS, C_BLOCK, C_BLOCK), lambda i: (0, 0, 0)),
            pl.BlockSpec((C_BLOCK, C_GROUPS), lambda i: (0, 0)),
        ],
        out_specs=pl.BlockSpec((tm, C_WIDTH), lambda i: (i, 0)),
        out_shape=jax.ShapeDtypeStruct((T, C_WIDTH), BF16),
        compiler_params=_params("parallel"),
        name="mixer_c",
    )(z, ln_g.reshape(1, -1), ln_b.reshape(1, -1), w_s, b_s.T)


def _branch_kernel(ya_ref, yb_ref, yc_ref, wb_ref, ga_ref, gb_ref, gc_ref, o_ref):
    acc = ga_ref[...].astype(F32) * jnp.dot(ya_ref[...], wb_ref[0], preferred_element_type=F32)
    acc += gb_ref[...].astype(F32) * jnp.dot(yb_ref[...], wb_ref[1], preferred_element_type=F32)
    acc += gc_ref[...].astype(F32) * jnp.dot(yc_ref[...], wb_ref[2], preferred_element_type=F32)
    o_ref[...] = acc.astype(o_ref.dtype)


def _branch_merge(ya, yb, yc, wb, gates, tm=1024, tn=512):
    T, W = ya.shape
    D = wb.shape[2]
    tm = min(tm, T)
    nj = D // tn
    y_spec = pl.BlockSpec((tm, W), lambda i, j: (i, 0))
    return pl.pallas_call(
        _branch_kernel,
        grid=(T // tm, nj),
        in_specs=[
            y_spec, y_spec, y_spec,
            pl.BlockSpec((3, W, tn), lambda i, j: (0, 0, j)),
            pl.BlockSpec((tm, tn), lambda i, j: (i, j)),
            pl.BlockSpec((tm, tn), lambda i, j: (i, nj + j)),
            pl.BlockSpec((tm, tn), lambda i, j: (i, 2 * nj + j)),
        ],
        out_specs=pl.BlockSpec((tm, tn), lambda i, j: (i, j)),
        out_shape=jax.ShapeDtypeStruct((T, D), BF16),
        compiler_params=_params("parallel", "arbitrary"),
        name="branch_merge",
    )(ya, yb, yc, wb, gates, gates, gates)


def _route_kernel(x_ref, g_ref, sc_ref, sh_ref, wrh_ref, wrl_ref, rb_ref,
                  hb_ref, ew_ref, lpos_ref, cnt_ref):
    tm = x_ref.shape[0]
    h = _modulated_norm(x_ref[...], g_ref[...], sc_ref[...], sh_ref[...])
    hb = h.astype(BF16)
    hb_ref[...] = hb
    h_lo = (h - hb.astype(F32)).astype(BF16)
    dn = (((1,), (1,)), ((), ()))
    logits = (lax.dot_general(wrh_ref[...], hb, dn, preferred_element_type=F32)
              + lax.dot_general(wrh_ref[...], h_lo, dn, preferred_element_type=F32)
              + lax.dot_general(wrl_ref[...], hb, dn, preferred_element_type=F32))
    scores = jax.nn.sigmoid(logits)
    sel = scores + rb_ref[...]
    neg = -jnp.inf
    n_groups = N_EXPERTS // EXPERT_GROUP

    i8 = lax.broadcasted_iota(I32, (EXPERT_GROUP, tm), 0)
    group_scores = []
    for g in range(n_groups):
        blk = sel[g * EXPERT_GROUP:(g + 1) * EXPERT_GROUP, :]
        m1 = jnp.max(blk, axis=0, keepdims=True)
        first = jnp.min(jnp.where(blk == m1, i8, EXPERT_GROUP), axis=0, keepdims=True)
        m2 = jnp.max(jnp.where(i8 == first, neg, blk), axis=0, keepdims=True)
        group_scores.append(m1 + m2)
    gs = jnp.concatenate(group_scores, axis=0)

    ig = lax.broadcasted_iota(I32, (n_groups, tm), 0)
    gsel = jnp.zeros((n_groups, tm), F32)
    for _ in range(TOPK_GROUPS):
        m = jnp.max(gs, axis=0, keepdims=True)
        first = jnp.min(jnp.where(gs == m, ig, n_groups), axis=0, keepdims=True)
        hit = ig == first
        gsel = jnp.where(hit, 1.0, gsel)
        gs = jnp.where(hit, neg, gs)

    masked = jnp.concatenate(
        [jnp.where(gsel[g:g + 1, :] > 0.0, sel[g * EXPERT_GROUP:(g + 1) * EXPERT_GROUP, :], neg)
         for g in range(n_groups)], axis=0)

    ie = lax.broadcasted_iota(I32, (N_EXPERTS, tm), 0)
    chosen = jnp.zeros((N_EXPERTS, tm), F32)
    hits, ws = [], []
    for _ in range(TOP_K):
        m = jnp.max(masked, axis=0, keepdims=True)
        first = jnp.min(jnp.where(masked == m, ie, N_EXPERTS), axis=0, keepdims=True)
        hit = ie == first
        ws.append(jnp.sum(jnp.where(hit, scores, 0.0), axis=0, keepdims=True))
        hits.append(hit)
        chosen = jnp.where(hit, 1.0, chosen)
        masked = jnp.where(hit, neg, masked)
    w = jnp.concatenate(ws, axis=0)
    denom = jnp.sum(w, axis=0, keepdims=True) + 1e-20
    ew_ref[...] = w / denom * ROUTED_SCALE

    s_idx = lax.broadcasted_iota(I32, (tm, tm), 0)
    t_idx = lax.broadcasted_iota(I32, (tm, tm), 1)
    earlier = jnp.where(s_idx < t_idx, 1.0, 0.0).astype(BF16)
    local_rank = jnp.dot(chosen.astype(BF16), earlier, preferred_element_type=F32)
    n_e = jnp.sum(chosen, axis=1, keepdims=True)
    run = jnp.floor((n_e + (ROW_ALIGN - 1.0)) * (1.0 / ROW_ALIGN)) * ROW_ALIGN
    run_b = jnp.broadcast_to(run, (N_EXPERTS, LANE))
    e_row = lax.broadcasted_iota(I32, (N_EXPERTS, N_EXPERTS), 0)
    e_col = lax.broadcasted_iota(I32, (N_EXPERTS, N_EXPERTS), 1)
    before = jnp.where(e_col < e_row, 1.0, 0.0).astype(BF16)
    run_start = jnp.dot(before, run_b.astype(BF16), preferred_element_type=F32)
    base = run_start[:, :1] + local_rank
    lpos = [jnp.sum(jnp.where(hit, base, 0.0), axis=0, keepdims=True) for hit in hits]
    lpos_ref[...] = jnp.concatenate(lpos, axis=0).astype(I32)
    cnt_ref[...] = run_b.astype(I32)


def _route(x, g, mod3, row_sc, row_sh, w_router, router_bias, S):
    T, D = x.shape
    tm = min(TOKEN_TILE, S)
    per = S // tm
    wr_t = w_router.T
    wr_hi = wr_t.astype(BF16)
    wr_lo = (wr_t - wr_hi.astype(F32)).astype(BF16)
    row = lambda i: (i, 0)
    col = lambda i: (0, i)
    fixed = lambda i: (0, 0)
    return pl.pallas_call(
        _route_kernel,
        grid=(T // tm,),
        in_specs=[
            pl.BlockSpec((tm, D), row),
            pl.BlockSpec((1, D), fixed),
            pl.BlockSpec((None, 1, D), lambda i: (row_sc(i // per), 0, 0)),
            pl.BlockSpec((None, 1, D), lambda i: (row_sh(i // per), 0, 0)),
            pl.BlockSpec((N_EXPERTS, D), fixed),
            pl.BlockSpec((N_EXPERTS, D), fixed),
            pl.BlockSpec((N_EXPERTS, 1), fixed),
        ],
        out_specs=[
            pl.BlockSpec((tm, D), row),
            pl.BlockSpec((TOP_K, tm), col),
            pl.BlockSpec((TOP_K, tm), col),
            pl.BlockSpec((N_EXPERTS, LANE), col),
        ],
        out_shape=[
            jax.ShapeDtypeStruct((T, D), BF16),
            jax.ShapeDtypeStruct((TOP_K, T), F32),
            jax.ShapeDtypeStruct((TOP_K, T), I32),
            jax.ShapeDtypeStruct((N_EXPERTS, (T // tm) * LANE), I32),
        ],
        compiler_params=_params("parallel"),
        name="norm_route",
    )(x, g.reshape(1, D), mod3, mod3, wr_hi, wr_lo, router_bias.reshape(N_EXPERTS, 1))


def _moe_tables(cnt, n_tiles):
    runs = cnt[:, ::LANE].T
    total = jnp.sum(runs, axis=0)
    region = (total + EXPERT_TILE - 1) // EXPERT_TILE * EXPERT_TILE
    region_end = jnp.cumsum(region)
    region_start = region_end - region
    dst = region_start[None, :] + jnp.cumsum(runs, axis=0) - runs
    src = jnp.cumsum(runs, axis=1) - runs
    tile_end = region_end // EXPERT_TILE
    tile_expert = jnp.sum(tile_end[None, :] <= jnp.arange(n_tiles, dtype=I32)[:, None], axis=1)
    tile_expert = jnp.minimum(tile_expert, N_EXPERTS - 1).astype(I32)
    as_i32 = lambda a: a.reshape(-1).astype(I32)
    gap_start = jnp.concatenate([region_start + total, region_end[-1:]])
    gap_rows = jnp.concatenate([region - total, n_tiles * EXPERT_TILE - region_end[-1:]])
    return dict(dst=as_i32(dst), src=as_i32(src), chunks=as_i32(runs // ROW_ALIGN),
                gap_start=as_i32(gap_start), gap_chunks=as_i32(gap_rows // ROW_ALIGN),
                tile_expert=tile_expert, n_used=as_i32(tile_end[-1:]))


def _run_copies(i, src_ref, dst_ref, chunks_ref, make_copy, wait):
    for e in range(N_EXPERTS):
        s0 = src_ref[i * N_EXPERTS + e]
        d0 = dst_ref[i * N_EXPERTS + e]

        def body(c, carry, s0=s0, d0=d0):
            cp = make_copy(pl.multiple_of(s0 + c * ROW_ALIGN, ROW_ALIGN), pl.multiple_of(d0 + c * ROW_ALIGN, ROW_ALIGN))
            cp.wait() if wait else cp.start()
            return carry
        lax.fori_loop(0, chunks_ref[i * N_EXPERTS + e], body, 0)


def _dispatch_kernel(src_ref, dst_ref, chunks_ref, ts_ref, tc_ref, hb_ref, lp_ref, xs_hbm,
                     xloc, pbuf, zrows, sem):
    i = pl.program_id(0)
    n = pl.num_programs(0)
    tt = hb_ref.shape[0]
    slot = i % 2

    @pl.when(i == 0)
    def _():
        zrows[...] = jnp.zeros_like(zrows)
        for wait in (False, True):
            for e in range(N_EXPERTS + 1):
                def body(c, carry, e=e, wait=wait):
                    cp = pltpu.make_async_copy(
                        zrows, xs_hbm.at[pl.ds(pl.multiple_of(ts_ref[e] + c * ROW_ALIGN, ROW_ALIGN), ROW_ALIGN)],
                        sem.at[0])
                    cp.wait() if wait else cp.start()
                    return carry
                lax.fori_loop(0, tc_ref[e], body, 0)

    lp = lp_ref[...]

    def onehot_block(rb, carry):
        r0 = pl.multiple_of(rb * PERM_BLOCK, PERM_BLOCK)
        rows = r0 + lax.broadcasted_iota(I32, (PERM_BLOCK, tt), 0)
        onehot = jnp.zeros((PERM_BLOCK, tt), F32)
        for k in range(TOP_K):
            onehot = jnp.where(rows == lp[k:k + 1, :], 1.0, onehot)
        pbuf[pl.ds(r0, PERM_BLOCK), :] = onehot.astype(BF16)
        return carry
    lax.fori_loop(0, pbuf.shape[0] // PERM_BLOCK, onehot_block, 0)

    perm = pbuf[...]
    for c in range(HALF_D // PERM_COLS):
        lo_cols = slice(c * PERM_COLS, (c + 1) * PERM_COLS)
        hi_cols = slice(HALF_D + c * PERM_COLS, HALF_D + (c + 1) * PERM_COLS)
        lo = jnp.dot(perm, hb_ref[:, lo_cols], preferred_element_type=F32)
        hi = jnp.dot(perm, hb_ref[:, hi_cols], preferred_element_type=F32)
        xloc[slot, :, lo_cols] = _pack_bf16_pair(lo, hi)

    def copies_from(buf_slot):
        def make_copy(s, d):
            return pltpu.make_async_copy(xloc.at[buf_slot, pl.ds(s, ROW_ALIGN)], xs_hbm.at[pl.ds(d, ROW_ALIGN)],
                                         sem.at[buf_slot])
        return make_copy
    _run_copies(i, src_ref, dst_ref, chunks_ref, copies_from(slot), wait=False)

    @pl.when(i > 0)
    def _():
        _run_copies(i - 1, src_ref, dst_ref, chunks_ref, copies_from(1 - slot), wait=True)

    @pl.when(i == n - 1)
    def _():
        _run_copies(i, src_ref, dst_ref, chunks_ref, copies_from(slot), wait=True)


def _local_rows(tt):
    rows = TOP_K * tt + N_EXPERTS * ROW_ALIGN
    return (rows + COMBINE_BLOCK - 1) // COMBINE_BLOCK * COMBINE_BLOCK


def _dispatch(hb, lpos, tables, n_rows, S):
    T, D = hb.shape
    tt = min(TOKEN_TILE, S)
    grid_spec = pltpu.PrefetchScalarGridSpec(
        num_scalar_prefetch=5,
        grid=(T // tt,),
        in_specs=[
            pl.BlockSpec((tt, D), lambda i, *_: (i, 0)),
            pl.BlockSpec((TOP_K, tt), lambda i, *_: (0, i)),
        ],
        out_specs=pl.BlockSpec(memory_space=pl.ANY),
        scratch_shapes=[pltpu.VMEM((2, _local_rows(tt), HALF_D), U32), pltpu.VMEM((_local_rows(tt), tt), BF16),
                        pltpu.VMEM((ROW_ALIGN, HALF_D), U32), pltpu.SemaphoreType.DMA((2,))],
    )
    return pl.pallas_call(
        _dispatch_kernel,
        grid_spec=grid_spec,
        out_shape=jax.ShapeDtypeStruct((n_rows, HALF_D), U32),
        compiler_params=_params("arbitrary"),
        name="moe_dispatch",
    )(tables["src"], tables["dst"], tables["chunks"], tables["gap_start"], tables["gap_chunks"], hb, lpos)


def _expert_kernel(te_ref, nu_ref, x_ref, wg_ref, wu_ref, wd_ref, y_ref, wgu_bf, wd_bf):
    i = pl.program_id(0)

    @pl.when(i < nu_ref[0])
    def _():
        @pl.when((i == 0) | (te_ref[i] != te_ref[jnp.maximum(i - 1, 0)]))
        def _():
            wgu_bf[:, :D_EXPERT] = wg_ref[...].astype(BF16)
            wgu_bf[:, D_EXPERT:] = wu_ref[...].astype(BF16)
            wd_bf[...] = wd_ref[...].astype(BF16)

        x_lo, x_hi = _unpack_halves(x_ref[...])
        gu = jnp.dot(x_lo.astype(BF16), wgu_bf[:HALF_D, :], preferred_element_type=F32)
        gu += jnp.dot(x_hi.astype(BF16), wgu_bf[HALF_D:, :], preferred_element_type=F32)
        gate = gu[:, :D_EXPERT]
        act = (gate * jax.nn.sigmoid(gate) * gu[:, D_EXPERT:]).astype(BF16)
        y = jnp.dot(act, wd_bf[...], preferred_element_type=F32)
        y_ref[...] = _pack_halves(y)

    @pl.when(i >= nu_ref[0])
    def _():
        y_ref[...] = jnp.zeros_like(y_ref)


def _experts(xs, tables, we_gate, we_up, we_down, layer):
    tm = EXPERT_TILE
    n_tiles = xs.shape[0] // tm
    used = lambda i, nu: jnp.minimum(i, nu[0] - 1)
    expert_block = lambda i, te, nu: (layer, te[used(i, nu)], 0, 0)
    row_block = lambda i, te, nu: (used(i, nu), 0)
    grid_spec = pltpu.PrefetchScalarGridSpec(
        num_scalar_prefetch=2,
        grid=(n_tiles,),
        in_specs=[
            pl.BlockSpec((tm, HALF_D), row_block),
            pl.BlockSpec((None, None, D_MODEL, D_EXPERT), expert_block),
            pl.BlockSpec((None, None, D_MODEL, D_EXPERT), expert_block),
            pl.BlockSpec((None, None, D_EXPERT, D_MODEL), expert_block),
        ],
        out_specs=pl.BlockSpec((tm, HALF_D), lambda i, te, nu: (i, 0)),
        scratch_shapes=[pltpu.VMEM((D_MODEL, 2 * D_EXPERT), BF16), pltpu.VMEM((D_EXPERT, D_MODEL), BF16)],
    )
    return pl.pallas_call(
        _expert_kernel,
        grid_spec=grid_spec,
        out_shape=jax.ShapeDtypeStruct(xs.shape, U32),
        compiler_params=_params("arbitrary"),
        name="experts",
    )(tables["tile_expert"], tables["n_used"], xs, we_gate, we_up, we_down)


def _shared_expert_kernel(h_ref, wgu_ref, wd_ref, o_ref):
    gu = jnp.dot(h_ref[...], wgu_ref[...], preferred_element_type=F32)
    gate = gu[:, :D_EXPERT]
    act = (gate * jax.nn.sigmoid(gate) * gu[:, D_EXPERT:]).astype(BF16)
    o_ref[...] = jnp.dot(act, wd_ref[...], preferred_element_type=F32).astype(o_ref.dtype)


def _shared_expert(hb, wsgu, wsd, tm=512):
    T, D = hb.shape
    return pl.pallas_call(
        _shared_expert_kernel,
        grid=(T // tm,),
        in_specs=[pl.BlockSpec((tm, D), lambda i: (i, 0)), pl.BlockSpec(wsgu.shape, lambda i: (0, 0)),
                  pl.BlockSpec(wsd.shape, lambda i: (0, 0))],
        out_specs=pl.BlockSpec((tm, D), lambda i: (i, 0)),
        out_shape=jax.ShapeDtypeStruct((T, D), BF16),
        compiler_params=_params("parallel"),
        name="shared_expert",
    )(hb, wsgu, wsd)


def _combine_kernel(src_ref, dst_ref, chunks_ref, lp_ref, ew_ref, x_ref, sh_ref, g2_ref, fg_ref, ys_hbm,
                    o_ref, yloc, acc, sem, *, final_norm):
    i = pl.program_id(0)
    n = pl.num_programs(0)
    tt = x_ref.shape[0]
    slot = i % 2

    def copies_into(buf_slot):
        def make_copy(s, d):
            return pltpu.make_async_copy(ys_hbm.at[pl.ds(d, ROW_ALIGN)], yloc.at[buf_slot, pl.ds(s, ROW_ALIGN)],
                                         sem.at[buf_slot])
        return make_copy

    @pl.when(i == 0)
    def _():
        yloc[...] = jnp.zeros_like(yloc)
        _run_copies(i, src_ref, dst_ref, chunks_ref, copies_into(0), wait=False)

    _run_copies(i, src_ref, dst_ref, chunks_ref, copies_into(slot), wait=True)

    @pl.when(i + 1 < n)
    def _():
        _run_copies(i + 1, src_ref, dst_ref, chunks_ref, copies_into(1 - slot), wait=False)

    lp = lp_ref[...]
    ew = ew_ref[...]
    acc[...] = jnp.zeros_like(acc)
    last_run = i * N_EXPERTS + N_EXPERTS - 1
    tile_rows = src_ref[last_run] + chunks_ref[last_run] * ROW_ALIGN

    def sum_block(rb, carry):
        r0 = pl.multiple_of(rb * COMBINE_BLOCK, COMBINE_BLOCK)
        rows = r0 + lax.broadcasted_iota(I32, (COMBINE_BLOCK, tt), 0)
        wmat = jnp.zeros((COMBINE_BLOCK, tt), F32)
        for k in range(TOP_K):
            wmat = jnp.where(rows == lp[k:k + 1, :], ew[k:k + 1, :], wmat)
        w_row = jnp.sum(wmat, axis=1, keepdims=True)
        onehot = jnp.where(wmat != 0.0, 1.0, 0.0).astype(BF16)
        y_lo, y_hi = _unpack_halves(yloc[slot, pl.ds(r0, COMBINE_BLOCK), :])
        dn = (((0,), (0,)), ((), ()))
        acc[:HALF_D, :] += lax.dot_general((w_row * y_lo).astype(BF16), onehot, dn, preferred_element_type=F32)
        acc[HALF_D:, :] += lax.dot_general((w_row * y_hi).astype(BF16), onehot, dn, preferred_element_type=F32)
        return carry
    lax.fori_loop(0, pl.cdiv(tile_rows, COMBINE_BLOCK), sum_block, 0)

    out = x_ref[...] + g2_ref[...] * (sh_ref[...].astype(F32) + acc[...].T)
    if final_norm:
        out = out * lax.rsqrt(jnp.mean(out * out, axis=-1, keepdims=True) + EPS) * fg_ref[...]
    o_ref[...] = out


def _combine(x, shared, lpos, ew, mod3, row_g, tables, ys, final_g, final_norm, S):
    T, D = x.shape
    tt = min(TOKEN_TILE, S)
    per = S // tt
    row = lambda i, *_: (i, 0)
    col = lambda i, *_: (0, i)
    grid_spec = pltpu.PrefetchScalarGridSpec(
        num_scalar_prefetch=3,
        grid=(T // tt,),
        in_specs=[
            pl.BlockSpec((TOP_K, tt), col),
            pl.BlockSpec((TOP_K, tt), col),
            pl.BlockSpec((tt, D), row),
            pl.BlockSpec((tt, D), row),
            pl.BlockSpec((None, 1, D), lambda i, *_: (row_g(i // per), 0, 0)),
            pl.BlockSpec((1, D), lambda i, *_: (0, 0)),
            pl.BlockSpec(memory_space=pl.ANY),
        ],
        out_specs=pl.BlockSpec((tt, D), row),
        scratch_shapes=[pltpu.VMEM((2, _local_rows(tt), HALF_D), U32), pltpu.VMEM((D, tt), F32),
                        pltpu.SemaphoreType.DMA((2,))],
    )
    return pl.pallas_call(
        functools.partial(_combine_kernel, final_norm=final_norm),
        grid_spec=grid_spec,
        out_shape=jax.ShapeDtypeStruct((T, D), F32),
        compiler_params=_params("arbitrary"),
        name="moe_combine",
    )(tables["src"], tables["dst"], tables["chunks"], lpos, ew, x, shared, mod3, final_g.reshape(1, D), ys)


def _swap_halves(w):
    half = w.shape[-1] // 2
    return jnp.concatenate([w[..., half:], w[..., :half]], axis=-1)


def _split_w_in(w_in):
    a_cols = 3 * A_WIDTH
    b_cols = B_Q_LORA + B_KV_LORA + B_ROPE
    c_cols = 2 * C_WIDTH
    wa = w_in[:, :a_cols].astype(BF16)
    wb_raw = w_in[:, a_cols:a_cols + b_cols]
    kpe = wb_raw[:, B_Q_LORA + B_KV_LORA:]
    zeros = jnp.zeros_like(kpe)
    wb = jnp.concatenate([wb_raw[:, :B_Q_LORA + B_KV_LORA], kpe, zeros, _swap_halves(kpe), zeros],
                         axis=1).astype(BF16)
    wc = w_in[:, a_cols + b_cols:a_cols + b_cols + c_cols].astype(BF16)
    wg = w_in[:, a_cols + b_cols + c_cols:].astype(BF16)
    return wa, wb, wc, wg


def _mla_weights(w_uq, w_ukv):
    r = w_uq.shape[0]
    wq = w_uq.reshape(r, B_HEADS, B_NOPE + B_ROPE)
    nope, pe = wq[..., :B_NOPE], wq[..., B_NOPE:]
    zeros = jnp.zeros_like(pe)
    main = jnp.concatenate([nope, pe, zeros], axis=-1).reshape(r, B_HEADS * B_QK_PAD)
    swapped = jnp.concatenate([_swap_halves(pe), zeros], axis=-1).reshape(r, B_HEADS * LANE)
    wq_ext = jnp.concatenate([main, swapped], axis=1).astype(BF16)
    rk = w_ukv.shape[0]
    wkv = w_ukv.reshape(rk, B_HEADS, B_NOPE + B_VDIM)
    wk = wkv[..., :B_NOPE].reshape(rk, -1).astype(BF16)
    wvt = wkv[..., B_NOPE:].reshape(rk, -1).T.astype(BF16)
    return wq_ext, wk, wvt


def kernel(x, c, positions, rel_bias, norm_mix, norm_ffn, w_mod, b_mod, w_in, mla_q_norm, mla_w_uq, mla_kv_norm, mla_w_ukv, gmlp_ln_g, gmlp_ln_b, gmlp_w_s, gmlp_b_s, w_branch, w_out, w_router, router_bias, we_gate, we_up, we_down, ws_gate, ws_up, ws_down, final_norm):
    B, S, D = x.shape
    T = B * S
    L = w_mod.shape[0]
    xt = x.reshape(T, D)

    mod = _modulation(c, w_mod, b_mod)
    mod3 = mod.reshape(L * B * 6, 1, D)
    cos, sin = _rope_tables(positions)
    a_bias = _mixer_a_bias(rel_bias)
    n_runs = (T // min(TOKEN_TILE, S)) * N_EXPERTS
    n_tiles = pl.cdiv(T * TOP_K + n_runs * (ROW_ALIGN - 1), EXPERT_TILE) + N_EXPERTS

    for l in range(L):
        row = lambda k: (lambda b, l=l, k=k: (l * B + b) * 6 + k)
        wa, wb, wc, wg = _split_w_in(w_in[l])
        wq_ext, wk_nope, wv_t = _mla_weights(mla_w_uq[l], mla_w_ukv[l])

        h = _norm_mod(xt, norm_mix[l], mod3, row(1), row(0), S)
        qkv = _matmul(h, wa, BF16, name="proj_a")
        bproj = _matmul(h, wb, F32, name="proj_b")
        z = _matmul(h, wc, BF16, act="gelu", name="proj_c")
        gates = _matmul(h, wg, BF16, act="sigmoid", name="proj_gates")

        y_a = _mixer_a(qkv, a_bias, B, S)
        qp, kp, vt = _mla_prep(bproj, cos, sin, mla_q_norm[l], mla_kv_norm[l], wq_ext, wk_nope, wv_t)
        y_b = _mla_attention(qp, kp, vt, B, S)
        y_c = _mixer_c(z, gmlp_ln_g[l], gmlp_ln_b[l], gmlp_w_s[l], gmlp_b_s[l])

        merged = _branch_merge(y_a, y_b, y_c, w_branch[l].astype(BF16), gates)
        xt = _matmul_residual(merged, w_out[l].astype(BF16), xt, mod3, row(2), S)

        hb, ew, lpos, cnt = _route(xt, norm_ffn[l], mod3, row(4), row(3), w_router[l], router_bias[l], S)
        tables = _moe_tables(cnt, n_tiles)
        xs = _dispatch(hb, lpos, tables, n_tiles * EXPERT_TILE, S)
        ys = _experts(xs, tables, we_gate, we_up, we_down, l)
        wsgu = jnp.concatenate([ws_gate[l], ws_up[l]], axis=1).astype(BF16)
        shared = _shared_expert(hb, wsgu, ws_down[l].astype(BF16))
        xt = _combine(xt, shared, lpos, ew, mod3, row(5), tables, ys, final_norm, l == L - 1, S)

    return xt.reshape(B, S, D)
```

```python
import functools

import jax
import jax.numpy as jnp
from jax import lax
from jax.experimental import pallas as pl
from jax.experimental.pallas import tpu as pltpu

F32 = jnp.float32
BF16 = jnp.bfloat16
U32 = jnp.uint32
I32 = jnp.int32

EPS = 1e-6
NEG_INF = -1e30

D_MODEL = 2048
HALF_D = D_MODEL // 2
CHUNK = 64

A_HEADS = 16
A_HEAD_DIM = 64
A_WIDTH = A_HEADS * A_HEAD_DIM
A_LEFT_CHUNKS = 8
MAX_REL = 128
A_QBLOCK = 256
A_KBLOCKS = 3
A_PAIRS_PER_STEP = 2

B_HEADS = 8
B_NOPE = 128
B_ROPE = 64
B_VDIM = 128
B_Q_LORA = 512
B_KV_LORA = 256
B_QK_PAD = 256
B_VT_ROWS = B_VDIM + 16
LOG2_E = 1.4426950408889634
ROPE_THETA = 10000.0
B_TQ = 1024
B_TK = 1024

C_BLOCK = 128
C_GROUPS = 8
C_WIDTH = 1024

N_EXPERTS = 64
EXPERT_GROUP = 8
TOP_K = 8
TOPK_GROUPS = 4
D_EXPERT = 512
ROUTED_SCALE = 2.5
EXPERT_TILE = 512
TOKEN_TILE = 256
ROW_ALIGN = 8
WAIT_ROWS = 2048
PERM_BLOCK = 256
PERM_COLS = 256
COMBINE_BLOCK = 512

LANE = 128
V7X_VMEM_LIMIT = 56 * 1024 * 1024


def _params(*sem):
    return pltpu.CompilerParams(dimension_semantics=sem, vmem_limit_bytes=V7X_VMEM_LIMIT)


def _pack_halves(v):
    w = v.shape[1] // 2
    r = v.astype(BF16).astype(F32)
    return _pack_bf16_pair(r[:, :w], r[:, w:])


def _pack_bf16_pair(lo, hi):
    return (lax.bitcast_convert_type(lo, U32) >> 16) | lax.bitcast_convert_type(hi, U32)


def _unpack_halves(p):
    lo = lax.bitcast_convert_type(p << 16, F32)
    hi = lax.bitcast_convert_type(p & jnp.uint32(0xFFFF0000), F32)
    return lo, hi


def _mod_kernel(c_ref, w_ref, b_ref, o_ref):
    c = c_ref[...]
    ca = (c * jax.nn.sigmoid(c)).astype(BF16)
    o_ref[...] = jnp.dot(ca, w_ref[...].astype(BF16), preferred_element_type=F32) + b_ref[...]


def _modulation(c, w_mod, b_mod):
    L, D, N = w_mod.shape
    B = c.shape[0]
    tn = 512
    return pl.pallas_call(
        _mod_kernel,
        grid=(L, N // tn),
        in_specs=[
            pl.BlockSpec((B, D), lambda l, j: (0, 0)),
            pl.BlockSpec((None, D, tn), lambda l, j: (l, 0, j)),
            pl.BlockSpec((None, 1, tn), lambda l, j: (l, 0, j)),
        ],
        out_specs=pl.BlockSpec((None, B, tn), lambda l, j: (l, 0, j)),
        out_shape=jax.ShapeDtypeStruct((L, B, N), F32),
        compiler_params=_params("parallel", "arbitrary"),
        name="modulation",
    )(c, w_mod, b_mod.reshape(L, 1, N))


def _modulated_norm(x, g, sc, sh):
    r = lax.rsqrt(jnp.mean(x * x, axis=-1, keepdims=True) + EPS)
    return (x * r * g) * (1.0 + sc) + sh


def _norm_mod_kernel(x_ref, g_ref, sc_ref, sh_ref, o_ref):
    o_ref[...] = _modulated_norm(x_ref[...], g_ref[...], sc_ref[...], sh_ref[...]).astype(o_ref.dtype)


def _norm_mod(x, g, mod3, row_sc, row_sh, S, tm=512):
    T, D = x.shape
    per = S // tm
    return pl.pallas_call(
        _norm_mod_kernel,
        grid=(T // tm,),
        in_specs=[
            pl.BlockSpec((tm, D), lambda i: (i, 0)),
            pl.BlockSpec((1, D), lambda i: (0, 0)),
            pl.BlockSpec((None, 1, D), lambda i: (row_sc(i // per), 0, 0)),
            pl.BlockSpec((None, 1, D), lambda i: (row_sh(i // per), 0, 0)),
        ],
        out_specs=pl.BlockSpec((tm, D), lambda i: (i, 0)),
        out_shape=jax.ShapeDtypeStruct((T, D), BF16),
        compiler_params=_params("parallel"),
        name="norm_mod",
    )(x, g.reshape(1, D), mod3, mod3)


def _mm_kernel(a_ref, w_ref, o_ref, *, act):
    acc = jnp.dot(a_ref[...], w_ref[...], preferred_element_type=F32)
    if act == "gelu":
        acc = jax.nn.gelu(acc)
    elif act == "sigmoid":
        acc = jax.nn.sigmoid(acc)
    o_ref[...] = acc.astype(o_ref.dtype)


def _matmul(a, w, out_dtype, act=None, tm=1024, tn=1024, name="matmul"):
    M, K = a.shape
    N = w.shape[1]
    tm = min(tm, M)
    return pl.pallas_call(
        functools.partial(_mm_kernel, act=act),
        grid=(M // tm, N // tn),
        in_specs=[
            pl.BlockSpec((tm, K), lambda i, j: (i, 0)),
            pl.BlockSpec((K, tn), lambda i, j: (0, j)),
        ],
        out_specs=pl.BlockSpec((tm, tn), lambda i, j: (i, j)),
        out_shape=jax.ShapeDtypeStruct((M, N), out_dtype),
        compiler_params=_params("parallel", "arbitrary"),
        name=name,
    )(a, w)


def _mm_residual_kernel(a_ref, w_ref, x_ref, g_ref, o_ref):
    acc = jnp.dot(a_ref[...], w_ref[...], preferred_element_type=F32)
    o_ref[...] = x_ref[...] + g_ref[...] * acc


def _matmul_residual(a, w, x, mod3, row_g, S, tm=1024, tn=1024):
    M, K = a.shape
    N = w.shape[1]
    tm = min(tm, S)
    per = S // tm
    return pl.pallas_call(
        _mm_residual_kernel,
        grid=(M // tm, N // tn),
        in_specs=[
            pl.BlockSpec((tm, K), lambda i, j: (i, 0)),
            pl.BlockSpec((K, tn), lambda i, j: (0, j)),
            pl.BlockSpec((tm, tn), lambda i, j: (i, j)),
            pl.BlockSpec((None, 1, tn), lambda i, j: (row_g(i // per), 0, j)),
        ],
        out_specs=pl.BlockSpec((tm, tn), lambda i, j: (i, j)),
        out_shape=jax.ShapeDtypeStruct((M, N), F32),
        compiler_params=_params("parallel", "arbitrary"),
        name="out_proj_residual",
    )(a, w, x, mod3)


def _mixer_a_kernel(q_ref, k0_ref, k1_ref, k2_ref, v0_ref, v1_ref, v2_ref, bias_ref, o_ref, *, scale):
    i = pl.program_id(2)
    qb = A_QBLOCK
    kw = A_KBLOCKS * qb
    lane = lax.broadcasted_iota(I32, (qb, LANE), 1)

    def attend(mask_before_sequence):
        ones_col = jnp.where(lax.broadcasted_iota(I32, (kw, LANE), 1) == 0, 1.0, 0.0).astype(BF16)
        for pp in range(A_PAIRS_PER_STEP):
            cols = slice(pp * LANE, (pp + 1) * LANE)
            q = q_ref[:, cols]
            k = jnp.concatenate([k0_ref[:, cols], k1_ref[:, cols], k2_ref[:, cols]], axis=0)
            v = jnp.concatenate([v0_ref[:, cols], v1_ref[:, cols], v2_ref[:, cols]], axis=0)
            v_ext = jnp.concatenate([v, ones_col], axis=1)
            outs = []
            for hh in range(2):
                head_lanes = (lane < A_HEAD_DIM) if hh == 0 else (lane >= A_HEAD_DIM)
                qm = jnp.where(head_lanes, q, jnp.zeros_like(q)) * jnp.asarray(scale, q.dtype)
                s = lax.dot_general(qm, k, (((1,), (1,)), ((), ())), preferred_element_type=F32)
                s = s + bias_ref[2 * pp + hh]
                if mask_before_sequence:
                    kpos = (i - (A_KBLOCKS - 1)) * qb + lax.broadcasted_iota(I32, (1, kw), 1)
                    s = jnp.where(kpos >= 0, s, NEG_INF)
                m = jnp.max(s, axis=-1, keepdims=True)
                p = jnp.exp((s - m).astype(BF16))
                o = jnp.dot(p, v_ext, preferred_element_type=F32)
                outs.append(o[:, :LANE] / o[:, LANE:LANE + 1])
            o_ref[:, cols] = jnp.where(lane < A_HEAD_DIM, outs[0], outs[1]).astype(o_ref.dtype)

    @pl.when(i < A_KBLOCKS - 1)
    def _():
        attend(True)

    @pl.when(i >= A_KBLOCKS - 1)
    def _():
        attend(False)


def _mixer_a_bias(rel_bias):
    qb = A_QBLOCK
    kw = A_KBLOCKS * qb
    qi = jnp.arange(qb)[:, None]
    kj = jnp.arange(kw)[None, :]
    qc = qi // CHUNK
    kc = kj // CHUNK
    first = (A_KBLOCKS - 1) * qb // CHUNK - A_LEFT_CHUNKS
    in_band = (kc >= qc + first) & (kc <= qc + first + A_LEFT_CHUNKS)
    period = kw + qb
    m = jnp.arange(period)
    k_minus_q = jnp.where(m < kw, m, m - period)
    d = (A_KBLOCKS - 1) * qb - k_minus_q
    line = rel_bias[:, jnp.clip(d, -MAX_REL, MAX_REL) + MAX_REL].astype(F32)
    heads = rel_bias.shape[0]
    flat = jnp.broadcast_to(line[:, None, :], (heads, qb, period)).reshape(heads, qb * period)
    bias = flat[:, :qb * (period - 1)].reshape(heads, qb, period - 1)[:, :, :kw]
    return jnp.where(in_band[None], bias, NEG_INF)


def _mixer_a(qkv, bias, B, S):
    T = qkv.shape[0]
    qb = A_QBLOCK
    nq = S // qb
    width = A_PAIRS_PER_STEP * LANE
    groups = A_WIDTH // width

    def kv_spec(j, col0):
        return pl.BlockSpec(
            (qb, width), lambda p, b, i: (b * nq + jnp.maximum(i - (A_KBLOCKS - 1) + j, 0), col0 + p))

    return pl.pallas_call(
        functools.partial(_mixer_a_kernel, scale=A_HEAD_DIM ** -0.5),
        grid=(groups, B, nq),
        in_specs=[
            pl.BlockSpec((qb, width), lambda p, b, i: (b * nq + i, p)),
            kv_spec(0, groups), kv_spec(1, groups), kv_spec(2, groups),
            kv_spec(0, 2 * groups), kv_spec(1, 2 * groups), kv_spec(2, 2 * groups),
            pl.BlockSpec((2 * A_PAIRS_PER_STEP, qb, A_KBLOCKS * qb), lambda p, b, i: (p, 0, 0)),
        ],
        out_specs=pl.BlockSpec((qb, width), lambda p, b, i: (b * nq + i, p)),
        out_shape=jax.ShapeDtypeStruct((T, A_WIDTH), BF16),
        compiler_params=_params("parallel", "parallel", "arbitrary"),
        name="mixer_a",
    )(qkv, qkv, qkv, qkv, qkv, qkv, qkv, bias)


def _rope_table_kernel(pos_ref, inv_ref, c_ref, s_ref):
    ang = pos_ref[...].astype(F32) * inv_ref[...]
    lane = lax.broadcasted_iota(I32, ang.shape, 1)
    first_half = (lane % B_ROPE) < (B_ROPE // 2)
    c_ref[...] = jnp.cos(ang)
    sn = jnp.sin(ang)
    s_ref[...] = jnp.where(first_half, -sn, sn)


def _rope_tables(positions, tm=512):
    T = positions.size
    half = B_ROPE // 2
    inv = ROPE_THETA ** (-jnp.arange(half, dtype=F32) / half)
    inv = jnp.tile(inv, LANE // half).reshape(1, LANE)
    return pl.pallas_call(
        _rope_table_kernel,
        grid=(T // tm,),
        in_specs=[pl.BlockSpec((tm, 1), lambda i: (i, 0)), pl.BlockSpec((1, LANE), lambda i: (0, 0))],
        out_specs=[pl.BlockSpec((tm, LANE), lambda i: (i, 0))] * 2,
        out_shape=[jax.ShapeDtypeStruct((T, LANE), F32)] * 2,
        compiler_params=_params("parallel"),
        name="rope_tables",
    )(positions.reshape(T, 1), inv)


def _mla_prep_kernel(bp_ref, c_ref, s_ref, qn_ref, kvn_ref, wq_ref, wk_ref, wvt_ref, q_out, k_out, vt_out, *, scale):
    bp = bp_ref[...]
    cq = bp[:, :B_Q_LORA]
    ckv = bp[:, B_Q_LORA:B_Q_LORA + B_KV_LORA]
    kpe = bp[:, B_Q_LORA + B_KV_LORA:B_Q_LORA + B_KV_LORA + LANE]
    kpe_sw = bp[:, B_Q_LORA + B_KV_LORA + LANE:]
    cqn = cq * lax.rsqrt(jnp.mean(cq * cq, axis=-1, keepdims=True) + EPS) * qn_ref[...]
    ckvn = ckv * lax.rsqrt(jnp.mean(ckv * ckv, axis=-1, keepdims=True) + EPS) * kvn_ref[...]
    qa = jnp.dot(cqn.astype(BF16), wq_ref[...], preferred_element_type=F32)
    ckvb = ckvn.astype(BF16)
    kn = jnp.dot(ckvb, wk_ref[...], preferred_element_type=F32)
    vt = lax.dot_general(wvt_ref[...], ckvb, (((1,), (1,)), ((), ())), preferred_element_type=F32).astype(BF16)
    ones_rows = jnp.ones((B_VT_ROWS - B_VDIM, vt.shape[1]), BF16)
    for h in range(B_HEADS):
        vt_out[h * B_VT_ROWS:h * B_VT_ROWS + B_VDIM, :] = vt[h * B_VDIM:(h + 1) * B_VDIM, :]
        vt_out[h * B_VT_ROWS + B_VDIM:(h + 1) * B_VT_ROWS, :] = ones_rows
    cos = c_ref[...]
    sin = s_ref[...]
    k_rot = (kpe * cos + kpe_sw * sin).astype(BF16)
    sw0 = B_HEADS * B_QK_PAD
    for h in range(B_HEADS):
        c0 = h * B_QK_PAD
        q_out[:, c0:c0 + B_NOPE] = (qa[:, c0:c0 + B_NOPE] * scale).astype(BF16)
        pe = qa[:, c0 + B_NOPE:c0 + B_QK_PAD]
        pe_sw = qa[:, sw0 + h * LANE:sw0 + (h + 1) * LANE]
        q_out[:, c0 + B_NOPE:c0 + B_QK_PAD] = ((pe * cos + pe_sw * sin) * scale).astype(BF16)
        k_out[:, c0:c0 + B_NOPE] = kn[:, h * B_NOPE:(h + 1) * B_NOPE].astype(BF16)
        k_out[:, c0 + B_NOPE:c0 + B_QK_PAD] = k_rot


def _mla_prep(bproj, cos, sin, q_norm, kv_norm, wq, wk, wvt, tm=256):
    T = bproj.shape[0]
    scale = (B_NOPE + B_ROPE) ** -0.5 * LOG2_E
    qk_w = B_HEADS * B_QK_PAD
    v_w = B_HEADS * B_VT_ROWS
    row = lambda i: (i, 0)
    fixed = lambda i: (0, 0)
    return pl.pallas_call(
        functools.partial(_mla_prep_kernel, scale=scale),
        grid=(T // tm,),
        in_specs=[
            pl.BlockSpec((tm, bproj.shape[1]), row),
            pl.BlockSpec((tm, LANE), row),
            pl.BlockSpec((tm, LANE), row),
            pl.BlockSpec((1, B_Q_LORA), fixed),
            pl.BlockSpec((1, B_KV_LORA), fixed),
            pl.BlockSpec(wq.shape, fixed),
            pl.BlockSpec(wk.shape, fixed),
            pl.BlockSpec(wvt.shape, fixed),
        ],
        out_specs=[pl.BlockSpec((tm, qk_w), row), pl.BlockSpec((tm, qk_w), row),
                   pl.BlockSpec((v_w, tm), lambda i: (0, i))],
        out_shape=[jax.ShapeDtypeStruct((T, qk_w), BF16), jax.ShapeDtypeStruct((T, qk_w), BF16),
                   jax.ShapeDtypeStruct((v_w, T), BF16)],
        compiler_params=_params("parallel"),
        name="mla_prep",
    )(bproj, cos, sin, q_norm.reshape(1, -1), kv_norm.reshape(1, -1), wq, wk, wvt)


def _mla_attn_kernel(qt_ref, kt_ref, q_ref, k_ref, vt_ref, o_ref, m_sc, acc_sc, *, tq, tk):
    p = pl.program_id(2)
    qi = qt_ref[p]
    kj = kt_ref[p]

    @pl.when(kj == 0)
    def _():
        m_sc[...] = jnp.full_like(m_sc, NEG_INF)
        acc_sc[...] = jnp.zeros_like(acc_sc)

    def online_softmax_step(s):
        m_prev = m_sc[...]
        m_new = jnp.maximum(m_prev, jnp.max(s, axis=0, keepdims=True))
        alpha = jnp.exp2(m_prev - m_new)
        pr = jnp.exp2((s - m_new).astype(BF16))
        acc_sc[...] = alpha * acc_sc[...] + jnp.dot(vt_ref[...], pr, preferred_element_type=F32)
        m_sc[...] = m_new

    def scores():
        return lax.dot_general(k_ref[...], q_ref[...], (((1,), (1,)), ((), ())), preferred_element_type=F32)

    k_end = (kj + 1) * tk
    crosses_diagonal = k_end > qi * tq

    @pl.when(jnp.logical_not(crosses_diagonal))
    def _():
        online_softmax_step(scores())

    @pl.when(crosses_diagonal)
    def _():
        s = scores()
        kc = (kj * tk + lax.broadcasted_iota(I32, s.shape, 0)) // CHUNK
        qc = (qi * tq + lax.broadcasted_iota(I32, s.shape, 1)) // CHUNK
        online_softmax_step(jnp.where(kc <= qc, s, NEG_INF))

    @pl.when(k_end == (qi + 1) * tq)
    def _():
        row_sum = acc_sc[B_VDIM:B_VDIM + 1, :]
        o_ref[...] = (acc_sc[:B_VDIM, :] / row_sum).T.astype(o_ref.dtype)


def _mla_attention(qp, kp, vt, B, S):
    T = qp.shape[0]
    tq = min(B_TQ, S)
    tk = min(B_TK, S)
    nq = S // tq
    nk = S // tk
    pairs = [(qi, kj) for qi in range(nq) for kj in range((qi + 1) * tq // tk)]
    qt = jnp.asarray([p[0] for p in pairs], I32)
    kt = jnp.asarray([p[1] for p in pairs], I32)
    grid_spec = pltpu.PrefetchScalarGridSpec(
        num_scalar_prefetch=2,
        grid=(B, B_HEADS, len(pairs)),
        in_specs=[
            pl.BlockSpec((tq, B_QK_PAD), lambda b, h, p, qt, kt: (b * nq + qt[p], h)),
            pl.BlockSpec((tk, B_QK_PAD), lambda b, h, p, qt, kt: (b * nk + kt[p], h)),
            pl.BlockSpec((B_VT_ROWS, tk), lambda b, h, p, qt, kt: (h, b * nk + kt[p])),
        ],
        out_specs=pl.BlockSpec((tq, B_VDIM), lambda b, h, p, qt, kt: (b * nq + qt[p], h)),
        scratch_shapes=[pltpu.VMEM((1, tq), F32), pltpu.VMEM((B_VT_ROWS, tq), F32)],
    )
    return pl.pallas_call(
        functools.partial(_mla_attn_kernel, tq=tq, tk=tk),
        grid_spec=grid_spec,
        out_shape=jax.ShapeDtypeStruct((T, B_HEADS * B_VDIM), BF16),
        compiler_params=_params("parallel", "parallel", "arbitrary"),
        name="mla_attention",
    )(qt, kt, qp, kp, vt)


def _mixer_c_kernel(z_ref, lg_ref, lb_ref, ws_ref, bs_ref, o_ref):
    tm = z_ref.shape[0]
    u = z_ref[:, :C_WIDTH].astype(F32)
    v = z_ref[:, C_WIDTH:].astype(F32)
    mu = jnp.mean(v, axis=-1, keepdims=True)
    var = jnp.mean(jnp.square(v - mu), axis=-1, keepdims=True)
    vn = ((v - mu) * lax.rsqrt(var + EPS) * lg_ref[...] + lb_ref[...]).astype(BF16)
    row = lax.broadcasted_iota(I32, (C_BLOCK, C_BLOCK), 0)
    col = lax.broadcasted_iota(I32, (C_BLOCK, C_BLOCK), 1)
    causal = col <= row
    gd = C_WIDTH // C_GROUPS
    for g in range(C_GROUPS):
        w = jnp.where(causal, ws_ref[g], 0.0).astype(BF16)
        b = bs_ref[:, g:g + 1]
        for n in range(tm // C_BLOCK):
            rows = slice(n * C_BLOCK, (n + 1) * C_BLOCK)
            cols = slice(g * gd, (g + 1) * gd)
            mixed = jnp.dot(w, vn[rows, cols], preferred_element_type=F32) + b
            o_ref[rows, cols] = (u[rows, cols] * mixed).astype(o_ref.dtype)


def _mixer_c(z, ln_g, ln_b, w_s, b_s, tm=256):
    T = z.shape[0]
    return pl.pallas_call(
        _mixer_c_kernel,
        grid=(T // tm,),
        in_specs=[
            pl.BlockSpec((tm, 2 * C_WIDTH), lambda i: (i, 0)),
            pl.BlockSpec((1, C_WIDTH), lambda i: (0, 0)),
            pl.BlockSpec((1, C_WIDTH), lambda i: (0, 0)),
            pl.BlockSpec((C_GROUPS, C_BLOCK, C_BLOCK), lambda i: (0, 0, 0)),
            pl.BlockSpec((C_BLOCK, C_GROUPS), lambda i: (0, 0)),
        ],
        out_specs=pl.BlockSpec((tm, C_WIDTH), lambda i: (i, 0)),
        out_shape=jax.ShapeDtypeStruct((T, C_WIDTH), BF16),
        compiler_params=_params("parallel"),
        name="mixer_c",
    )(z, ln_g.reshape(1, -1), ln_b.reshape(1, -1), w_s, b_s.T)


def _branch_kernel(ya_ref, yb_ref, yc_ref, wb_ref, ga_ref, gb_ref, gc_ref, o_ref):
    acc = ga_ref[...].astype(F32) * jnp.dot(ya_ref[...], wb_ref[0], preferred_element_type=F32)
    acc += gb_ref[...].astype(F32) * jnp.dot(yb_ref[...], wb_ref[1], preferred_element_type=F32)
    acc += gc_ref[...].astype(F32) * jnp.dot(yc_ref[...], wb_ref[2], preferred_element_type=F32)
    o_ref[...] = acc.astype(o_ref.dtype)


def _branch_merge(ya, yb, yc, wb, gates, tm=1024, tn=512):
    T, W = ya.shape
    D = wb.shape[2]
    tm = min(tm, T)
    nj = D // tn
    y_spec = pl.BlockSpec((tm, W), lambda i, j: (i, 0))
    return pl.pallas_call(
        _branch_kernel,
        grid=(T // tm, nj),
        in_specs=[
            y_spec, y_spec, y_spec,
            pl.BlockSpec((3, W, tn), lambda i, j: (0, 0, j)),
            pl.BlockSpec((tm, tn), lambda i, j: (i, j)),
            pl.BlockSpec((tm, tn), lambda i, j: (i, nj + j)),
            pl.BlockSpec((tm, tn), lambda i, j: (i, 2 * nj + j)),
        ],
        out_specs=pl.BlockSpec((tm, tn), lambda i, j: (i, j)),
        out_shape=jax.ShapeDtypeStruct((T, D), BF16),
        compiler_params=_params("parallel", "arbitrary"),
        name="branch_merge",
    )(ya, yb, yc, wb, gates, gates, gates)


def _route_kernel(x_ref, g_ref, sc_ref, sh_ref, wrh_ref, wrl_ref, rb_ref,
                  hb_ref, ew_ref, lpos_ref, cnt_ref):
    tm = x_ref.shape[0]
    h = _modulated_norm(x_ref[...], g_ref[...], sc_ref[...], sh_ref[...])
    hb = h.astype(BF16)
    hb_ref[...] = hb
    h_lo = (h - hb.astype(F32)).astype(BF16)
    dn = (((1,), (1,)), ((), ()))
    logits = (lax.dot_general(wrh_ref[...], hb, dn, preferred_element_type=F32)
              + lax.dot_general(wrh_ref[...], h_lo, dn, preferred_element_type=F32)
              + lax.dot_general(wrl_ref[...], hb, dn, preferred_element_type=F32))
    scores = jax.nn.sigmoid(logits)
    sel = scores + rb_ref[...]
    neg = -jnp.inf
    n_groups = N_EXPERTS // EXPERT_GROUP

    i8 = lax.broadcasted_iota(I32, (EXPERT_GROUP, tm), 0)
    group_scores = []
    for g in range(n_groups):
        blk = sel[g * EXPERT_GROUP:(g + 1) * EXPERT_GROUP, :]
        m1 = jnp.max(blk, axis=0, keepdims=True)
        first = jnp.min(jnp.where(blk == m1, i8, EXPERT_GROUP), axis=0, keepdims=True)
        m2 = jnp.max(jnp.where(i8 == first, neg, blk), axis=0, keepdims=True)
        group_scores.append(m1 + m2)
    gs = jnp.concatenate(group_scores, axis=0)

    ig = lax.broadcasted_iota(I32, (n_groups, tm), 0)
    gsel = jnp.zeros((n_groups, tm), F32)
    for _ in range(TOPK_GROUPS):
        m = jnp.max(gs, axis=0, keepdims=True)
        first = jnp.min(jnp.where(gs == m, ig, n_groups), axis=0, keepdims=True)
        hit = ig == first
        gsel = jnp.where(hit, 1.0, gsel)
        gs = jnp.where(hit, neg, gs)

    masked = jnp.concatenate(
        [jnp.where(gsel[g:g + 1, :] > 0.0, sel[g * EXPERT_GROUP:(g + 1) * EXPERT_GROUP, :], neg)
         for g in range(n_groups)], axis=0)

    ie = lax.broadcasted_iota(I32, (N_EXPERTS, tm), 0)
    chosen = jnp.zeros((N_EXPERTS, tm), F32)
    hits, ws = [], []
    for _ in range(TOP_K):
        m = jnp.max(masked, axis=0, keepdims=True)
        first = jnp.min(jnp.where(masked == m, ie, N_EXPERTS), axis=0, keepdims=True)
        hit = ie == first
        ws.append(jnp.sum(jnp.where(hit, scores, 0.0), axis=0, keepdims=True))
        hits.append(hit)
        chosen = jnp.where(hit, 1.0, chosen)
        masked = jnp.where(hit, neg, masked)
    w = jnp.concatenate(ws, axis=0)
    denom = jnp.sum(w, axis=0, keepdims=True) + 1e-20
    ew_ref[...] = w / denom * ROUTED_SCALE

    s_idx = lax.broadcasted_iota(I32, (tm, tm), 0)
    t_idx = lax.broadcasted_iota(I32, (tm, tm), 1)
    earlier = jnp.where(s_idx < t_idx, 1.0, 0.0).astype(BF16)
    local_rank = jnp.dot(chosen.astype(BF16), earlier, preferred_element_type=F32)
    n_e = jnp.sum(chosen, axis=1, keepdims=True)
    run = jnp.floor((n_e + (ROW_ALIGN - 1.0)) * (1.0 / ROW_ALIGN)) * ROW_ALIGN
    run_b = jnp.broadcast_to(run, (N_EXPERTS, LANE))
    e_row = lax.broadcasted_iota(I32, (N_EXPERTS, N_EXPERTS), 0)
    e_col = lax.broadcasted_iota(I32, (N_EXPERTS, N_EXPERTS), 1)
    before = jnp.where(e_col < e_row, 1.0, 0.0).astype(BF16)
    run_start = jnp.dot(before, run_b.astype(BF16), preferred_element_type=F32)
    base = run_start[:, :1] + local_rank
    lpos = [jnp.sum(jnp.where(hit, base, 0.0), axis=0, keepdims=True) for hit in hits]
    lpos_ref[...] = jnp.concatenate(lpos, axis=0).astype(I32)
    cnt_ref[...] = run_b.astype(I32)


def _route(x, g, mod3, row_sc, row_sh, w_router, router_bias, S):
    T, D = x.shape
    tm = min(TOKEN_TILE, S)
    per = S // tm
    wr_t = w_router.T
    wr_hi = wr_t.astype(BF16)
    wr_lo = (wr_t - wr_hi.astype(F32)).astype(BF16)
    row = lambda i: (i, 0)
    col = lambda i: (0, i)
    fixed = lambda i: (0, 0)
    return pl.pallas_call(
        _route_kernel,
        grid=(T // tm,),
        in_specs=[
            pl.BlockSpec((tm, D), row),
            pl.BlockSpec((1, D), fixed),
            pl.BlockSpec((None, 1, D), lambda i: (row_sc(i // per), 0, 0)),
            pl.BlockSpec((None, 1, D), lambda i: (row_sh(i // per), 0, 0)),
            pl.BlockSpec((N_EXPERTS, D), fixed),
            pl.BlockSpec((N_EXPERTS, D), fixed),
            pl.BlockSpec((N_EXPERTS, 1), fixed),
        ],
        out_specs=[
            pl.BlockSpec((tm, D), row),
            pl.BlockSpec((TOP_K, tm), col),
            pl.BlockSpec((TOP_K, tm), col),
            pl.BlockSpec((N_EXPERTS, LANE), col),
        ],
        out_shape=[
            jax.ShapeDtypeStruct((T, D), BF16),
            jax.ShapeDtypeStruct((TOP_K, T), F32),
            jax.ShapeDtypeStruct((TOP_K, T), I32),
            jax.ShapeDtypeStruct((N_EXPERTS, (T // tm) * LANE), I32),
        ],
        compiler_params=_params("parallel"),
        name="norm_route",
    )(x, g.reshape(1, D), mod3, mod3, wr_hi, wr_lo, router_bias.reshape(N_EXPERTS, 1))


def _moe_tables(cnt, n_tiles):
    runs = cnt[:, ::LANE].T
    total = jnp.sum(runs, axis=0)
    region = (total + EXPERT_TILE - 1) // EXPERT_TILE * EXPERT_TILE
    region_end = jnp.cumsum(region)
    region_start = region_end - region
    dst = region_start[None, :] + jnp.cumsum(runs, axis=0) - runs
    src = jnp.cumsum(runs, axis=1) - runs
    tile_end = region_end // EXPERT_TILE
    tile_expert = jnp.sum(tile_end[None, :] <= jnp.arange(n_tiles, dtype=I32)[:, None], axis=1)
    tile_expert = jnp.minimum(tile_expert, N_EXPERTS - 1).astype(I32)
    as_i32 = lambda a: a.reshape(-1).astype(I32)
    gap_start = jnp.concatenate([region_start + total, region_end[-1:]])
    gap_rows = jnp.concatenate([region - total, n_tiles * EXPERT_TILE - region_end[-1:]])
    return dict(dst=as_i32(dst), src=as_i32(src), chunks=as_i32(runs // ROW_ALIGN),
                gap_start=as_i32(gap_start), gap_chunks=as_i32(gap_rows // ROW_ALIGN),
                tile_expert=tile_expert, n_used=as_i32(tile_end[-1:]))


def _run_copies(i, src_ref, dst_ref, chunks_ref, make_copy):
    for e in range(N_EXPERTS):
        s0 = src_ref[i * N_EXPERTS + e]
        d0 = dst_ref[i * N_EXPERTS + e]

        def body(c, carry, s0=s0, d0=d0):
            make_copy(pl.multiple_of(s0 + c * ROW_ALIGN, ROW_ALIGN),
                      pl.multiple_of(d0 + c * ROW_ALIGN, ROW_ALIGN)).start()
            return carry
        lax.fori_loop(0, chunks_ref[i * N_EXPERTS + e], body, 0)


def _tile_rows(i, src_ref, chunks_ref):
    last_run = i * N_EXPERTS + N_EXPERTS - 1
    return src_ref[last_run] + chunks_ref[last_run] * ROW_ALIGN


def _wait_rows(n_rows, src, dst, sem, max_rows):
    def wait_for(size):
        pltpu.make_async_copy(src.at[pl.ds(0, size)], dst.at[pl.ds(0, size)], sem).wait()

    if max_rows >= 2 * WAIT_ROWS:
        def body(c, carry):
            wait_for(WAIT_ROWS)
            return carry
        lax.fori_loop(0, n_rows // WAIT_ROWS, body, 0)
        n_rows = n_rows % WAIT_ROWS
        max_rows = WAIT_ROWS - ROW_ALIGN
    size = ROW_ALIGN
    while size <= max_rows:
        @pl.when((n_rows & size) != 0)
        def _(size=size):
            wait_for(size)
        size *= 2


def _dispatch_kernel(src_ref, dst_ref, chunks_ref, ts_ref, tc_ref, hb_ref, lp_ref, xs_hbm,
                     xloc, pbuf, zrows, sem):
    i = pl.program_id(0)
    n = pl.num_programs(0)
    tt = hb_ref.shape[0]
    slot = i % 2

    @pl.when(i == 0)
    def _():
        zrows[...] = jnp.zeros_like(zrows)
        gap_chunks = 0
        for e in range(N_EXPERTS + 1):
            def body(c, carry, e=e):
                pltpu.make_async_copy(
                    zrows, xs_hbm.at[pl.ds(pl.multiple_of(ts_ref[e] + c * ROW_ALIGN, ROW_ALIGN), ROW_ALIGN)],
                    sem.at[0]).start()
                return carry
            lax.fori_loop(0, tc_ref[e], body, 0)
            gap_chunks = gap_chunks + tc_ref[e]
        _wait_rows(gap_chunks * ROW_ALIGN, xs_hbm, xs_hbm, sem.at[0], xs_hbm.shape[0])

    lp = lp_ref[...]

    def onehot_block(rb, carry):
        r0 = pl.multiple_of(rb * PERM_BLOCK, PERM_BLOCK)
        rows = r0 + lax.broadcasted_iota(I32, (PERM_BLOCK, tt), 0)
        onehot = jnp.zeros((PERM_BLOCK, tt), F32)
        for k in range(TOP_K):
            onehot = jnp.where(rows == lp[k:k + 1, :], 1.0, onehot)
        pbuf[pl.ds(r0, PERM_BLOCK), :] = onehot.astype(BF16)
        return carry
    lax.fori_loop(0, pbuf.shape[0] // PERM_BLOCK, onehot_block, 0)

    perm = pbuf[...]
    for c in range(HALF_D // PERM_COLS):
        lo_cols = slice(c * PERM_COLS, (c + 1) * PERM_COLS)
        hi_cols = slice(HALF_D + c * PERM_COLS, HALF_D + (c + 1) * PERM_COLS)
        lo = jnp.dot(perm, hb_ref[:, lo_cols], preferred_element_type=F32)
        hi = jnp.dot(perm, hb_ref[:, hi_cols], preferred_element_type=F32)
        xloc[slot, :, lo_cols] = _pack_bf16_pair(lo, hi)

    def copies_from(buf_slot):
        def make_copy(s, d):
            return pltpu.make_async_copy(xloc.at[buf_slot, pl.ds(s, ROW_ALIGN)], xs_hbm.at[pl.ds(d, ROW_ALIGN)],
                                         sem.at[buf_slot])
        return make_copy
    _run_copies(i, src_ref, dst_ref, chunks_ref, copies_from(slot))
    local_rows = xloc.shape[1]

    @pl.when(i > 0)
    def _():
        _wait_rows(_tile_rows(i - 1, src_ref, chunks_ref), xloc.at[1 - slot], xs_hbm, sem.at[1 - slot], local_rows)

    @pl.when(i == n - 1)
    def _():
        _wait_rows(_tile_rows(i, src_ref, chunks_ref), xloc.at[slot], xs_hbm, sem.at[slot], local_rows)


def _local_rows(tt):
    rows = TOP_K * tt + N_EXPERTS * ROW_ALIGN
    return (rows + COMBINE_BLOCK - 1) // COMBINE_BLOCK * COMBINE_BLOCK


def _dispatch(hb, lpos, tables, n_rows, S):
    T, D = hb.shape
    tt = min(TOKEN_TILE, S)
    grid_spec = pltpu.PrefetchScalarGridSpec(
        num_scalar_prefetch=5,
        grid=(T // tt,),
        in_specs=[
            pl.BlockSpec((tt, D), lambda i, *_: (i, 0)),
            pl.BlockSpec((TOP_K, tt), lambda i, *_: (0, i)),
        ],
        out_specs=pl.BlockSpec(memory_space=pl.ANY),
        scratch_shapes=[pltpu.VMEM((2, _local_rows(tt), HALF_D), U32), pltpu.VMEM((_local_rows(tt), tt), BF16),
                        pltpu.VMEM((ROW_ALIGN, HALF_D), U32), pltpu.SemaphoreType.DMA((2,))],
    )
    return pl.pallas_call(
        _dispatch_kernel,
        grid_spec=grid_spec,
        out_shape=jax.ShapeDtypeStruct((n_rows, HALF_D), U32),
        compiler_params=_params("arbitrary"),
        name="moe_dispatch",
    )(tables["src"], tables["dst"], tables["chunks"], tables["gap_start"], tables["gap_chunks"], hb, lpos)


def _expert_kernel(te_ref, nu_ref, x_ref, wg_ref, wu_ref, wd_ref, y_ref, wgu_bf, wd_bf):
    i = pl.program_id(0)

    @pl.when(i < nu_ref[0])
    def _():
        @pl.when((i == 0) | (te_ref[i] != te_ref[jnp.maximum(i - 1, 0)]))
        def _():
            wgu_bf[:, :D_EXPERT] = wg_ref[...].astype(BF16)
            wgu_bf[:, D_EXPERT:] = wu_ref[...].astype(BF16)
            wd_bf[...] = wd_ref[...].astype(BF16)

        x_lo, x_hi = _unpack_halves(x_ref[...])
        gu = jnp.dot(x_lo.astype(BF16), wgu_bf[:HALF_D, :], preferred_element_type=F32)
        gu += jnp.dot(x_hi.astype(BF16), wgu_bf[HALF_D:, :], preferred_element_type=F32)
        gate = gu[:, :D_EXPERT]
        act = (gate * jax.nn.sigmoid(gate) * gu[:, D_EXPERT:]).astype(BF16)
        y = jnp.dot(act, wd_bf[...], preferred_element_type=F32)
        y_ref[...] = _pack_halves(y)

    @pl.when(i >= nu_ref[0])
    def _():
        y_ref[...] = jnp.zeros_like(y_ref)


def _experts(xs, tables, we_gate, we_up, we_down, layer):
    tm = EXPERT_TILE
    n_tiles = xs.shape[0] // tm
    used = lambda i, nu: jnp.minimum(i, nu[0] - 1)
    expert_block = lambda i, te, nu: (layer, te[used(i, nu)], 0, 0)
    row_block = lambda i, te, nu: (used(i, nu), 0)
    grid_spec = pltpu.PrefetchScalarGridSpec(
        num_scalar_prefetch=2,
        grid=(n_tiles,),
        in_specs=[
            pl.BlockSpec((tm, HALF_D), row_block),
            pl.BlockSpec((None, None, D_MODEL, D_EXPERT), expert_block),
            pl.BlockSpec((None, None, D_MODEL, D_EXPERT), expert_block),
            pl.BlockSpec((None, None, D_EXPERT, D_MODEL), expert_block),
        ],
        out_specs=pl.BlockSpec((tm, HALF_D), lambda i, te, nu: (i, 0)),
        scratch_shapes=[pltpu.VMEM((D_MODEL, 2 * D_EXPERT), BF16), pltpu.VMEM((D_EXPERT, D_MODEL), BF16)],
    )
    return pl.pallas_call(
        _expert_kernel,
        grid_spec=grid_spec,
        out_shape=jax.ShapeDtypeStruct(xs.shape, U32),
        compiler_params=_params("arbitrary"),
        name="experts",
    )(tables["tile_expert"], tables["n_used"], xs, we_gate, we_up, we_down)


def _shared_expert_kernel(h_ref, wgu_ref, wd_ref, o_ref):
    gu = jnp.dot(h_ref[...], wgu_ref[...], preferred_element_type=F32)
    gate = gu[:, :D_EXPERT]
    act = (gate * jax.nn.sigmoid(gate) * gu[:, D_EXPERT:]).astype(BF16)
    o_ref[...] = jnp.dot(act, wd_ref[...], preferred_element_type=F32).astype(o_ref.dtype)


def _shared_expert(hb, wsgu, wsd, tm=512):
    T, D = hb.shape
    return pl.pallas_call(
        _shared_expert_kernel,
        grid=(T // tm,),
        in_specs=[pl.BlockSpec((tm, D), lambda i: (i, 0)), pl.BlockSpec(wsgu.shape, lambda i: (0, 0)),
                  pl.BlockSpec(wsd.shape, lambda i: (0, 0))],
        out_specs=pl.BlockSpec((tm, D), lambda i: (i, 0)),
        out_shape=jax.ShapeDtypeStruct((T, D), BF16),
        compiler_params=_params("parallel"),
        name="shared_expert",
    )(hb, wsgu, wsd)


def _combine_kernel(src_ref, dst_ref, chunks_ref, lp_ref, ew_ref, x_ref, sh_ref, g2_ref, fg_ref, ys_hbm,
                    o_ref, yloc, acc, sem, *, final_norm):
    i = pl.program_id(0)
    n = pl.num_programs(0)
    tt = x_ref.shape[0]
    slot = i % 2

    def copies_into(buf_slot):
        def make_copy(s, d):
            return pltpu.make_async_copy(ys_hbm.at[pl.ds(d, ROW_ALIGN)], yloc.at[buf_slot, pl.ds(s, ROW_ALIGN)],
                                         sem.at[buf_slot])
        return make_copy

    @pl.when(i == 0)
    def _():
        yloc[...] = jnp.zeros_like(yloc)
        _run_copies(i, src_ref, dst_ref, chunks_ref, copies_into(0))

    tile_rows = _tile_rows(i, src_ref, chunks_ref)
    _wait_rows(tile_rows, ys_hbm, yloc.at[slot], sem.at[slot], yloc.shape[1])

    @pl.when(i + 1 < n)
    def _():
        _run_copies(i + 1, src_ref, dst_ref, chunks_ref, copies_into(1 - slot))

    lp = lp_ref[...]
    ew = ew_ref[...]
    acc[...] = jnp.zeros_like(acc)

    def sum_block(rb, carry):
        r0 = pl.multiple_of(rb * COMBINE_BLOCK, COMBINE_BLOCK)
        rows = r0 + lax.broadcasted_iota(I32, (COMBINE_BLOCK, tt), 0)
        wmat = jnp.zeros((COMBINE_BLOCK, tt), F32)
        for k in range(TOP_K):
            wmat = jnp.where(rows == lp[k:k + 1, :], ew[k:k + 1, :], wmat)
        w_row = jnp.sum(wmat, axis=1, keepdims=True)
        onehot = jnp.where(wmat != 0.0, 1.0, 0.0).astype(BF16)
        y_lo, y_hi = _unpack_halves(yloc[slot, pl.ds(r0, COMBINE_BLOCK), :])
        dn = (((0,), (0,)), ((), ()))
        acc[:HALF_D, :] += lax.dot_general((w_row * y_lo).astype(BF16), onehot, dn, preferred_element_type=F32)
        acc[HALF_D:, :] += lax.dot_general((w_row * y_hi).astype(BF16), onehot, dn, preferred_element_type=F32)
        return carry
    lax.fori_loop(0, pl.cdiv(tile_rows, COMBINE_BLOCK), sum_block, 0)

    out = x_ref[...] + g2_ref[...] * (sh_ref[...].astype(F32) + acc[...].T)
    if final_norm:
        out = out * lax.rsqrt(jnp.mean(out * out, axis=-1, keepdims=True) + EPS) * fg_ref[...]
    o_ref[...] = out


def _combine(x, shared, lpos, ew, mod3, row_g, tables, ys, final_g, final_norm, S):
    T, D = x.shape
    tt = min(TOKEN_TILE, S)
    per = S // tt
    row = lambda i, *_: (i, 0)
    col = lambda i, *_: (0, i)
    grid_spec = pltpu.PrefetchScalarGridSpec(
        num_scalar_prefetch=3,
        grid=(T // tt,),
        in_specs=[
            pl.BlockSpec((TOP_K, tt), col),
            pl.BlockSpec((TOP_K, tt), col),
            pl.BlockSpec((tt, D), row),
            pl.BlockSpec((tt, D), row),
            pl.BlockSpec((None, 1, D), lambda i, *_: (row_g(i // per), 0, 0)),
            pl.BlockSpec((1, D), lambda i, *_: (0, 0)),
            pl.BlockSpec(memory_space=pl.ANY),
        ],
        out_specs=pl.BlockSpec((tt, D), row),
        scratch_shapes=[pltpu.VMEM((2, _local_rows(tt), HALF_D), U32), pltpu.VMEM((D, tt), F32),
                        pltpu.SemaphoreType.DMA((2,))],
    )
    return pl.pallas_call(
        functools.partial(_combine_kernel, final_norm=final_norm),
        grid_spec=grid_spec,
        out_shape=jax.ShapeDtypeStruct((T, D), F32),
        compiler_params=_params("arbitrary"),
        name="moe_combine",
    )(tables["src"], tables["dst"], tables["chunks"], lpos, ew, x, shared, mod3, final_g.reshape(1, D), ys)


def _swap_halves(w):
    half = w.shape[-1] // 2
    return jnp.concatenate([w[..., half:], w[..., :half]], axis=-1)


def _split_w_in(w_in):
    a_cols = 3 * A_WIDTH
    b_cols = B_Q_LORA + B_KV_LORA + B_ROPE
    c_cols = 2 * C_WIDTH
    wa = w_in[:, :a_cols].astype(BF16)
    wb_raw = w_in[:, a_cols:a_cols + b_cols]
    kpe = wb_raw[:, B_Q_LORA + B_KV_LORA:]
    zeros = jnp.zeros_like(kpe)
    wb = jnp.concatenate([wb_raw[:, :B_Q_LORA + B_KV_LORA], kpe, zeros, _swap_halves(kpe), zeros],
                         axis=1).astype(BF16)
    wc = w_in[:, a_cols + b_cols:a_cols + b_cols + c_cols].astype(BF16)
    wg = w_in[:, a_cols + b_cols + c_cols:].astype(BF16)
    return wa, wb, wc, wg


def _mla_weights(w_uq, w_ukv):
    r = w_uq.shape[0]
    wq = w_uq.reshape(r, B_HEADS, B_NOPE + B_ROPE)
    nope, pe = wq[..., :B_NOPE], wq[..., B_NOPE:]
    zeros = jnp.zeros_like(pe)
    main = jnp.concatenate([nope, pe, zeros], axis=-1).reshape(r, B_HEADS * B_QK_PAD)
    swapped = jnp.concatenate([_swap_halves(pe), zeros], axis=-1).reshape(r, B_HEADS * LANE)
    wq_ext = jnp.concatenate([main, swapped], axis=1).astype(BF16)
    rk = w_ukv.shape[0]
    wkv = w_ukv.reshape(rk, B_HEADS, B_NOPE + B_VDIM)
    wk = wkv[..., :B_NOPE].reshape(rk, -1).astype(BF16)
    wvt = wkv[..., B_NOPE:].reshape(rk, -1).T.astype(BF16)
    return wq_ext, wk, wvt


def kernel(x, c, positions, rel_bias, norm_mix, norm_ffn, w_mod, b_mod, w_in, mla_q_norm, mla_w_uq, mla_kv_norm, mla_w_ukv, gmlp_ln_g, gmlp_ln_b, gmlp_w_s, gmlp_b_s, w_branch, w_out, w_router, router_bias, we_gate, we_up, we_down, ws_gate, ws_up, ws_down, final_norm):
    B, S, D = x.shape
    T = B * S
    L = w_mod.shape[0]
    xt = x.reshape(T, D)

    mod = _modulation(c, w_mod, b_mod)
    mod3 = mod.reshape(L * B * 6, 1, D)
    cos, sin = _rope_tables(positions)
    a_bias = _mixer_a_bias(rel_bias)
    n_runs = (T // min(TOKEN_TILE, S)) * N_EXPERTS
    n_tiles = pl.cdiv(T * TOP_K + n_runs * (ROW_ALIGN - 1), EXPERT_TILE) + N_EXPERTS

    for l in range(L):
        row = lambda k: (lambda b, l=l, k=k: (l * B + b) * 6 + k)
        wa, wb, wc, wg = _split_w_in(w_in[l])
        wq_ext, wk_nope, wv_t = _mla_weights(mla_w_uq[l], mla_w_ukv[l])

        h = _norm_mod(xt, norm_mix[l], mod3, row(1), row(0), S)
        qkv = _matmul(h, wa, BF16, name="proj_a")
        bproj = _matmul(h, wb, F32, name="proj_b")
        z = _matmul(h, wc, BF16, act="gelu", name="proj_c")
        gates = _matmul(h, wg, BF16, act="sigmoid", name="proj_gates")

        y_a = _mixer_a(qkv, a_bias, B, S)
        qp, kp, vt = _mla_prep(bproj, cos, sin, mla_q_norm[l], mla_kv_norm[l], wq_ext, wk_nope, wv_t)
        y_b = _mla_attention(qp, kp, vt, B, S)
        y_c = _mixer_c(z, gmlp_ln_g[l], gmlp_ln_b[l], gmlp_w_s[l], gmlp_b_s[l])

        merged = _branch_merge(y_a, y_b, y_c, w_branch[l].astype(BF16), gates)
        xt = _matmul_residual(merged, w_out[l].astype(BF16), xt, mod3, row(2), S)

        hb, ew, lpos, cnt = _route(xt, norm_ffn[l], mod3, row(4), row(3), w_router[l], router_bias[l], S)
        tables = _moe_tables(cnt, n_tiles)
        xs = _dispatch(hb, lpos, tables, n_tiles * EXPERT_TILE, S)
        ys = _experts(xs, tables, we_gate, we_up, we_down, l)
        wsgu = jnp.concatenate([ws_gate[l], ws_up[l]], axis=1).astype(BF16)
        shared = _shared_expert(hb, wsgu, ws_down[l].astype(BF16))
        xt = _combine(xt, shared, lpos, ew, mod3, row(5), tables, ys, final_norm, l == L - 1, S)

    return xt.reshape(B, S, D)
```

```python
import functools

import jax
import jax.numpy as jnp
from jax import lax
from jax.experimental import pallas as pl
from jax.experimental.pallas import tpu as pltpu

F32 = jnp.float32
BF16 = jnp.bfloat16
U32 = jnp.uint32
I32 = jnp.int32

EPS = 1e-6
NEG_INF = -1e30

D_MODEL = 2048
HALF_D = D_MODEL // 2
CHUNK = 64

A_HEADS = 16
A_HEAD_DIM = 64
A_WIDTH = A_HEADS * A_HEAD_DIM
A_LEFT_CHUNKS = 8
MAX_REL = 128
A_QBLOCK = 256
A_KBLOCKS = 3
A_PAIRS_PER_STEP = 4

B_HEADS = 8
B_NOPE = 128
B_ROPE = 64
B_VDIM = 128
B_Q_LORA = 512
B_KV_LORA = 256
B_QK_PAD = 256
B_VT_ROWS = B_VDIM + 16
LOG2_E = 1.4426950408889634
ROPE_THETA = 10000.0
B_TQ = 1024
B_TK = 1024
B_HEADS_PER_STEP = 2

C_BLOCK = 128
C_GROUPS = 8
C_WIDTH = 1024

N_EXPERTS = 64
EXPERT_GROUP = 8
TOP_K = 8
TOPK_GROUPS = 4
D_EXPERT = 512
ROUTED_SCALE = 2.5
EXPERT_TILE = 512
TOKEN_TILE = 256
ROW_ALIGN = 8
WAIT_ROWS = 2048
PERM_BLOCK = 256
PERM_COLS = 256
COMBINE_BLOCK = 512

LANE = 128
V7X_VMEM_LIMIT = 56 * 1024 * 1024


def _params(*sem):
    return pltpu.CompilerParams(dimension_semantics=sem, vmem_limit_bytes=V7X_VMEM_LIMIT)


def _pack_halves(v):
    w = v.shape[1] // 2
    r = v.astype(BF16).astype(F32)
    return _pack_bf16_pair(r[:, :w], r[:, w:])


def _pack_bf16_pair(lo, hi):
    return (lax.bitcast_convert_type(lo, U32) >> 16) | lax.bitcast_convert_type(hi, U32)


def _unpack_halves(p):
    lo = lax.bitcast_convert_type(p << 16, F32)
    hi = lax.bitcast_convert_type(p & jnp.uint32(0xFFFF0000), F32)
    return lo, hi


def _mod_kernel(c_ref, w_ref, b_ref, o_ref):
    c = c_ref[...]
    ca = (c * jax.nn.sigmoid(c)).astype(BF16)
    o_ref[...] = jnp.dot(ca, w_ref[...].astype(BF16), preferred_element_type=F32) + b_ref[...]


def _modulation(c, w_mod, b_mod):
    L, D, N = w_mod.shape
    B = c.shape[0]
    tn = 512
    return pl.pallas_call(
        _mod_kernel,
        grid=(L, N // tn),
        in_specs=[
            pl.BlockSpec((B, D), lambda l, j: (0, 0)),
            pl.BlockSpec((None, D, tn), lambda l, j: (l, 0, j)),
            pl.BlockSpec((None, 1, tn), lambda l, j: (l, 0, j)),
        ],
        out_specs=pl.BlockSpec((None, B, tn), lambda l, j: (l, 0, j)),
        out_shape=jax.ShapeDtypeStruct((L, B, N), F32),
        compiler_params=_params("parallel", "arbitrary"),
        name="modulation",
    )(c, w_mod, b_mod.reshape(L, 1, N))


def _modulated_norm(x, g, sc, sh):
    r = lax.rsqrt(jnp.mean(x * x, axis=-1, keepdims=True) + EPS)
    return (x * r * g) * (1.0 + sc) + sh


def _norm_mod_kernel(x_ref, g_ref, sc_ref, sh_ref, o_ref):
    o_ref[...] = _modulated_norm(x_ref[...], g_ref[...], sc_ref[...], sh_ref[...]).astype(o_ref.dtype)


def _norm_mod(x, g, mod3, row_sc, row_sh, S, tm=512):
    T, D = x.shape
    per = S // tm
    return pl.pallas_call(
        _norm_mod_kernel,
        grid=(T // tm,),
        in_specs=[
            pl.BlockSpec((tm, D), lambda i: (i, 0)),
            pl.BlockSpec((1, D), lambda i: (0, 0)),
            pl.BlockSpec((None, 1, D), lambda i: (row_sc(i // per), 0, 0)),
            pl.BlockSpec((None, 1, D), lambda i: (row_sh(i // per), 0, 0)),
        ],
        out_specs=pl.BlockSpec((tm, D), lambda i: (i, 0)),
        out_shape=jax.ShapeDtypeStruct((T, D), BF16),
        compiler_params=_params("parallel"),
        name="norm_mod",
    )(x, g.reshape(1, D), mod3, mod3)


def _mm_kernel(a_ref, w_ref, o_ref, *, act):
    acc = jnp.dot(a_ref[...], w_ref[...], preferred_element_type=F32)
    if act == "gelu":
        acc = jax.nn.gelu(acc)
    elif act == "sigmoid":
        acc = jax.nn.sigmoid(acc)
    o_ref[...] = acc.astype(o_ref.dtype)


def _matmul(a, w, out_dtype, act=None, tm=1024, tn=1024, name="matmul"):
    M, K = a.shape
    N = w.shape[1]
    tm = min(tm, M)
    return pl.pallas_call(
        functools.partial(_mm_kernel, act=act),
        grid=(M // tm, N // tn),
        in_specs=[
            pl.BlockSpec((tm, K), lambda i, j: (i, 0)),
            pl.BlockSpec((K, tn), lambda i, j: (0, j)),
        ],
        out_specs=pl.BlockSpec((tm, tn), lambda i, j: (i, j)),
        out_shape=jax.ShapeDtypeStruct((M, N), out_dtype),
        compiler_params=_params("parallel", "arbitrary"),
        name=name,
    )(a, w)


def _mm_residual_kernel(a_ref, w_ref, x_ref, g_ref, o_ref):
    acc = jnp.dot(a_ref[...], w_ref[...], preferred_element_type=F32)
    o_ref[...] = x_ref[...] + g_ref[...] * acc


def _matmul_residual(a, w, x, mod3, row_g, S, tm=1024, tn=1024):
    M, K = a.shape
    N = w.shape[1]
    tm = min(tm, S)
    per = S // tm
    return pl.pallas_call(
        _mm_residual_kernel,
        grid=(M // tm, N // tn),
        in_specs=[
            pl.BlockSpec((tm, K), lambda i, j: (i, 0)),
            pl.BlockSpec((K, tn), lambda i, j: (0, j)),
            pl.BlockSpec((tm, tn), lambda i, j: (i, j)),
            pl.BlockSpec((None, 1, tn), lambda i, j: (row_g(i // per), 0, j)),
        ],
        out_specs=pl.BlockSpec((tm, tn), lambda i, j: (i, j)),
        out_shape=jax.ShapeDtypeStruct((M, N), F32),
        compiler_params=_params("parallel", "arbitrary"),
        name="out_proj_residual",
    )(a, w, x, mod3)


def _mixer_a_kernel(q_ref, k0_ref, k1_ref, k2_ref, v0_ref, v1_ref, v2_ref, bias_ref, o_ref, *, scale):
    i = pl.program_id(2)
    qb = A_QBLOCK
    kw = A_KBLOCKS * qb
    lane = lax.broadcasted_iota(I32, (qb, LANE), 1)

    def attend(mask_before_sequence):
        ones_col = jnp.where(lax.broadcasted_iota(I32, (kw, LANE), 1) == 0, 1.0, 0.0).astype(BF16)
        for pp in range(A_PAIRS_PER_STEP):
            cols = slice(pp * LANE, (pp + 1) * LANE)
            q = q_ref[:, cols]
            k = jnp.concatenate([k0_ref[:, cols], k1_ref[:, cols], k2_ref[:, cols]], axis=0)
            v = jnp.concatenate([v0_ref[:, cols], v1_ref[:, cols], v2_ref[:, cols]], axis=0)
            v_ext = jnp.concatenate([v, ones_col], axis=1)
            outs = []
            for hh in range(2):
                head_lanes = (lane < A_HEAD_DIM) if hh == 0 else (lane >= A_HEAD_DIM)
                qm = jnp.where(head_lanes, q, jnp.zeros_like(q)) * jnp.asarray(scale, q.dtype)
                s = lax.dot_general(qm, k, (((1,), (1,)), ((), ())), preferred_element_type=F32)
                s = s + bias_ref[2 * pp + hh]
                if mask_before_sequence:
                    kpos = (i - (A_KBLOCKS - 1)) * qb + lax.broadcasted_iota(I32, (1, kw), 1)
                    s = jnp.where(kpos >= 0, s, NEG_INF)
                m = jnp.max(s, axis=-1, keepdims=True)
                p = jnp.exp((s - m).astype(BF16))
                o = jnp.dot(p, v_ext, preferred_element_type=F32)
                outs.append(o[:, :LANE] / o[:, LANE:LANE + 1])
            o_ref[:, cols] = jnp.where(lane < A_HEAD_DIM, outs[0], outs[1]).astype(o_ref.dtype)

    @pl.when(i < A_KBLOCKS - 1)
    def _():
        attend(True)

    @pl.when(i >= A_KBLOCKS - 1)
    def _():
        attend(False)


def _mixer_a_bias(rel_bias):
    qb = A_QBLOCK
    kw = A_KBLOCKS * qb
    qi = jnp.arange(qb)[:, None]
    kj = jnp.arange(kw)[None, :]
    qc = qi // CHUNK
    kc = kj // CHUNK
    first = (A_KBLOCKS - 1) * qb // CHUNK - A_LEFT_CHUNKS
    in_band = (kc >= qc + first) & (kc <= qc + first + A_LEFT_CHUNKS)
    period = kw + qb
    m = jnp.arange(period)
    k_minus_q = jnp.where(m < kw, m, m - period)
    d = (A_KBLOCKS - 1) * qb - k_minus_q
    line = rel_bias[:, jnp.clip(d, -MAX_REL, MAX_REL) + MAX_REL].astype(F32)
    heads = rel_bias.shape[0]
    flat = jnp.broadcast_to(line[:, None, :], (heads, qb, period)).reshape(heads, qb * period)
    bias = flat[:, :qb * (period - 1)].reshape(heads, qb, period - 1)[:, :, :kw]
    return jnp.where(in_band[None], bias, NEG_INF)


def _mixer_a(qkv, bias, B, S):
    T = qkv.shape[0]
    qb = A_QBLOCK
    nq = S // qb
    width = A_PAIRS_PER_STEP * LANE
    groups = A_WIDTH // width

    def kv_spec(j, col0):
        return pl.BlockSpec(
            (qb, width), lambda p, b, i: (b * nq + jnp.maximum(i - (A_KBLOCKS - 1) + j, 0), col0 + p))

    return pl.pallas_call(
        functools.partial(_mixer_a_kernel, scale=A_HEAD_DIM ** -0.5),
        grid=(groups, B, nq),
        in_specs=[
            pl.BlockSpec((qb, width), lambda p, b, i: (b * nq + i, p)),
            kv_spec(0, groups), kv_spec(1, groups), kv_spec(2, groups),
            kv_spec(0, 2 * groups), kv_spec(1, 2 * groups), kv_spec(2, 2 * groups),
            pl.BlockSpec((2 * A_PAIRS_PER_STEP, qb, A_KBLOCKS * qb), lambda p, b, i: (p, 0, 0)),
        ],
        out_specs=pl.BlockSpec((qb, width), lambda p, b, i: (b * nq + i, p)),
        out_shape=jax.ShapeDtypeStruct((T, A_WIDTH), BF16),
        compiler_params=_params("parallel", "parallel", "arbitrary"),
        name="mixer_a",
    )(qkv, qkv, qkv, qkv, qkv, qkv, qkv, bias)


def _rope_table_kernel(pos_ref, inv_ref, c_ref, s_ref):
    ang = pos_ref[...].astype(F32) * inv_ref[...]
    lane = lax.broadcasted_iota(I32, ang.shape, 1)
    first_half = (lane % B_ROPE) < (B_ROPE // 2)
    c_ref[...] = jnp.cos(ang)
    sn = jnp.sin(ang)
    s_ref[...] = jnp.where(first_half, -sn, sn)


def _rope_tables(positions, tm=512):
    T = positions.size
    half = B_ROPE // 2
    inv = ROPE_THETA ** (-jnp.arange(half, dtype=F32) / half)
    inv = jnp.tile(inv, LANE // half).reshape(1, LANE)
    return pl.pallas_call(
        _rope_table_kernel,
        grid=(T // tm,),
        in_specs=[pl.BlockSpec((tm, 1), lambda i: (i, 0)), pl.BlockSpec((1, LANE), lambda i: (0, 0))],
        out_specs=[pl.BlockSpec((tm, LANE), lambda i: (i, 0))] * 2,
        out_shape=[jax.ShapeDtypeStruct((T, LANE), F32)] * 2,
        compiler_params=_params("parallel"),
        name="rope_tables",
    )(positions.reshape(T, 1), inv)


def _mla_prep_kernel(bp_ref, c_ref, s_ref, qn_ref, kvn_ref, wq_ref, wk_ref, wvt_ref, q_out, k_out, vt_out, *, scale):
    bp = bp_ref[...]
    cq = bp[:, :B_Q_LORA]
    ckv = bp[:, B_Q_LORA:B_Q_LORA + B_KV_LORA]
    kpe = bp[:, B_Q_LORA + B_KV_LORA:B_Q_LORA + B_KV_LORA + LANE]
    kpe_sw = bp[:, B_Q_LORA + B_KV_LORA + LANE:]
    cqn = cq * lax.rsqrt(jnp.mean(cq * cq, axis=-1, keepdims=True) + EPS) * qn_ref[...]
    ckvn = ckv * lax.rsqrt(jnp.mean(ckv * ckv, axis=-1, keepdims=True) + EPS) * kvn_ref[...]
    qa = jnp.dot(cqn.astype(BF16), wq_ref[...], preferred_element_type=F32)
    ckvb = ckvn.astype(BF16)
    kn = jnp.dot(ckvb, wk_ref[...], preferred_element_type=F32)
    vt = lax.dot_general(wvt_ref[...], ckvb, (((1,), (1,)), ((), ())), preferred_element_type=F32).astype(BF16)
    ones_rows = jnp.ones((B_VT_ROWS - B_VDIM, vt.shape[1]), BF16)
    for h in range(B_HEADS):
        vt_out[h * B_VT_ROWS:h * B_VT_ROWS + B_VDIM, :] = vt[h * B_VDIM:(h + 1) * B_VDIM, :]
        vt_out[h * B_VT_ROWS + B_VDIM:(h + 1) * B_VT_ROWS, :] = ones_rows
    cos = c_ref[...]
    sin = s_ref[...]
    k_rot = (kpe * cos + kpe_sw * sin).astype(BF16)
    sw0 = B_HEADS * B_QK_PAD
    for h in range(B_HEADS):
        c0 = h * B_QK_PAD
        q_out[:, c0:c0 + B_NOPE] = (qa[:, c0:c0 + B_NOPE] * scale).astype(BF16)
        pe = qa[:, c0 + B_NOPE:c0 + B_QK_PAD]
        pe_sw = qa[:, sw0 + h * LANE:sw0 + (h + 1) * LANE]
        q_out[:, c0 + B_NOPE:c0 + B_QK_PAD] = ((pe * cos + pe_sw * sin) * scale).astype(BF16)
        k_out[:, c0:c0 + B_NOPE] = kn[:, h * B_NOPE:(h + 1) * B_NOPE].astype(BF16)
        k_out[:, c0 + B_NOPE:c0 + B_QK_PAD] = k_rot


def _mla_prep(bproj, cos, sin, q_norm, kv_norm, wq, wk, wvt, tm=256):
    T = bproj.shape[0]
    scale = (B_NOPE + B_ROPE) ** -0.5 * LOG2_E
    qk_w = B_HEADS * B_QK_PAD
    v_w = B_HEADS * B_VT_ROWS
    row = lambda i: (i, 0)
    fixed = lambda i: (0, 0)
    return pl.pallas_call(
        functools.partial(_mla_prep_kernel, scale=scale),
        grid=(T // tm,),
        in_specs=[
            pl.BlockSpec((tm, bproj.shape[1]), row),
            pl.BlockSpec((tm, LANE), row),
            pl.BlockSpec((tm, LANE), row),
            pl.BlockSpec((1, B_Q_LORA), fixed),
            pl.BlockSpec((1, B_KV_LORA), fixed),
            pl.BlockSpec(wq.shape, fixed),
            pl.BlockSpec(wk.shape, fixed),
            pl.BlockSpec(wvt.shape, fixed),
        ],
        out_specs=[pl.BlockSpec((tm, qk_w), row), pl.BlockSpec((tm, qk_w), row),
                   pl.BlockSpec((v_w, tm), lambda i: (0, i))],
        out_shape=[jax.ShapeDtypeStruct((T, qk_w), BF16), jax.ShapeDtypeStruct((T, qk_w), BF16),
                   jax.ShapeDtypeStruct((v_w, T), BF16)],
        compiler_params=_params("parallel"),
        name="mla_prep",
    )(bproj, cos, sin, q_norm.reshape(1, -1), kv_norm.reshape(1, -1), wq, wk, wvt)


def _mla_attn_kernel(qt_ref, kt_ref, q_ref, k_ref, vt_ref, o_ref, m_sc, acc_sc, *, tq, tk):
    p = pl.program_id(2)
    qi = qt_ref[p]
    kj = kt_ref[p]

    @pl.when(kj == 0)
    def _():
        m_sc[...] = jnp.full_like(m_sc, NEG_INF)
        acc_sc[...] = jnp.zeros_like(acc_sc)

    def online_softmax_step(masked):
        for g in range(B_HEADS_PER_STEP):
            qk_cols = slice(g * B_QK_PAD, (g + 1) * B_QK_PAD)
            v_rows = slice(g * B_VT_ROWS, (g + 1) * B_VT_ROWS)
            s = lax.dot_general(k_ref[:, qk_cols], q_ref[:, qk_cols], (((1,), (1,)), ((), ())),
                                preferred_element_type=F32)
            if masked:
                kc = (kj * tk + lax.broadcasted_iota(I32, s.shape, 0)) // CHUNK
                qc = (qi * tq + lax.broadcasted_iota(I32, s.shape, 1)) // CHUNK
                s = jnp.where(kc <= qc, s, NEG_INF)
            m_prev = m_sc[g:g + 1, :]
            m_new = jnp.maximum(m_prev, jnp.max(s, axis=0, keepdims=True))
            alpha = jnp.exp2(m_prev - m_new)
            pr = jnp.exp2((s - m_new).astype(BF16))
            acc_sc[v_rows, :] = alpha * acc_sc[v_rows, :] + jnp.dot(vt_ref[v_rows, :], pr,
                                                                    preferred_element_type=F32)
            m_sc[g:g + 1, :] = m_new

    k_end = (kj + 1) * tk
    crosses_diagonal = k_end > qi * tq

    @pl.when(jnp.logical_not(crosses_diagonal))
    def _():
        online_softmax_step(masked=False)

    @pl.when(crosses_diagonal)
    def _():
        online_softmax_step(masked=True)

    @pl.when(k_end == (qi + 1) * tq)
    def _():
        for g in range(B_HEADS_PER_STEP):
            r0 = g * B_VT_ROWS
            row_sum = acc_sc[r0 + B_VDIM:r0 + B_VDIM + 1, :]
            o_ref[:, g * B_VDIM:(g + 1) * B_VDIM] = (acc_sc[r0:r0 + B_VDIM, :] / row_sum).T.astype(o_ref.dtype)


def _mla_attention(qp, kp, vt, B, S):
    T = qp.shape[0]
    tq = min(B_TQ, S)
    tk = min(B_TK, S)
    nq = S // tq
    nk = S // tk
    hg = B_HEADS_PER_STEP
    pairs = [(qi, kj) for qi in range(nq) for kj in range((qi + 1) * tq // tk)]
    qt = jnp.asarray([p[0] for p in pairs], I32)
    kt = jnp.asarray([p[1] for p in pairs], I32)
    grid_spec = pltpu.PrefetchScalarGridSpec(
        num_scalar_prefetch=2,
        grid=(B, B_HEADS // hg, len(pairs)),
        in_specs=[
            pl.BlockSpec((tq, hg * B_QK_PAD), lambda b, h, p, qt, kt: (b * nq + qt[p], h)),
            pl.BlockSpec((tk, hg * B_QK_PAD), lambda b, h, p, qt, kt: (b * nk + kt[p], h)),
            pl.BlockSpec((hg * B_VT_ROWS, tk), lambda b, h, p, qt, kt: (h, b * nk + kt[p])),
        ],
        out_specs=pl.BlockSpec((tq, hg * B_VDIM), lambda b, h, p, qt, kt: (b * nq + qt[p], h)),
        scratch_shapes=[pltpu.VMEM((hg, tq), F32), pltpu.VMEM((hg * B_VT_ROWS, tq), F32)],
    )
    return pl.pallas_call(
        functools.partial(_mla_attn_kernel, tq=tq, tk=tk),
        grid_spec=grid_spec,
        out_shape=jax.ShapeDtypeStruct((T, B_HEADS * B_VDIM), BF16),
        compiler_params=_params("parallel", "parallel", "arbitrary"),
        name="mla_attention",
    )(qt, kt, qp, kp, vt)


def _mixer_c_kernel(z_ref, lg_ref, lb_ref, ws_ref, bs_ref, o_ref):
    tm = z_ref.shape[0]
    u = z_ref[:, :C_WIDTH].astype(F32)
    v = z_ref[:, C_WIDTH:].astype(F32)
    mu = jnp.mean(v, axis=-1, keepdims=True)
    var = jnp.mean(jnp.square(v - mu), axis=-1, keepdims=True)
    vn = ((v - mu) * lax.rsqrt(var + EPS) * lg_ref[...] + lb_ref[...]).astype(BF16)
    row = lax.broadcasted_iota(I32, (C_BLOCK, C_BLOCK), 0)
    col = lax.broadcasted_iota(I32, (C_BLOCK, C_BLOCK), 1)
    causal = col <= row
    gd = C_WIDTH // C_GROUPS
    for g in range(C_GROUPS):
        w = jnp.where(causal, ws_ref[g], 0.0).astype(BF16)
        b = bs_ref[:, g:g + 1]
        for n in range(tm // C_BLOCK):
            rows = slice(n * C_BLOCK, (n + 1) * C_BLOCK)
            cols = slice(g * gd, (g + 1) * gd)
            mixed = jnp.dot(w, vn[rows, cols], preferred_element_type=F32) + b
            o_ref[rows, cols] = (u[rows, cols] * mixed).astype(o_ref.dtype)


def _mixer_c(z, ln_g, ln_b, w_s, b_s, tm=256):
    T = z.shape[0]
    return pl.pallas_call(
        _mixer_c_kernel,
        grid=(T // tm,),
        in_specs=[
            pl.BlockSpec((tm, 2 * C_WIDTH), lambda i: (i, 0)),
            pl.BlockSpec((1, C_WIDTH), lambda i: (0, 0)),
            pl.BlockSpec((1, C_WIDTH), lambda i: (0, 0)),
            pl.BlockSpec((C_GROUPS, C_BLOCK, C_BLOCK), lambda i: (0, 0, 0)),
            pl.BlockSpec((C_BLOCK, C_GROUPS), lambda i: (0, 0)),
        ],
        out_specs=pl.BlockSpec((tm, C_WIDTH), lambda i: (i, 0)),
        out_shape=jax.ShapeDtypeStruct((T, C_WIDTH), BF16),
        compiler_params=_params("parallel"),
        name="mixer_c",
    )(z, ln_g.reshape(1, -1), ln_b.reshape(1, -1), w_s, b_s.T)


def _branch_kernel(ya_ref, yb_ref, yc_ref, wb_ref, ga_ref, gb_ref, gc_ref, o_ref):
    acc = ga_ref[...].astype(F32) * jnp.dot(ya_ref[...], wb_ref[0], preferred_element_type=F32)
    acc += gb_ref[...].astype(F32) * jnp.dot(yb_ref[...], wb_ref[1], preferred_element_type=F32)
    acc += gc_ref[...].astype(F32) * jnp.dot(yc_ref[...], wb_ref[2], preferred_element_type=F32)
    o_ref[...] = acc.astype(o_ref.dtype)


def _branch_merge(ya, yb, yc, wb, gates, tm=1024, tn=512):
    T, W = ya.shape
    D = wb.shape[2]
    tm = min(tm, T)
    nj = D // tn
    y_spec = pl.BlockSpec((tm, W), lambda i, j: (i, 0))
    return pl.pallas_call(
        _branch_kernel,
        grid=(T // tm, nj),
        in_specs=[
            y_spec, y_spec, y_spec,
            pl.BlockSpec((3, W, tn), lambda i, j: (0, 0, j)),
            pl.BlockSpec((tm, tn), lambda i, j: (i, j)),
            pl.BlockSpec((tm, tn), lambda i, j: (i, nj + j)),
            pl.BlockSpec((tm, tn), lambda i, j: (i, 2 * nj + j)),
        ],
        out_specs=pl.BlockSpec((tm, tn), lambda i, j: (i, j)),
        out_shape=jax.ShapeDtypeStruct((T, D), BF16),
        compiler_params=_params("parallel", "arbitrary"),
        name="branch_merge",
    )(ya, yb, yc, wb, gates, gates, gates)


def _route_kernel(x_ref, g_ref, sc_ref, sh_ref, wrh_ref, wrl_ref, rb_ref,
                  hb_ref, ew_ref, lpos_ref, cnt_ref):
    tm = x_ref.shape[0]
    h = _modulated_norm(x_ref[...], g_ref[...], sc_ref[...], sh_ref[...])
    hb = h.astype(BF16)
    hb_ref[...] = hb
    h_lo = (h - hb.astype(F32)).astype(BF16)
    dn = (((1,), (1,)), ((), ()))
    logits = (lax.dot_general(wrh_ref[...], hb, dn, preferred_element_type=F32)
              + lax.dot_general(wrh_ref[...], h_lo, dn, preferred_element_type=F32)
              + lax.dot_general(wrl_ref[...], hb, dn, preferred_element_type=F32))
    scores = jax.nn.sigmoid(logits)
    sel = scores + rb_ref[...]
    neg = -jnp.inf
    n_groups = N_EXPERTS // EXPERT_GROUP

    i8 = lax.broadcasted_iota(I32, (EXPERT_GROUP, tm), 0)
    group_scores = []
    for g in range(n_groups):
        blk = sel[g * EXPERT_GROUP:(g + 1) * EXPERT_GROUP, :]
        m1 = jnp.max(blk, axis=0, keepdims=True)
        first = jnp.min(jnp.where(blk == m1, i8, EXPERT_GROUP), axis=0, keepdims=True)
        m2 = jnp.max(jnp.where(i8 == first, neg, blk), axis=0, keepdims=True)
        group_scores.append(m1 + m2)
    gs = jnp.concatenate(group_scores, axis=0)

    ig = lax.broadcasted_iota(I32, (n_groups, tm), 0)
    gsel = jnp.zeros((n_groups, tm), F32)
    for _ in range(TOPK_GROUPS):
        m = jnp.max(gs, axis=0, keepdims=True)
        first = jnp.min(jnp.where(gs == m, ig, n_groups), axis=0, keepdims=True)
        hit = ig == first
        gsel = jnp.where(hit, 1.0, gsel)
        gs = jnp.where(hit, neg, gs)

    masked = jnp.concatenate(
        [jnp.where(gsel[g:g + 1, :] > 0.0, sel[g * EXPERT_GROUP:(g + 1) * EXPERT_GROUP, :], neg)
         for g in range(n_groups)], axis=0)

    ie = lax.broadcasted_iota(I32, (N_EXPERTS, tm), 0)
    chosen = jnp.zeros((N_EXPERTS, tm), F32)
    hits, ws = [], []
    for _ in range(TOP_K):
        m = jnp.max(masked, axis=0, keepdims=True)
        first = jnp.min(jnp.where(masked == m, ie, N_EXPERTS), axis=0, keepdims=True)
        hit = ie == first
        ws.append(jnp.sum(jnp.where(hit, scores, 0.0), axis=0, keepdims=True))
        hits.append(hit)
        chosen = jnp.where(hit, 1.0, chosen)
        masked = jnp.where(hit, neg, masked)
    w = jnp.concatenate(ws, axis=0)
    denom = jnp.sum(w, axis=0, keepdims=True) + 1e-20
    ew_ref[...] = w / denom * ROUTED_SCALE

    s_idx = lax.broadcasted_iota(I32, (tm, tm), 0)
    t_idx = lax.broadcasted_iota(I32, (tm, tm), 1)
    earlier = jnp.where(s_idx < t_idx, 1.0, 0.0).astype(BF16)
    local_rank = jnp.dot(chosen.astype(BF16), earlier, preferred_element_type=F32)
    n_e = jnp.sum(chosen, axis=1, keepdims=True)
    run = jnp.floor((n_e + (ROW_ALIGN - 1.0)) * (1.0 / ROW_ALIGN)) * ROW_ALIGN
    run_b = jnp.broadcast_to(run, (N_EXPERTS, LANE))
    e_row = lax.broadcasted_iota(I32, (N_EXPERTS, N_EXPERTS), 0)
    e_col = lax.broadcasted_iota(I32, (N_EXPERTS, N_EXPERTS), 1)
    before = jnp.where(e_col < e_row, 1.0, 0.0).astype(BF16)
    run_start = jnp.dot(before, run_b.astype(BF16), preferred_element_type=F32)
    base = run_start[:, :1] + local_rank
    lpos = [jnp.sum(jnp.where(hit, base, 0.0), axis=0, keepdims=True) for hit in hits]
    lpos_ref[...] = jnp.concatenate(lpos, axis=0).astype(I32)
    cnt_ref[...] = run_b.astype(I32)


def _route(x, g, mod3, row_sc, row_sh, w_router, router_bias, S):
    T, D = x.shape
    tm = min(TOKEN_TILE, S)
    per = S // tm
    wr_t = w_router.T
    wr_hi = wr_t.astype(BF16)
    wr_lo = (wr_t - wr_hi.astype(F32)).astype(BF16)
    row = lambda i: (i, 0)
    col = lambda i: (0, i)
    fixed = lambda i: (0, 0)
    return pl.pallas_call(
        _route_kernel,
        grid=(T // tm,),
        in_specs=[
            pl.BlockSpec((tm, D), row),
            pl.BlockSpec((1, D), fixed),
            pl.BlockSpec((None, 1, D), lambda i: (row_sc(i // per), 0, 0)),
            pl.BlockSpec((None, 1, D), lambda i: (row_sh(i // per), 0, 0)),
            pl.BlockSpec((N_EXPERTS, D), fixed),
            pl.BlockSpec((N_EXPERTS, D), fixed),
            pl.BlockSpec((N_EXPERTS, 1), fixed),
        ],
        out_specs=[
            pl.BlockSpec((tm, D), row),
            pl.BlockSpec((TOP_K, tm), col),
            pl.BlockSpec((TOP_K, tm), col),
            pl.BlockSpec((N_EXPERTS, LANE), col),
        ],
        out_shape=[
            jax.ShapeDtypeStruct((T, D), BF16),
            jax.ShapeDtypeStruct((TOP_K, T), F32),
            jax.ShapeDtypeStruct((TOP_K, T), I32),
            jax.ShapeDtypeStruct((N_EXPERTS, (T // tm) * LANE), I32),
        ],
        compiler_params=_params("parallel"),
        name="norm_route",
    )(x, g.reshape(1, D), mod3, mod3, wr_hi, wr_lo, router_bias.reshape(N_EXPERTS, 1))


def _moe_tables(cnt, n_tiles):
    runs = cnt[:, ::LANE].T
    total = jnp.sum(runs, axis=0)
    region = (total + EXPERT_TILE - 1) // EXPERT_TILE * EXPERT_TILE
    region_end = jnp.cumsum(region)
    region_start = region_end - region
    dst = region_start[None, :] + jnp.cumsum(runs, axis=0) - runs
    src = jnp.cumsum(runs, axis=1) - runs
    tile_end = region_end // EXPERT_TILE
    tile_expert = jnp.sum(tile_end[None, :] <= jnp.arange(n_tiles, dtype=I32)[:, None], axis=1)
    tile_expert = jnp.minimum(tile_expert, N_EXPERTS - 1).astype(I32)
    as_i32 = lambda a: a.reshape(-1).astype(I32)
    gap_start = jnp.concatenate([region_start + total, region_end[-1:]])
    gap_rows = jnp.concatenate([region - total, n_tiles * EXPERT_TILE - region_end[-1:]])
    return dict(dst=as_i32(dst), src=as_i32(src), chunks=as_i32(runs // ROW_ALIGN),
                gap_start=as_i32(gap_start), gap_chunks=as_i32(gap_rows // ROW_ALIGN),
                tile_expert=tile_expert, n_used=as_i32(tile_end[-1:]))


def _run_copies(i, src_ref, dst_ref, chunks_ref, make_copy):
    for e in range(N_EXPERTS):
        s0 = src_ref[i * N_EXPERTS + e]
        d0 = dst_ref[i * N_EXPERTS + e]

        def body(c, carry, s0=s0, d0=d0):
            make_copy(pl.multiple_of(s0 + c * ROW_ALIGN, ROW_ALIGN),
                      pl.multiple_of(d0 + c * ROW_ALIGN, ROW_ALIGN)).start()
            return carry
        lax.fori_loop(0, chunks_ref[i * N_EXPERTS + e], body, 0)


def _tile_rows(i, src_ref, chunks_ref):
    last_run = i * N_EXPERTS + N_EXPERTS - 1
    return src_ref[last_run] + chunks_ref[last_run] * ROW_ALIGN


def _wait_rows(n_rows, src, dst, sem, max_rows):
    def wait_for(size):
        pltpu.make_async_copy(src.at[pl.ds(0, size)], dst.at[pl.ds(0, size)], sem).wait()

    if max_rows >= 2 * WAIT_ROWS:
        def body(c, carry):
            wait_for(WAIT_ROWS)
            return carry
        lax.fori_loop(0, n_rows // WAIT_ROWS, body, 0)
        n_rows = n_rows % WAIT_ROWS
        max_rows = WAIT_ROWS - ROW_ALIGN
    size = ROW_ALIGN
    while size <= max_rows:
        @pl.when((n_rows & size) != 0)
        def _(size=size):
            wait_for(size)
        size *= 2


def _dispatch_kernel(src_ref, dst_ref, chunks_ref, ts_ref, tc_ref, hb_ref, lp_ref, xs_hbm,
                     xloc, pbuf, zrows, sem):
    i = pl.program_id(0)
    n = pl.num_programs(0)
    tt = hb_ref.shape[0]
    slot = i % 2

    @pl.when(i == 0)
    def _():
        zrows[...] = jnp.zeros_like(zrows)
        gap_chunks = 0
        for e in range(N_EXPERTS + 1):
            def body(c, carry, e=e):
                pltpu.make_async_copy(
                    zrows, xs_hbm.at[pl.ds(pl.multiple_of(ts_ref[e] + c * ROW_ALIGN, ROW_ALIGN), ROW_ALIGN)],
                    sem.at[0]).start()
                return carry
            lax.fori_loop(0, tc_ref[e], body, 0)
            gap_chunks = gap_chunks + tc_ref[e]
        _wait_rows(gap_chunks * ROW_ALIGN, xs_hbm, xs_hbm, sem.at[0], xs_hbm.shape[0])

    lp = lp_ref[...]

    def onehot_block(rb, carry):
        r0 = pl.multiple_of(rb * PERM_BLOCK, PERM_BLOCK)
        rows = r0 + lax.broadcasted_iota(I32, (PERM_BLOCK, tt), 0)
        onehot = jnp.zeros((PERM_BLOCK, tt), F32)
        for k in range(TOP_K):
            onehot = jnp.where(rows == lp[k:k + 1, :], 1.0, onehot)
        pbuf[pl.ds(r0, PERM_BLOCK), :] = onehot.astype(BF16)
        return carry
    lax.fori_loop(0, pbuf.shape[0] // PERM_BLOCK, onehot_block, 0)

    perm = pbuf[...]
    for c in range(HALF_D // PERM_COLS):
        lo_cols = slice(c * PERM_COLS, (c + 1) * PERM_COLS)
        hi_cols = slice(HALF_D + c * PERM_COLS, HALF_D + (c + 1) * PERM_COLS)
        lo = jnp.dot(perm, hb_ref[:, lo_cols], preferred_element_type=F32)
        hi = jnp.dot(perm, hb_ref[:, hi_cols], preferred_element_type=F32)
        xloc[slot, :, lo_cols] = _pack_bf16_pair(lo, hi)

    def copies_from(buf_slot):
        def make_copy(s, d):
            return pltpu.make_async_copy(xloc.at[buf_slot, pl.ds(s, ROW_ALIGN)], xs_hbm.at[pl.ds(d, ROW_ALIGN)],
                                         sem.at[buf_slot])
        return make_copy
    _run_copies(i, src_ref, dst_ref, chunks_ref, copies_from(slot))
    local_rows = xloc.shape[1]

    @pl.when(i > 0)
    def _():
        _wait_rows(_tile_rows(i - 1, src_ref, chunks_ref), xloc.at[1 - slot], xs_hbm, sem.at[1 - slot], local_rows)

    @pl.when(i == n - 1)
    def _():
        _wait_rows(_tile_rows(i, src_ref, chunks_ref), xloc.at[slot], xs_hbm, sem.at[slot], local_rows)


def _local_rows(tt):
    rows = TOP_K * tt + N_EXPERTS * ROW_ALIGN
    return (rows + COMBINE_BLOCK - 1) // COMBINE_BLOCK * COMBINE_BLOCK


def _dispatch(hb, lpos, tables, n_rows, S):
    T, D = hb.shape
    tt = min(TOKEN_TILE, S)
    grid_spec = pltpu.PrefetchScalarGridSpec(
        num_scalar_prefetch=5,
        grid=(T // tt,),
        in_specs=[
            pl.BlockSpec((tt, D), lambda i, *_: (i, 0)),
            pl.BlockSpec((TOP_K, tt), lambda i, *_: (0, i)),
        ],
        out_specs=pl.BlockSpec(memory_space=pl.ANY),
        scratch_shapes=[pltpu.VMEM((2, _local_rows(tt), HALF_D), U32), pltpu.VMEM((_local_rows(tt), tt), BF16),
                        pltpu.VMEM((ROW_ALIGN, HALF_D), U32), pltpu.SemaphoreType.DMA((2,))],
    )
    return pl.pallas_call(
        _dispatch_kernel,
        grid_spec=grid_spec,
        out_shape=jax.ShapeDtypeStruct((n_rows, HALF_D), U32),
        compiler_params=_params("arbitrary"),
        name="moe_dispatch",
    )(tables["src"], tables["dst"], tables["chunks"], tables["gap_start"], tables["gap_chunks"], hb, lpos)


def _expert_kernel(te_ref, nu_ref, x_ref, wg_ref, wu_ref, wd_ref, y_ref, wgu_bf, wd_bf):
    i = pl.program_id(0)

    @pl.when(i < nu_ref[0])
    def _():
        @pl.when((i == 0) | (te_ref[i] != te_ref[jnp.maximum(i - 1, 0)]))
        def _():
            wgu_bf[:, :D_EXPERT] = wg_ref[...].astype(BF16)
            wgu_bf[:, D_EXPERT:] = wu_ref[...].astype(BF16)
            wd_bf[...] = wd_ref[...].astype(BF16)

        x_lo, x_hi = _unpack_halves(x_ref[...])
        gu = jnp.dot(x_lo.astype(BF16), wgu_bf[:HALF_D, :], preferred_element_type=F32)
        gu += jnp.dot(x_hi.astype(BF16), wgu_bf[HALF_D:, :], preferred_element_type=F32)
        gate = gu[:, :D_EXPERT]
        act = (gate * jax.nn.sigmoid(gate) * gu[:, D_EXPERT:]).astype(BF16)
        y = jnp.dot(act, wd_bf[...], preferred_element_type=F32)
        y_ref[...] = _pack_halves(y)

    @pl.when(i >= nu_ref[0])
    def _():
        y_ref[...] = jnp.zeros_like(y_ref)


def _experts(xs, tables, we_gate, we_up, we_down, layer):
    tm = EXPERT_TILE
    n_tiles = xs.shape[0] // tm
    used = lambda i, nu: jnp.minimum(i, nu[0] - 1)
    expert_block = lambda i, te, nu: (layer, te[used(i, nu)], 0, 0)
    row_block = lambda i, te, nu: (used(i, nu), 0)
    grid_spec = pltpu.PrefetchScalarGridSpec(
        num_scalar_prefetch=2,
        grid=(n_tiles,),
        in_specs=[
            pl.BlockSpec((tm, HALF_D), row_block),
            pl.BlockSpec((None, None, D_MODEL, D_EXPERT), expert_block),
            pl.BlockSpec((None, None, D_MODEL, D_EXPERT), expert_block),
            pl.BlockSpec((None, None, D_EXPERT, D_MODEL), expert_block),
        ],
        out_specs=pl.BlockSpec((tm, HALF_D), lambda i, te, nu: (i, 0)),
        scratch_shapes=[pltpu.VMEM((D_MODEL, 2 * D_EXPERT), BF16), pltpu.VMEM((D_EXPERT, D_MODEL), BF16)],
    )
    return pl.pallas_call(
        _expert_kernel,
        grid_spec=grid_spec,
        out_shape=jax.ShapeDtypeStruct(xs.shape, U32),
        compiler_params=_params("arbitrary"),
        name="experts",
    )(tables["tile_expert"], tables["n_used"], xs, we_gate, we_up, we_down)


def _shared_expert_kernel(h_ref, wgu_ref, wd_ref, o_ref):
    gu = jnp.dot(h_ref[...], wgu_ref[...], preferred_element_type=F32)
    gate = gu[:, :D_EXPERT]
    act = (gate * jax.nn.sigmoid(gate) * gu[:, D_EXPERT:]).astype(BF16)
    o_ref[...] = jnp.dot(act, wd_ref[...], preferred_element_type=F32).astype(o_ref.dtype)


def _shared_expert(hb, wsgu, wsd, tm=512):
    T, D = hb.shape
    return pl.pallas_call(
        _shared_expert_kernel,
        grid=(T // tm,),
        in_specs=[pl.BlockSpec((tm, D), lambda i: (i, 0)), pl.BlockSpec(wsgu.shape, lambda i: (0, 0)),
                  pl.BlockSpec(wsd.shape, lambda i: (0, 0))],
        out_specs=pl.BlockSpec((tm, D), lambda i: (i, 0)),
        out_shape=jax.ShapeDtypeStruct((T, D), BF16),
        compiler_params=_params("parallel"),
        name="shared_expert",
    )(hb, wsgu, wsd)


def _combine_kernel(src_ref, dst_ref, chunks_ref, lp_ref, ew_ref, x_ref, sh_ref, g2_ref, fg_ref, ys_hbm,
                    o_ref, yloc, acc, sem, *, final_norm):
    i = pl.program_id(0)
    n = pl.num_programs(0)
    tt = x_ref.shape[0]
    slot = i % 2

    def copies_into(buf_slot):
        def make_copy(s, d):
            return pltpu.make_async_copy(ys_hbm.at[pl.ds(d, ROW_ALIGN)], yloc.at[buf_slot, pl.ds(s, ROW_ALIGN)],
                                         sem.at[buf_slot])
        return make_copy

    @pl.when(i == 0)
    def _():
        yloc[...] = jnp.zeros_like(yloc)
        _run_copies(i, src_ref, dst_ref, chunks_ref, copies_into(0))

    tile_rows = _tile_rows(i, src_ref, chunks_ref)
    _wait_rows(tile_rows, ys_hbm, yloc.at[slot], sem.at[slot], yloc.shape[1])

    @pl.when(i + 1 < n)
    def _():
        _run_copies(i + 1, src_ref, dst_ref, chunks_ref, copies_into(1 - slot))

    lp = lp_ref[...]
    ew = ew_ref[...]
    acc[...] = jnp.zeros_like(acc)

    def sum_block(rb, carry):
        r0 = pl.multiple_of(rb * COMBINE_BLOCK, COMBINE_BLOCK)
        rows = r0 + lax.broadcasted_iota(I32, (COMBINE_BLOCK, tt), 0)
        wmat = jnp.zeros((COMBINE_BLOCK, tt), F32)
        for k in range(TOP_K):
            wmat = jnp.where(rows == lp[k:k + 1, :], ew[k:k + 1, :], wmat)
        w_row = jnp.sum(wmat, axis=1, keepdims=True)
        onehot = jnp.where(wmat != 0.0, 1.0, 0.0).astype(BF16)
        y_lo, y_hi = _unpack_halves(yloc[slot, pl.ds(r0, COMBINE_BLOCK), :])
        dn = (((0,), (0,)), ((), ()))
        acc[:HALF_D, :] += lax.dot_general((w_row * y_lo).astype(BF16), onehot, dn, preferred_element_type=F32)
        acc[HALF_D:, :] += lax.dot_general((w_row * y_hi).astype(BF16), onehot, dn, preferred_element_type=F32)
        return carry
    lax.fori_loop(0, pl.cdiv(tile_rows, COMBINE_BLOCK), sum_block, 0)

    out = x_ref[...] + g2_ref[...] * (sh_ref[...].astype(F32) + acc[...].T)
    if final_norm:
        out = out * lax.rsqrt(jnp.mean(out * out, axis=-1, keepdims=True) + EPS) * fg_ref[...]
    o_ref[...] = out


def _combine(x, shared, lpos, ew, mod3, row_g, tables, ys, final_g, final_norm, S):
    T, D = x.shape
    tt = min(TOKEN_TILE, S)
    per = S // tt
    row = lambda i, *_: (i, 0)
    col = lambda i, *_: (0, i)
    grid_spec = pltpu.PrefetchScalarGridSpec(
        num_scalar_prefetch=3,
        grid=(T // tt,),
        in_specs=[
            pl.BlockSpec((TOP_K, tt), col),
            pl.BlockSpec((TOP_K, tt), col),
            pl.BlockSpec((tt, D), row),
            pl.BlockSpec((tt, D), row),
            pl.BlockSpec((None, 1, D), lambda i, *_: (row_g(i // per), 0, 0)),
            pl.BlockSpec((1, D), lambda i, *_: (0, 0)),
            pl.BlockSpec(memory_space=pl.ANY),
        ],
        out_specs=pl.BlockSpec((tt, D), row),
        scratch_shapes=[pltpu.VMEM((2, _local_rows(tt), HALF_D), U32), pltpu.VMEM((D, tt), F32),
                        pltpu.SemaphoreType.DMA((2,))],
    )
    return pl.pallas_call(
        functools.partial(_combine_kernel, final_norm=final_norm),
        grid_spec=grid_spec,
        out_shape=jax.ShapeDtypeStruct((T, D), F32),
        compiler_params=_params("arbitrary"),
        name="moe_combine",
    )(tables["src"], tables["dst"], tables["chunks"], lpos, ew, x, shared, mod3, final_g.reshape(1, D), ys)


def _swap_halves(w):
    half = w.shape[-1] // 2
    return jnp.concatenate([w[..., half:], w[..., :half]], axis=-1)


def _split_w_in(w_in):
    a_cols = 3 * A_WIDTH
    b_cols = B_Q_LORA + B_KV_LORA + B_ROPE
    c_cols = 2 * C_WIDTH
    wa = w_in[:, :a_cols].astype(BF16)
    wb_raw = w_in[:, a_cols:a_cols + b_cols]
    kpe = wb_raw[:, B_Q_LORA + B_KV_LORA:]
    zeros = jnp.zeros_like(kpe)
    wb = jnp.concatenate([wb_raw[:, :B_Q_LORA + B_KV_LORA], kpe, zeros, _swap_halves(kpe), zeros],
                         axis=1).astype(BF16)
    wc = w_in[:, a_cols + b_cols:a_cols + b_cols + c_cols].astype(BF16)
    wg = w_in[:, a_cols + b_cols + c_cols:].astype(BF16)
    return wa, wb, wc, wg


def _mla_weights(w_uq, w_ukv):
    r = w_uq.shape[0]
    wq = w_uq.reshape(r, B_HEADS, B_NOPE + B_ROPE)
    nope, pe = wq[..., :B_NOPE], wq[..., B_NOPE:]
    zeros = jnp.zeros_like(pe)
    main = jnp.concatenate([nope, pe, zeros], axis=-1).reshape(r, B_HEADS * B_QK_PAD)
    swapped = jnp.concatenate([_swap_halves(pe), zeros], axis=-1).reshape(r, B_HEADS * LANE)
    wq_ext = jnp.concatenate([main, swapped], axis=1).astype(BF16)
    rk = w_ukv.shape[0]
    wkv = w_ukv.reshape(rk, B_HEADS, B_NOPE + B_VDIM)
    wk = wkv[..., :B_NOPE].reshape(rk, -1).astype(BF16)
    wvt = wkv[..., B_NOPE:].reshape(rk, -1).T.astype(BF16)
    return wq_ext, wk, wvt


def kernel(x, c, positions, rel_bias, norm_mix, norm_ffn, w_mod, b_mod, w_in, mla_q_norm, mla_w_uq, mla_kv_norm, mla_w_ukv, gmlp_ln_g, gmlp_ln_b, gmlp_w_s, gmlp_b_s, w_branch, w_out, w_router, router_bias, we_gate, we_up, we_down, ws_gate, ws_up, ws_down, final_norm):
    B, S, D = x.shape
    T = B * S
    L = w_mod.shape[0]
    xt = x.reshape(T, D)

    mod = _modulation(c, w_mod, b_mod)
    mod3 = mod.reshape(L * B * 6, 1, D)
    cos, sin = _rope_tables(positions)
    a_bias = _mixer_a_bias(rel_bias)
    n_runs = (T // min(TOKEN_TILE, S)) * N_EXPERTS
    n_tiles = pl.cdiv(T * TOP_K + n_runs * (ROW_ALIGN - 1), EXPERT_TILE) + N_EXPERTS

    for l in range(L):
        row = lambda k: (lambda b, l=l, k=k: (l * B + b) * 6 + k)
        wa, wb, wc, wg = _split_w_in(w_in[l])
        wq_ext, wk_nope, wv_t = _mla_weights(mla_w_uq[l], mla_w_ukv[l])

        h = _norm_mod(xt, norm_mix[l], mod3, row(1), row(0), S)
        qkv = _matmul(h, wa, BF16, name="proj_a")
        bproj = _matmul(h, wb, F32, name="proj_b")
        z = _matmul(h, wc, BF16, act="gelu", name="proj_c")
        gates = _matmul(h, wg, BF16, act="sigmoid", name="proj_gates")

        y_a = _mixer_a(qkv, a_bias, B, S)
        qp, kp, vt = _mla_prep(bproj, cos, sin, mla_q_norm[l], mla_kv_norm[l], wq_ext, wk_nope, wv_t)
        y_b = _mla_attention(qp, kp, vt, B, S)
        y_c = _mixer_c(z, gmlp_ln_g[l], gmlp_ln_b[l], gmlp_w_s[l], gmlp_b_s[l])

        merged = _branch_merge(y_a, y_b, y_c, w_branch[l].astype(BF16), gates)
        xt = _matmul_residual(merged, w_out[l].astype(BF16), xt, mod3, row(2), S)

        hb, ew, lpos, cnt = _route(xt, norm_ffn[l], mod3, row(4), row(3), w_router[l], router_bias[l], S)
        tables = _moe_tables(cnt, n_tiles)
        xs = _dispatch(hb, lpos, tables, n_tiles * EXPERT_TILE, S)
        ys = _experts(xs, tables, we_gate, we_up, we_down, l)
        wsgu = jnp.concatenate([ws_gate[l], ws_up[l]], axis=1).astype(BF16)
        shared = _shared_expert(hb, wsgu, ws_down[l].astype(BF16))
        xt = _combine(xt, shared, lpos, ew, mod3, row(5), tables, ys, final_norm, l == L - 1, S)

    return xt.reshape(B, S, D)
```

```python
import functools

import jax
import jax.numpy as jnp
from jax import lax
from jax.experimental import pallas as pl
from jax.experimental.pallas import tpu as pltpu

F32 = jnp.float32
BF16 = jnp.bfloat16
U32 = jnp.uint32
I32 = jnp.int32

EPS = 1e-6
NEG_INF = -1e30

D_MODEL = 2048
HALF_D = D_MODEL // 2
CHUNK = 64

A_HEADS = 16
A_HEAD_DIM = 64
A_WIDTH = A_HEADS * A_HEAD_DIM
A_LEFT_CHUNKS = 8
MAX_REL = 128
A_QBLOCK = 256
A_KBLOCKS = 3
A_PAIRS_PER_STEP = 4

B_HEADS = 8
B_NOPE = 128
B_ROPE = 64
B_VDIM = 128
B_Q_LORA = 512
B_KV_LORA = 256
B_QK_PAD = 256
B_VT_ROWS = B_VDIM + 16
LOG2_E = 1.4426950408889634
ROPE_THETA = 10000.0
B_TQ = 1024
B_TK = 1024
B_HEADS_PER_STEP = 4

C_BLOCK = 128
C_GROUPS = 8
C_WIDTH = 1024

N_EXPERTS = 64
EXPERT_GROUP = 8
TOP_K = 8
TOPK_GROUPS = 4
D_EXPERT = 512
ROUTED_SCALE = 2.5
EXPERT_TILE = 512
TOKEN_TILE = 256
ROW_ALIGN = 8
WAIT_ROWS = 2048
PERM_BLOCK = 256
PERM_COLS = 256
COMBINE_BLOCK = 512

LANE = 128
V7X_VMEM_LIMIT = 56 * 1024 * 1024


def _params(*sem):
    return pltpu.CompilerParams(dimension_semantics=sem, vmem_limit_bytes=V7X_VMEM_LIMIT)


def _pack_halves(v):
    w = v.shape[1] // 2
    r = v.astype(BF16).astype(F32)
    return _pack_bf16_pair(r[:, :w], r[:, w:])


def _pack_bf16_pair(lo, hi):
    return (lax.bitcast_convert_type(lo, U32) >> 16) | lax.bitcast_convert_type(hi, U32)


def _unpack_halves(p):
    lo = lax.bitcast_convert_type(p << 16, F32)
    hi = lax.bitcast_convert_type(p & jnp.uint32(0xFFFF0000), F32)
    return lo, hi


def _mod_kernel(c_ref, w_ref, b_ref, o_ref):
    c = c_ref[...]
    ca = (c * jax.nn.sigmoid(c)).astype(BF16)
    o_ref[...] = jnp.dot(ca, w_ref[...].astype(BF16), preferred_element_type=F32) + b_ref[...]


def _modulation(c, w_mod, b_mod):
    L, D, N = w_mod.shape
    B = c.shape[0]
    tn = 512
    return pl.pallas_call(
        _mod_kernel,
        grid=(L, N // tn),
        in_specs=[
            pl.BlockSpec((B, D), lambda l, j: (0, 0)),
            pl.BlockSpec((None, D, tn), lambda l, j: (l, 0, j)),
            pl.BlockSpec((None, 1, tn), lambda l, j: (l, 0, j)),
        ],
        out_specs=pl.BlockSpec((None, B, tn), lambda l, j: (l, 0, j)),
        out_shape=jax.ShapeDtypeStruct((L, B, N), F32),
        compiler_params=_params("parallel", "arbitrary"),
        name="modulation",
    )(c, w_mod, b_mod.reshape(L, 1, N))


def _modulated_norm(x, g, sc, sh):
    r = lax.rsqrt(jnp.mean(x * x, axis=-1, keepdims=True) + EPS)
    return (x * r * g) * (1.0 + sc) + sh


def _norm_mod_kernel(x_ref, g_ref, sc_ref, sh_ref, o_ref):
    o_ref[...] = _modulated_norm(x_ref[...], g_ref[...], sc_ref[...], sh_ref[...]).astype(o_ref.dtype)


def _norm_mod(x, g, mod3, row_sc, row_sh, S, tm=512):
    T, D = x.shape
    per = S // tm
    return pl.pallas_call(
        _norm_mod_kernel,
        grid=(T // tm,),
        in_specs=[
            pl.BlockSpec((tm, D), lambda i: (i, 0)),
            pl.BlockSpec((1, D), lambda i: (0, 0)),
            pl.BlockSpec((None, 1, D), lambda i: (row_sc(i // per), 0, 0)),
            pl.BlockSpec((None, 1, D), lambda i: (row_sh(i // per), 0, 0)),
        ],
        out_specs=pl.BlockSpec((tm, D), lambda i: (i, 0)),
        out_shape=jax.ShapeDtypeStruct((T, D), BF16),
        compiler_params=_params("parallel"),
        name="norm_mod",
    )(x, g.reshape(1, D), mod3, mod3)


def _mm_kernel(a_ref, w_ref, o_ref, *, act):
    acc = jnp.dot(a_ref[...], w_ref[...], preferred_element_type=F32)
    if act == "gelu":
        acc = jax.nn.gelu(acc)
    elif act == "sigmoid":
        acc = jax.nn.sigmoid(acc)
    o_ref[...] = acc.astype(o_ref.dtype)


def _matmul(a, w, out_dtype, act=None, tm=1024, tn=1024, name="matmul"):
    M, K = a.shape
    N = w.shape[1]
    tm = min(tm, M)
    return pl.pallas_call(
        functools.partial(_mm_kernel, act=act),
        grid=(M // tm, N // tn),
        in_specs=[
            pl.BlockSpec((tm, K), lambda i, j: (i, 0)),
            pl.BlockSpec((K, tn), lambda i, j: (0, j)),
        ],
        out_specs=pl.BlockSpec((tm, tn), lambda i, j: (i, j)),
        out_shape=jax.ShapeDtypeStruct((M, N), out_dtype),
        compiler_params=_params("parallel", "arbitrary"),
        name=name,
    )(a, w)


def _mm_residual_kernel(a_ref, w_ref, x_ref, g_ref, o_ref):
    acc = jnp.dot(a_ref[...], w_ref[...], preferred_element_type=F32)
    o_ref[...] = x_ref[...] + g_ref[...] * acc


def _matmul_residual(a, w, x, mod3, row_g, S, tm=1024, tn=1024):
    M, K = a.shape
    N = w.shape[1]
    tm = min(tm, S)
    per = S // tm
    return pl.pallas_call(
        _mm_residual_kernel,
        grid=(M // tm, N // tn),
        in_specs=[
            pl.BlockSpec((tm, K), lambda i, j: (i, 0)),
            pl.BlockSpec((K, tn), lambda i, j: (0, j)),
            pl.BlockSpec((tm, tn), lambda i, j: (i, j)),
            pl.BlockSpec((None, 1, tn), lambda i, j: (row_g(i // per), 0, j)),
        ],
        out_specs=pl.BlockSpec((tm, tn), lambda i, j: (i, j)),
        out_shape=jax.ShapeDtypeStruct((M, N), F32),
        compiler_params=_params("parallel", "arbitrary"),
        name="out_proj_residual",
    )(a, w, x, mod3)


def _mixer_a_kernel(q_ref, k0_ref, k1_ref, k2_ref, v0_ref, v1_ref, v2_ref, bias_ref, o_ref, *, scale):
    i = pl.program_id(2)
    qb = A_QBLOCK
    kw = A_KBLOCKS * qb
    lane = lax.broadcasted_iota(I32, (qb, LANE), 1)

    def attend(mask_before_sequence):
        ones_col = jnp.where(lax.broadcasted_iota(I32, (kw, LANE), 1) == 0, 1.0, 0.0).astype(BF16)
        for pp in range(A_PAIRS_PER_STEP):
            cols = slice(pp * LANE, (pp + 1) * LANE)
            q = q_ref[:, cols]
            k = jnp.concatenate([k0_ref[:, cols], k1_ref[:, cols], k2_ref[:, cols]], axis=0)
            v = jnp.concatenate([v0_ref[:, cols], v1_ref[:, cols], v2_ref[:, cols]], axis=0)
            v_ext = jnp.concatenate([v, ones_col], axis=1)
            outs = []
            for hh in range(2):
                head_lanes = (lane < A_HEAD_DIM) if hh == 0 else (lane >= A_HEAD_DIM)
                qm = jnp.where(head_lanes, q, jnp.zeros_like(q)) * jnp.asarray(scale, q.dtype)
                s = lax.dot_general(qm, k, (((1,), (1,)), ((), ())), preferred_element_type=F32)
                s = s + bias_ref[2 * pp + hh]
                if mask_before_sequence:
                    kpos = (i - (A_KBLOCKS - 1)) * qb + lax.broadcasted_iota(I32, (1, kw), 1)
                    s = jnp.where(kpos >= 0, s, NEG_INF)
                m = jnp.max(s, axis=-1, keepdims=True)
                p = jnp.exp((s - m).astype(BF16))
                o = jnp.dot(p, v_ext, preferred_element_type=F32)
                outs.append(o[:, :LANE] / o[:, LANE:LANE + 1])
            o_ref[:, cols] = jnp.where(lane < A_HEAD_DIM, outs[0], outs[1]).astype(o_ref.dtype)

    @pl.when(i < A_KBLOCKS - 1)
    def _():
        attend(True)

    @pl.when(i >= A_KBLOCKS - 1)
    def _():
        attend(False)


def _mixer_a_bias(rel_bias):
    qb = A_QBLOCK
    kw = A_KBLOCKS * qb
    qi = jnp.arange(qb)[:, None]
    kj = jnp.arange(kw)[None, :]
    qc = qi // CHUNK
    kc = kj // CHUNK
    first = (A_KBLOCKS - 1) * qb // CHUNK - A_LEFT_CHUNKS
    in_band = (kc >= qc + first) & (kc <= qc + first + A_LEFT_CHUNKS)
    period = kw + qb
    m = jnp.arange(period)
    k_minus_q = jnp.where(m < kw, m, m - period)
    d = (A_KBLOCKS - 1) * qb - k_minus_q
    line = rel_bias[:, jnp.clip(d, -MAX_REL, MAX_REL) + MAX_REL].astype(F32)
    heads = rel_bias.shape[0]
    flat = jnp.broadcast_to(line[:, None, :], (heads, qb, period)).reshape(heads, qb * period)
    bias = flat[:, :qb * (period - 1)].reshape(heads, qb, period - 1)[:, :, :kw]
    return jnp.where(in_band[None], bias, NEG_INF)


def _mixer_a(qkv, bias, B, S):
    T = qkv.shape[0]
    qb = A_QBLOCK
    nq = S // qb
    width = A_PAIRS_PER_STEP * LANE
    groups = A_WIDTH // width

    def kv_spec(j, col0):
        return pl.BlockSpec(
            (qb, width), lambda p, b, i: (b * nq + jnp.maximum(i - (A_KBLOCKS - 1) + j, 0), col0 + p))

    return pl.pallas_call(
        functools.partial(_mixer_a_kernel, scale=A_HEAD_DIM ** -0.5),
        grid=(groups, B, nq),
        in_specs=[
            pl.BlockSpec((qb, width), lambda p, b, i: (b * nq + i, p)),
            kv_spec(0, groups), kv_spec(1, groups), kv_spec(2, groups),
            kv_spec(0, 2 * groups), kv_spec(1, 2 * groups), kv_spec(2, 2 * groups),
            pl.BlockSpec((2 * A_PAIRS_PER_STEP, qb, A_KBLOCKS * qb), lambda p, b, i: (p, 0, 0)),
        ],
        out_specs=pl.BlockSpec((qb, width), lambda p, b, i: (b * nq + i, p)),
        out_shape=jax.ShapeDtypeStruct((T, A_WIDTH), BF16),
        compiler_params=_params("parallel", "parallel", "arbitrary"),
        name="mixer_a",
    )(qkv, qkv, qkv, qkv, qkv, qkv, qkv, bias)


def _rope_table_kernel(pos_ref, inv_ref, c_ref, s_ref):
    ang = pos_ref[...].astype(F32) * inv_ref[...]
    lane = lax.broadcasted_iota(I32, ang.shape, 1)
    first_half = (lane % B_ROPE) < (B_ROPE // 2)
    c_ref[...] = jnp.cos(ang)
    sn = jnp.sin(ang)
    s_ref[...] = jnp.where(first_half, -sn, sn)


def _rope_tables(positions, tm=512):
    T = positions.size
    half = B_ROPE // 2
    inv = ROPE_THETA ** (-jnp.arange(half, dtype=F32) / half)
    inv = jnp.tile(inv, LANE // half).reshape(1, LANE)
    return pl.pallas_call(
        _rope_table_kernel,
        grid=(T // tm,),
        in_specs=[pl.BlockSpec((tm, 1), lambda i: (i, 0)), pl.BlockSpec((1, LANE), lambda i: (0, 0))],
        out_specs=[pl.BlockSpec((tm, LANE), lambda i: (i, 0))] * 2,
        out_shape=[jax.ShapeDtypeStruct((T, LANE), F32)] * 2,
        compiler_params=_params("parallel"),
        name="rope_tables",
    )(positions.reshape(T, 1), inv)


def _mla_prep_kernel(bp_ref, c_ref, s_ref, qn_ref, kvn_ref, wq_ref, wk_ref, wvt_ref, q_out, k_out, vt_out, *, scale):
    bp = bp_ref[...]
    cq = bp[:, :B_Q_LORA]
    ckv = bp[:, B_Q_LORA:B_Q_LORA + B_KV_LORA]
    kpe = bp[:, B_Q_LORA + B_KV_LORA:B_Q_LORA + B_KV_LORA + LANE]
    kpe_sw = bp[:, B_Q_LORA + B_KV_LORA + LANE:]
    cqn = cq * lax.rsqrt(jnp.mean(cq * cq, axis=-1, keepdims=True) + EPS) * qn_ref[...]
    ckvn = ckv * lax.rsqrt(jnp.mean(ckv * ckv, axis=-1, keepdims=True) + EPS) * kvn_ref[...]
    qa = jnp.dot(cqn.astype(BF16), wq_ref[...], preferred_element_type=F32)
    ckvb = ckvn.astype(BF16)
    kn = jnp.dot(ckvb, wk_ref[...], preferred_element_type=F32)
    vt = lax.dot_general(wvt_ref[...], ckvb, (((1,), (1,)), ((), ())), preferred_element_type=F32).astype(BF16)
    ones_rows = jnp.ones((B_VT_ROWS - B_VDIM, vt.shape[1]), BF16)
    for h in range(B_HEADS):
        vt_out[h * B_VT_ROWS:h * B_VT_ROWS + B_VDIM, :] = vt[h * B_VDIM:(h + 1) * B_VDIM, :]
        vt_out[h * B_VT_ROWS + B_VDIM:(h + 1) * B_VT_ROWS, :] = ones_rows
    cos = c_ref[...]
    sin = s_ref[...]
    k_rot = (kpe * cos + kpe_sw * sin).astype(BF16)
    sw0 = B_HEADS * B_QK_PAD
    for h in range(B_HEADS):
        c0 = h * B_QK_PAD
        q_out[:, c0:c0 + B_NOPE] = (qa[:, c0:c0 + B_NOPE] * scale).astype(BF16)
        pe = qa[:, c0 + B_NOPE:c0 + B_QK_PAD]
        pe_sw = qa[:, sw0 + h * LANE:sw0 + (h + 1) * LANE]
        q_out[:, c0 + B_NOPE:c0 + B_QK_PAD] = ((pe * cos + pe_sw * sin) * scale).astype(BF16)
        k_out[:, c0:c0 + B_NOPE] = kn[:, h * B_NOPE:(h + 1) * B_NOPE].astype(BF16)
        k_out[:, c0 + B_NOPE:c0 + B_QK_PAD] = k_rot


def _mla_prep(bproj, cos, sin, q_norm, kv_norm, wq, wk, wvt, tm=256):
    T = bproj.shape[0]
    scale = (B_NOPE + B_ROPE) ** -0.5 * LOG2_E
    qk_w = B_HEADS * B_QK_PAD
    v_w = B_HEADS * B_VT_ROWS
    row = lambda i: (i, 0)
    fixed = lambda i: (0, 0)
    return pl.pallas_call(
        functools.partial(_mla_prep_kernel, scale=scale),
        grid=(T // tm,),
        in_specs=[
            pl.BlockSpec((tm, bproj.shape[1]), row),
            pl.BlockSpec((tm, LANE), row),
            pl.BlockSpec((tm, LANE), row),
            pl.BlockSpec((1, B_Q_LORA), fixed),
            pl.BlockSpec((1, B_KV_LORA), fixed),
            pl.BlockSpec(wq.shape, fixed),
            pl.BlockSpec(wk.shape, fixed),
            pl.BlockSpec(wvt.shape, fixed),
        ],
        out_specs=[pl.BlockSpec((tm, qk_w), row), pl.BlockSpec((tm, qk_w), row),
                   pl.BlockSpec((v_w, tm), lambda i: (0, i))],
        out_shape=[jax.ShapeDtypeStruct((T, qk_w), BF16), jax.ShapeDtypeStruct((T, qk_w), BF16),
                   jax.ShapeDtypeStruct((v_w, T), BF16)],
        compiler_params=_params("parallel"),
        name="mla_prep",
    )(bproj, cos, sin, q_norm.reshape(1, -1), kv_norm.reshape(1, -1), wq, wk, wvt)


def _mla_attn_kernel(qt_ref, kt_ref, q_ref, k_ref, vt_ref, o_ref, m_sc, acc_sc, *, tq, tk):
    p = pl.program_id(2)
    qi = qt_ref[p]
    kj = kt_ref[p]

    @pl.when(kj == 0)
    def _():
        m_sc[...] = jnp.full_like(m_sc, NEG_INF)
        acc_sc[...] = jnp.zeros_like(acc_sc)

    def update(g, keys, queries, masked):
        qk_cols = slice(g * B_QK_PAD, (g + 1) * B_QK_PAD)
        v_rows = slice(g * B_VT_ROWS, (g + 1) * B_VT_ROWS)
        s = lax.dot_general(k_ref[keys, qk_cols], q_ref[queries, qk_cols], (((1,), (1,)), ((), ())),
                            preferred_element_type=F32)
        if masked:
            kc = (kj * tk + keys.start + lax.broadcasted_iota(I32, s.shape, 0)) // CHUNK
            qc = (qi * tq + queries.start + lax.broadcasted_iota(I32, s.shape, 1)) // CHUNK
            s = jnp.where(kc <= qc, s, NEG_INF)
        m_prev = m_sc[g:g + 1, queries]
        m_new = jnp.maximum(m_prev, jnp.max(s, axis=0, keepdims=True))
        alpha = jnp.exp2(m_prev - m_new)
        pr = jnp.exp2((s - m_new).astype(BF16))
        acc_sc[v_rows, queries] = alpha * acc_sc[v_rows, queries] + jnp.dot(vt_ref[v_rows, keys], pr,
                                                                            preferred_element_type=F32)
        m_sc[g:g + 1, queries] = m_new

    k_end = (kj + 1) * tk
    crosses_diagonal = k_end > qi * tq

    @pl.when(jnp.logical_not(crosses_diagonal))
    def _():
        for g in range(B_HEADS_PER_STEP):
            update(g, slice(0, tk), slice(0, tq), masked=False)

    @pl.when(crosses_diagonal)
    def _():
        half = tk // 2
        for g in range(B_HEADS_PER_STEP):
            update(g, slice(0, half), slice(0, tq), masked=True)
            update(g, slice(half, tk), slice(half, tq), masked=True)

    @pl.when(k_end == (qi + 1) * tq)
    def _():
        for g in range(B_HEADS_PER_STEP):
            r0 = g * B_VT_ROWS
            row_sum = acc_sc[r0 + B_VDIM:r0 + B_VDIM + 1, :]
            o_ref[:, g * B_VDIM:(g + 1) * B_VDIM] = (acc_sc[r0:r0 + B_VDIM, :] / row_sum).T.astype(o_ref.dtype)


def _mla_attention(qp, kp, vt, B, S):
    T = qp.shape[0]
    tq = min(B_TQ, S)
    tk = min(B_TK, S)
    assert tq == tk, "the diagonal-block split in the kernel assumes square blocks"
    nq = S // tq
    nk = S // tk
    hg = B_HEADS_PER_STEP
    pairs = [(qi, kj) for qi in range(nq) for kj in range((qi + 1) * tq // tk)]
    qt = jnp.asarray([p[0] for p in pairs], I32)
    kt = jnp.asarray([p[1] for p in pairs], I32)
    grid_spec = pltpu.PrefetchScalarGridSpec(
        num_scalar_prefetch=2,
        grid=(B, B_HEADS // hg, len(pairs)),
        in_specs=[
            pl.BlockSpec((tq, hg * B_QK_PAD), lambda b, h, p, qt, kt: (b * nq + qt[p], h)),
            pl.BlockSpec((tk, hg * B_QK_PAD), lambda b, h, p, qt, kt: (b * nk + kt[p], h)),
            pl.BlockSpec((hg * B_VT_ROWS, tk), lambda b, h, p, qt, kt: (h, b * nk + kt[p])),
        ],
        out_specs=pl.BlockSpec((tq, hg * B_VDIM), lambda b, h, p, qt, kt: (b * nq + qt[p], h)),
        scratch_shapes=[pltpu.VMEM((hg, tq), F32), pltpu.VMEM((hg * B_VT_ROWS, tq), F32)],
    )
    return pl.pallas_call(
        functools.partial(_mla_attn_kernel, tq=tq, tk=tk),
        grid_spec=grid_spec,
        out_shape=jax.ShapeDtypeStruct((T, B_HEADS * B_VDIM), BF16),
        compiler_params=_params("parallel", "parallel", "arbitrary"),
        name="mla_attention",
    )(qt, kt, qp, kp, vt)


def _mixer_c_kernel(z_ref, lg_ref, lb_ref, ws_ref, bs_ref, o_ref):
    tm = z_ref.shape[0]
    u = z_ref[:, :C_WIDTH].astype(F32)
    v = z_ref[:, C_WIDTH:].astype(F32)
    mu = jnp.mean(v, axis=-1, keepdims=True)
    var = jnp.mean(jnp.square(v - mu), axis=-1, keepdims=True)
    vn = ((v - mu) * lax.rsqrt(var + EPS) * lg_ref[...] + lb_ref[...]).astype(BF16)
    row = lax.broadcasted_iota(I32, (C_BLOCK, C_BLOCK), 0)
    col = lax.broadcasted_iota(I32, (C_BLOCK, C_BLOCK), 1)
    causal = col <= row
    gd = C_WIDTH // C_GROUPS
    for g in range(C_GROUPS):
        w = jnp.where(causal, ws_ref[g], 0.0).astype(BF16)
        b = bs_ref[:, g:g + 1]
        for n in range(tm // C_BLOCK):
            rows = slice(n * C_BLOCK, (n + 1) * C_BLOCK)
            cols = slice(g * gd, (g + 1) * gd)
            mixed = jnp.dot(w, vn[rows, cols], preferred_element_type=F32) + b
            o_ref[rows, cols] = (u[rows, cols] * mixed).astype(o_ref.dtype)


def _mixer_c(z, ln_g, ln_b, w_s, b_s, tm=256):
    T = z.shape[0]
    return pl.pallas_call(
        _mixer_c_kernel,
        grid=(T // tm,),
        in_specs=[
            pl.BlockSpec((tm, 2 * C_WIDTH), lambda i: (i, 0)),
            pl.BlockSpec((1, C_WIDTH), lambda i: (0, 0)),
            pl.BlockSpec((1, C_WIDTH), lambda i: (0, 0)),
            pl.BlockSpec((C_GROUPS, C_BLOCK, C_BLOCK), lambda i: (0, 0, 0)),
            pl.BlockSpec((C_BLOCK, C_GROUPS), lambda i: (0, 0)),
        ],
        out_specs=pl.BlockSpec((tm, C_WIDTH), lambda i: (i, 0)),
        out_shape=jax.ShapeDtypeStruct((T, C_WIDTH), BF16),
        compiler_params=_params("parallel"),
        name="mixer_c",
    )(z, ln_g.reshape(1, -1), ln_b.reshape(1, -1), w_s, b_s.T)


def _branch_kernel(ya_ref, yb_ref, yc_ref, wb_ref, ga_ref, gb_ref, gc_ref, o_ref):
    acc = ga_ref[...].astype(F32) * jnp.dot(ya_ref[...], wb_ref[0], preferred_element_type=F32)
    acc += gb_ref[...].astype(F32) * jnp.dot(yb_ref[...], wb_ref[1], preferred_element_type=F32)
    acc += gc_ref[...].astype(F32) * jnp.dot(yc_ref[...], wb_ref[2], preferred_element_type=F32)
    o_ref[...] = acc.astype(o_ref.dtype)


def _branch_merge(ya, yb, yc, wb, gates, tm=1024, tn=512):
    T, W = ya.shape
    D = wb.shape[2]
    tm = min(tm, T)
    nj = D // tn
    y_spec = pl.BlockSpec((tm, W), lambda i, j: (i, 0))
    return pl.pallas_call(
        _branch_kernel,
        grid=(T // tm, nj),
        in_specs=[
            y_spec, y_spec, y_spec,
            pl.BlockSpec((3, W, tn), lambda i, j: (0, 0, j)),
            pl.BlockSpec((tm, tn), lambda i, j: (i, j)),
            pl.BlockSpec((tm, tn), lambda i, j: (i, nj + j)),
            pl.BlockSpec((tm, tn), lambda i, j: (i, 2 * nj + j)),
        ],
        out_specs=pl.BlockSpec((tm, tn), lambda i, j: (i, j)),
        out_shape=jax.ShapeDtypeStruct((T, D), BF16),
        compiler_params=_params("parallel", "arbitrary"),
        name="branch_merge",
    )(ya, yb, yc, wb, gates, gates, gates)


def _route_kernel(x_ref, g_ref, sc_ref, sh_ref, wrh_ref, wrl_ref, rb_ref,
                  hb_ref, ew_ref, lpos_ref, cnt_ref):
    tm = x_ref.shape[0]
    h = _modulated_norm(x_ref[...], g_ref[...], sc_ref[...], sh_ref[...])
    hb = h.astype(BF16)
    hb_ref[...] = hb
    h_lo = (h - hb.astype(F32)).astype(BF16)
    dn = (((1,), (1,)), ((), ()))
    logits = (lax.dot_general(wrh_ref[...], hb, dn, preferred_element_type=F32)
              + lax.dot_general(wrh_ref[...], h_lo, dn, preferred_element_type=F32)
              + lax.dot_general(wrl_ref[...], hb, dn, preferred_element_type=F32))
    scores = jax.nn.sigmoid(logits)
    sel = scores + rb_ref[...]
    neg = -jnp.inf
    n_groups = N_EXPERTS // EXPERT_GROUP

    i8 = lax.broadcasted_iota(I32, (EXPERT_GROUP, tm), 0)
    group_scores = []
    for g in range(n_groups):
        blk = sel[g * EXPERT_GROUP:(g + 1) * EXPERT_GROUP, :]
        m1 = jnp.max(blk, axis=0, keepdims=True)
        first = jnp.min(jnp.where(blk == m1, i8, EXPERT_GROUP), axis=0, keepdims=True)
        m2 = jnp.max(jnp.where(i8 == first, neg, blk), axis=0, keepdims=True)
        group_scores.append(m1 + m2)
    gs = jnp.concatenate(group_scores, axis=0)

    ig = lax.broadcasted_iota(I32, (n_groups, tm), 0)
    gsel = jnp.zeros((n_groups, tm), F32)
    for _ in range(TOPK_GROUPS):
        m = jnp.max(gs, axis=0, keepdims=True)
        first = jnp.min(jnp.where(gs == m, ig, n_groups), axis=0, keepdims=True)
        hit = ig == first
        gsel = jnp.where(hit, 1.0, gsel)
        gs = jnp.where(hit, neg, gs)

    masked = jnp.concatenate(
        [jnp.where(gsel[g:g + 1, :] > 0.0, sel[g * EXPERT_GROUP:(g + 1) * EXPERT_GROUP, :], neg)
         for g in range(n_groups)], axis=0)

    ie = lax.broadcasted_iota(I32, (N_EXPERTS, tm), 0)
    chosen = jnp.zeros((N_EXPERTS, tm), F32)
    hits, ws = [], []
    for _ in range(TOP_K):
        m = jnp.max(masked, axis=0, keepdims=True)
        first = jnp.min(jnp.where(masked == m, ie, N_EXPERTS), axis=0, keepdims=True)
        hit = ie == first
        ws.append(jnp.sum(jnp.where(hit, scores, 0.0), axis=0, keepdims=True))
        hits.append(hit)
        chosen = jnp.where(hit, 1.0, chosen)
        masked = jnp.where(hit, neg, masked)
    w = jnp.concatenate(ws, axis=0)
    denom = jnp.sum(w, axis=0, keepdims=True) + 1e-20
    ew_ref[...] = w / denom * ROUTED_SCALE

    s_idx = lax.broadcasted_iota(I32, (tm, tm), 0)
    t_idx = lax.broadcasted_iota(I32, (tm, tm), 1)
    earlier = jnp.where(s_idx < t_idx, 1.0, 0.0).astype(BF16)
    local_rank = jnp.dot(chosen.astype(BF16), earlier, preferred_element_type=F32)
    n_e = jnp.sum(chosen, axis=1, keepdims=True)
    run = jnp.floor((n_e + (ROW_ALIGN - 1.0)) * (1.0 / ROW_ALIGN)) * ROW_ALIGN
    run_b = jnp.broadcast_to(run, (N_EXPERTS, LANE))
    e_row = lax.broadcasted_iota(I32, (N_EXPERTS, N_EXPERTS), 0)
    e_col = lax.broadcasted_iota(I32, (N_EXPERTS, N_EXPERTS), 1)
    before = jnp.where(e_col < e_row, 1.0, 0.0).astype(BF16)
    run_start = jnp.dot(before, run_b.astype(BF16), preferred_element_type=F32)
    base = run_start[:, :1] + local_rank
    lpos = [jnp.sum(jnp.where(hit, base, 0.0), axis=0, keepdims=True) for hit in hits]
    lpos_ref[...] = jnp.concatenate(lpos, axis=0).astype(I32)
    cnt_ref[...] = run_b.astype(I32)


def _route(x, g, mod3, row_sc, row_sh, w_router, router_bias, S):
    T, D = x.shape
    tm = min(TOKEN_TILE, S)
    per = S // tm
    wr_t = w_router.T
    wr_hi = wr_t.astype(BF16)
    wr_lo = (wr_t - wr_hi.astype(F32)).astype(BF16)
    row = lambda i: (i, 0)
    col = lambda i: (0, i)
    fixed = lambda i: (0, 0)
    return pl.pallas_call(
        _route_kernel,
        grid=(T // tm,),
        in_specs=[
            pl.BlockSpec((tm, D), row),
            pl.BlockSpec((1, D), fixed),
            pl.BlockSpec((None, 1, D), lambda i: (row_sc(i // per), 0, 0)),
            pl.BlockSpec((None, 1, D), lambda i: (row_sh(i // per), 0, 0)),
            pl.BlockSpec((N_EXPERTS, D), fixed),
            pl.BlockSpec((N_EXPERTS, D), fixed),
            pl.BlockSpec((N_EXPERTS, 1), fixed),
        ],
        out_specs=[
            pl.BlockSpec((tm, D), row),
            pl.BlockSpec((TOP_K, tm), col),
            pl.BlockSpec((TOP_K, tm), col),
            pl.BlockSpec((N_EXPERTS, LANE), col),
        ],
        out_shape=[
            jax.ShapeDtypeStruct((T, D), BF16),
            jax.ShapeDtypeStruct((TOP_K, T), F32),
            jax.ShapeDtypeStruct((TOP_K, T), I32),
            jax.ShapeDtypeStruct((N_EXPERTS, (T // tm) * LANE), I32),
        ],
        compiler_params=_params("parallel"),
        name="norm_route",
    )(x, g.reshape(1, D), mod3, mod3, wr_hi, wr_lo, router_bias.reshape(N_EXPERTS, 1))


def _moe_tables(cnt, n_tiles):
    runs = cnt[:, ::LANE].T
    total = jnp.sum(runs, axis=0)
    region = (total + EXPERT_TILE - 1) // EXPERT_TILE * EXPERT_TILE
    region_end = jnp.cumsum(region)
    region_start = region_end - region
    dst = region_start[None, :] + jnp.cumsum(runs, axis=0) - runs
    src = jnp.cumsum(runs, axis=1) - runs
    tile_end = region_end // EXPERT_TILE
    tile_expert = jnp.sum(tile_end[None, :] <= jnp.arange(n_tiles, dtype=I32)[:, None], axis=1)
    tile_expert = jnp.minimum(tile_expert, N_EXPERTS - 1).astype(I32)
    as_i32 = lambda a: a.reshape(-1).astype(I32)
    gap_start = jnp.concatenate([region_start + total, region_end[-1:]])
    gap_rows = jnp.concatenate([region - total, n_tiles * EXPERT_TILE - region_end[-1:]])
    return dict(dst=as_i32(dst), src=as_i32(src), chunks=as_i32(runs // ROW_ALIGN),
                gap_start=as_i32(gap_start), gap_chunks=as_i32(gap_rows // ROW_ALIGN),
                tile_expert=tile_expert, n_used=as_i32(tile_end[-1:]))


def _run_copies(i, src_ref, dst_ref, chunks_ref, make_copy):
    for e in range(N_EXPERTS):
        s0 = src_ref[i * N_EXPERTS + e]
        d0 = dst_ref[i * N_EXPERTS + e]

        def body(c, carry, s0=s0, d0=d0):
            make_copy(pl.multiple_of(s0 + c * ROW_ALIGN, ROW_ALIGN),
                      pl.multiple_of(d0 + c * ROW_ALIGN, ROW_ALIGN)).start()
            return carry
        lax.fori_loop(0, chunks_ref[i * N_EXPERTS + e], body, 0)


def _tile_rows(i, src_ref, chunks_ref):
    last_run = i * N_EXPERTS + N_EXPERTS - 1
    return src_ref[last_run] + chunks_ref[last_run] * ROW_ALIGN


def _wait_rows(n_rows, src, dst, sem, max_rows):
    def wait_for(size):
        pltpu.make_async_copy(src.at[pl.ds(0, size)], dst.at[pl.ds(0, size)], sem).wait()

    if max_rows >= 2 * WAIT_ROWS:
        def body(c, carry):
            wait_for(WAIT_ROWS)
            return carry
        lax.fori_loop(0, n_rows // WAIT_ROWS, body, 0)
        n_rows = n_rows % WAIT_ROWS
        max_rows = WAIT_ROWS - ROW_ALIGN
    size = ROW_ALIGN
    while size <= max_rows:
        @pl.when((n_rows & size) != 0)
        def _(size=size):
            wait_for(size)
        size *= 2


def _dispatch_kernel(src_ref, dst_ref, chunks_ref, ts_ref, tc_ref, hb_ref, lp_ref, xs_hbm,
                     xloc, pbuf, zrows, sem):
    i = pl.program_id(0)
    n = pl.num_programs(0)
    tt = hb_ref.shape[0]
    slot = i % 2

    @pl.when(i == 0)
    def _():
        zrows[...] = jnp.zeros_like(zrows)
        gap_chunks = 0
        for e in range(N_EXPERTS + 1):
            def body(c, carry, e=e):
                pltpu.make_async_copy(
                    zrows, xs_hbm.at[pl.ds(pl.multiple_of(ts_ref[e] + c * ROW_ALIGN, ROW_ALIGN), ROW_ALIGN)],
                    sem.at[0]).start()
                return carry
            lax.fori_loop(0, tc_ref[e], body, 0)
            gap_chunks = gap_chunks + tc_ref[e]
        _wait_rows(gap_chunks * ROW_ALIGN, xs_hbm, xs_hbm, sem.at[0], xs_hbm.shape[0])

    lp = lp_ref[...]

    def onehot_block(rb, carry):
        r0 = pl.multiple_of(rb * PERM_BLOCK, PERM_BLOCK)
        rows = r0 + lax.broadcasted_iota(I32, (PERM_BLOCK, tt), 0)
        onehot = jnp.zeros((PERM_BLOCK, tt), F32)
        for k in range(TOP_K):
            onehot = jnp.where(rows == lp[k:k + 1, :], 1.0, onehot)
        pbuf[pl.ds(r0, PERM_BLOCK), :] = onehot.astype(BF16)
        return carry
    lax.fori_loop(0, pbuf.shape[0] // PERM_BLOCK, onehot_block, 0)

    perm = pbuf[...]
    for c in range(HALF_D // PERM_COLS):
        lo_cols = slice(c * PERM_COLS, (c + 1) * PERM_COLS)
        hi_cols = slice(HALF_D + c * PERM_COLS, HALF_D + (c + 1) * PERM_COLS)
        lo = jnp.dot(perm, hb_ref[:, lo_cols], preferred_element_type=F32)
        hi = jnp.dot(perm, hb_ref[:, hi_cols], preferred_element_type=F32)
        xloc[slot, :, lo_cols] = _pack_bf16_pair(lo, hi)

    def copies_from(buf_slot):
        def make_copy(s, d):
            return pltpu.make_async_copy(xloc.at[buf_slot, pl.ds(s, ROW_ALIGN)], xs_hbm.at[pl.ds(d, ROW_ALIGN)],
                                         sem.at[buf_slot])
        return make_copy
    _run_copies(i, src_ref, dst_ref, chunks_ref, copies_from(slot))
    local_rows = xloc.shape[1]

    @pl.when(i > 0)
    def _():
        _wait_rows(_tile_rows(i - 1, src_ref, chunks_ref), xloc.at[1 - slot], xs_hbm, sem.at[1 - slot], local_rows)

    @pl.when(i == n - 1)
    def _():
        _wait_rows(_tile_rows(i, src_ref, chunks_ref), xloc.at[slot], xs_hbm, sem.at[slot], local_rows)


def _local_rows(tt):
    rows = TOP_K * tt + N_EXPERTS * ROW_ALIGN
    return (rows + COMBINE_BLOCK - 1) // COMBINE_BLOCK * COMBINE_BLOCK


def _dispatch(hb, lpos, tables, n_rows, S):
    T, D = hb.shape
    tt = min(TOKEN_TILE, S)
    grid_spec = pltpu.PrefetchScalarGridSpec(
        num_scalar_prefetch=5,
        grid=(T // tt,),
        in_specs=[
            pl.BlockSpec((tt, D), lambda i, *_: (i, 0)),
            pl.BlockSpec((TOP_K, tt), lambda i, *_: (0, i)),
        ],
        out_specs=pl.BlockSpec(memory_space=pl.ANY),
        scratch_shapes=[pltpu.VMEM((2, _local_rows(tt), HALF_D), U32), pltpu.VMEM((_local_rows(tt), tt), BF16),
                        pltpu.VMEM((ROW_ALIGN, HALF_D), U32), pltpu.SemaphoreType.DMA((2,))],
    )
    return pl.pallas_call(
        _dispatch_kernel,
        grid_spec=grid_spec,
        out_shape=jax.ShapeDtypeStruct((n_rows, HALF_D), U32),
        compiler_params=_params("arbitrary"),
        name="moe_dispatch",
    )(tables["src"], tables["dst"], tables["chunks"], tables["gap_start"], tables["gap_chunks"], hb, lpos)


def _expert_kernel(te_ref, nu_ref, x_ref, wg_ref, wu_ref, wd_ref, y_ref, wgu_bf, wd_bf):
    i = pl.program_id(0)

    @pl.when(i < nu_ref[0])
    def _():
        @pl.when((i == 0) | (te_ref[i] != te_ref[jnp.maximum(i - 1, 0)]))
        def _():
            wgu_bf[:, :D_EXPERT] = wg_ref[...].astype(BF16)
            wgu_bf[:, D_EXPERT:] = wu_ref[...].astype(BF16)
            wd_bf[...] = wd_ref[...].astype(BF16)

        x_lo, x_hi = _unpack_halves(x_ref[...])
        gu = jnp.dot(x_lo.astype(BF16), wgu_bf[:HALF_D, :], preferred_element_type=F32)
        gu += jnp.dot(x_hi.astype(BF16), wgu_bf[HALF_D:, :], preferred_element_type=F32)
        gate = gu[:, :D_EXPERT]
        act = (gate * jax.nn.sigmoid(gate) * gu[:, D_EXPERT:]).astype(BF16)
        y = jnp.dot(act, wd_bf[...], preferred_element_type=F32)
        y_ref[...] = _pack_halves(y)

    @pl.when(i >= nu_ref[0])
    def _():
        y_ref[...] = jnp.zeros_like(y_ref)


def _experts(xs, tables, we_gate, we_up, we_down, layer):
    tm = EXPERT_TILE
    n_tiles = xs.shape[0] // tm
    used = lambda i, nu: jnp.minimum(i, nu[0] - 1)
    expert_block = lambda i, te, nu: (layer, te[used(i, nu)], 0, 0)
    row_block = lambda i, te, nu: (used(i, nu), 0)
    grid_spec = pltpu.PrefetchScalarGridSpec(
        num_scalar_prefetch=2,
        grid=(n_tiles,),
        in_specs=[
            pl.BlockSpec((tm, HALF_D), row_block),
            pl.BlockSpec((None, None, D_MODEL, D_EXPERT), expert_block),
            pl.BlockSpec((None, None, D_MODEL, D_EXPERT), expert_block),
            pl.BlockSpec((None, None, D_EXPERT, D_MODEL), expert_block),
        ],
        out_specs=pl.BlockSpec((tm, HALF_D), lambda i, te, nu: (i, 0)),
        scratch_shapes=[pltpu.VMEM((D_MODEL, 2 * D_EXPERT), BF16), pltpu.VMEM((D_EXPERT, D_MODEL), BF16)],
    )
    return pl.pallas_call(
        _expert_kernel,
        grid_spec=grid_spec,
        out_shape=jax.ShapeDtypeStruct(xs.shape, U32),
        compiler_params=_params("arbitrary"),
        name="experts",
    )(tables["tile_expert"], tables["n_used"], xs, we_gate, we_up, we_down)


def _shared_expert_kernel(h_ref, wgu_ref, wd_ref, o_ref):
    gu = jnp.dot(h_ref[...], wgu_ref[...], preferred_element_type=F32)
    gate = gu[:, :D_EXPERT]
    act = (gate * jax.nn.sigmoid(gate) * gu[:, D_EXPERT:]).astype(BF16)
    o_ref[...] = jnp.dot(act, wd_ref[...], preferred_element_type=F32).astype(o_ref.dtype)


def _shared_expert(hb, wsgu, wsd, tm=512):
    T, D = hb.shape
    return pl.pallas_call(
        _shared_expert_kernel,
        grid=(T // tm,),
        in_specs=[pl.BlockSpec((tm, D), lambda i: (i, 0)), pl.BlockSpec(wsgu.shape, lambda i: (0, 0)),
                  pl.BlockSpec(wsd.shape, lambda i: (0, 0))],
        out_specs=pl.BlockSpec((tm, D), lambda i: (i, 0)),
        out_shape=jax.ShapeDtypeStruct((T, D), BF16),
        compiler_params=_params("parallel"),
        name="shared_expert",
    )(hb, wsgu, wsd)


def _combine_kernel(src_ref, dst_ref, chunks_ref, lp_ref, ew_ref, x_ref, sh_ref, g2_ref, fg_ref, ys_hbm,
                    o_ref, yloc, acc, sem, *, final_norm):
    i = pl.program_id(0)
    n = pl.num_programs(0)
    tt = x_ref.shape[0]
    slot = i % 2

    def copies_into(buf_slot):
        def make_copy(s, d):
            return pltpu.make_async_copy(ys_hbm.at[pl.ds(d, ROW_ALIGN)], yloc.at[buf_slot, pl.ds(s, ROW_ALIGN)],
                                         sem.at[buf_slot])
        return make_copy

    @pl.when(i == 0)
    def _():
        yloc[...] = jnp.zeros_like(yloc)
        _run_copies(i, src_ref, dst_ref, chunks_ref, copies_into(0))

    tile_rows = _tile_rows(i, src_ref, chunks_ref)
    _wait_rows(tile_rows, ys_hbm, yloc.at[slot], sem.at[slot], yloc.shape[1])

    @pl.when(i + 1 < n)
    def _():
        _run_copies(i + 1, src_ref, dst_ref, chunks_ref, copies_into(1 - slot))

    lp = lp_ref[...]
    ew = ew_ref[...]
    acc[...] = jnp.zeros_like(acc)

    def sum_block(rb, carry):
        r0 = pl.multiple_of(rb * COMBINE_BLOCK, COMBINE_BLOCK)
        rows = r0 + lax.broadcasted_iota(I32, (COMBINE_BLOCK, tt), 0)
        wmat = jnp.zeros((COMBINE_BLOCK, tt), F32)
        for k in range(TOP_K):
            wmat = jnp.where(rows == lp[k:k + 1, :], ew[k:k + 1, :], wmat)
        w_row = jnp.sum(wmat, axis=1, keepdims=True)
        onehot = jnp.where(wmat != 0.0, 1.0, 0.0).astype(BF16)
        y_lo, y_hi = _unpack_halves(yloc[slot, pl.ds(r0, COMBINE_BLOCK), :])
        dn = (((0,), (0,)), ((), ()))
        acc[:HALF_D, :] += lax.dot_general((w_row * y_lo).astype(BF16), onehot, dn, preferred_element_type=F32)
        acc[HALF_D:, :] += lax.dot_general((w_row * y_hi).astype(BF16), onehot, dn, preferred_element_type=F32)
        return carry
    lax.fori_loop(0, pl.cdiv(tile_rows, COMBINE_BLOCK), sum_block, 0)

    out = x_ref[...] + g2_ref[...] * (sh_ref[...].astype(F32) + acc[...].T)
    if final_norm:
        out = out * lax.rsqrt(jnp.mean(out * out, axis=-1, keepdims=True) + EPS) * fg_ref[...]
    o_ref[...] = out


def _combine(x, shared, lpos, ew, mod3, row_g, tables, ys, final_g, final_norm, S):
    T, D = x.shape
    tt = min(TOKEN_TILE, S)
    per = S // tt
    row = lambda i, *_: (i, 0)
    col = lambda i, *_: (0, i)
    grid_spec = pltpu.PrefetchScalarGridSpec(
        num_scalar_prefetch=3,
        grid=(T // tt,),
        in_specs=[
            pl.BlockSpec((TOP_K, tt), col),
            pl.BlockSpec((TOP_K, tt), col),
            pl.BlockSpec((tt, D), row),
            pl.BlockSpec((tt, D), row),
            pl.BlockSpec((None, 1, D), lambda i, *_: (row_g(i // per), 0, 0)),
            pl.BlockSpec((1, D), lambda i, *_: (0, 0)),
            pl.BlockSpec(memory_space=pl.ANY),
        ],
        out_specs=pl.BlockSpec((tt, D), row),
        scratch_shapes=[pltpu.VMEM((2, _local_rows(tt), HALF_D), U32), pltpu.VMEM((D, tt), F32),
                        pltpu.SemaphoreType.DMA((2,))],
    )
    return pl.pallas_call(
        functools.partial(_combine_kernel, final_norm=final_norm),
        grid_spec=grid_spec,
        out_shape=jax.ShapeDtypeStruct((T, D), F32),
        compiler_params=_params("arbitrary"),
        name="moe_combine",
    )(tables["src"], tables["dst"], tables["chunks"], lpos, ew, x, shared, mod3, final_g.reshape(1, D), ys)


def _swap_halves(w):
    half = w.shape[-1] // 2
    return jnp.concatenate([w[..., half:], w[..., :half]], axis=-1)


def _split_w_in(w_in):
    a_cols = 3 * A_WIDTH
    b_cols = B_Q_LORA + B_KV_LORA + B_ROPE
    c_cols = 2 * C_WIDTH
    wa = w_in[:, :a_cols].astype(BF16)
    wb_raw = w_in[:, a_cols:a_cols + b_cols]
    kpe = wb_raw[:, B_Q_LORA + B_KV_LORA:]
    zeros = jnp.zeros_like(kpe)
    wb = jnp.concatenate([wb_raw[:, :B_Q_LORA + B_KV_LORA], kpe, zeros, _swap_halves(kpe), zeros],
                         axis=1).astype(BF16)
    wc = w_in[:, a_cols + b_cols:a_cols + b_cols + c_cols].astype(BF16)
    wg = w_in[:, a_cols + b_cols + c_cols:].astype(BF16)
    return wa, wb, wc, wg


def _mla_weights(w_uq, w_ukv):
    r = w_uq.shape[0]
    wq = w_uq.reshape(r, B_HEADS, B_NOPE + B_ROPE)
    nope, pe = wq[..., :B_NOPE], wq[..., B_NOPE:]
    zeros = jnp.zeros_like(pe)
    main = jnp.concatenate([nope, pe, zeros], axis=-1).reshape(r, B_HEADS * B_QK_PAD)
    swapped = jnp.concatenate([_swap_halves(pe), zeros], axis=-1).reshape(r, B_HEADS * LANE)
    wq_ext = jnp.concatenate([main, swapped], axis=1).astype(BF16)
    rk = w_ukv.shape[0]
    wkv = w_ukv.reshape(rk, B_HEADS, B_NOPE + B_VDIM)
    wk = wkv[..., :B_NOPE].reshape(rk, -1).astype(BF16)
    wvt = wkv[..., B_NOPE:].reshape(rk, -1).T.astype(BF16)
    return wq_ext, wk, wvt


def kernel(x, c, positions, rel_bias, norm_mix, norm_ffn, w_mod, b_mod, w_in, mla_q_norm, mla_w_uq, mla_kv_norm, mla_w_ukv, gmlp_ln_g, gmlp_ln_b, gmlp_w_s, gmlp_b_s, w_branch, w_out, w_router, router_bias, we_gate, we_up, we_down, ws_gate, ws_up, ws_down, final_norm):
    B, S, D = x.shape
    T = B * S
    L = w_mod.shape[0]
    xt = x.reshape(T, D)

    mod = _modulation(c, w_mod, b_mod)
    mod3 = mod.reshape(L * B * 6, 1, D)
    cos, sin = _rope_tables(positions)
    a_bias = _mixer_a_bias(rel_bias)
    n_runs = (T // min(TOKEN_TILE, S)) * N_EXPERTS
    n_tiles = pl.cdiv(T * TOP_K + n_runs * (ROW_ALIGN - 1), EXPERT_TILE) + N_EXPERTS

    for l in range(L):
        row = lambda k: (lambda b, l=l, k=k: (l * B + b) * 6 + k)
        wa, wb, wc, wg = _split_w_in(w_in[l])
        wq_ext, wk_nope, wv_t = _mla_weights(mla_w_uq[l], mla_w_ukv[l])

        h = _norm_mod(xt, norm_mix[l], mod3, row(1), row(0), S)
        qkv = _matmul(h, wa, BF16, name="proj_a")
        bproj = _matmul(h, wb, F32, name="proj_b")
        z = _matmul(h, wc, BF16, act="gelu", name="proj_c")
        gates = _matmul(h, wg, BF16, act="sigmoid", name="proj_gates")

        y_a = _mixer_a(qkv, a_bias, B, S)
        qp, kp, vt = _mla_prep(bproj, cos, sin, mla_q_norm[l], mla_kv_norm[l], wq_ext, wk_nope, wv_t)
        y_b = _mla_attention(qp, kp, vt, B, S)
        y_c = _mixer_c(z, gmlp_ln_g[l], gmlp_ln_b[l], gmlp_w_s[l], gmlp_b_s[l])

        merged = _branch_merge(y_a, y_b, y_c, w_branch[l].astype(BF16), gates)
        xt = _matmul_residual(merged, w_out[l].astype(BF16), xt, mod3, row(2), S)

        hb, ew, lpos, cnt = _route(xt, norm_ffn[l], mod3, row(4), row(3), w_router[l], router_bias[l], S)
        tables = _moe_tables(cnt, n_tiles)
        xs = _dispatch(hb, lpos, tables, n_tiles * EXPERT_TILE, S)
        ys = _experts(xs, tables, we_gate, we_up, we_down, l)
        wsgu = jnp.concatenate([ws_gate[l], ws_up[l]], axis=1).astype(BF16)
        shared = _shared_expert(hb, wsgu, ws_down[l].astype(BF16))
        xt = _combine(xt, shared, lpos, ew, mod3, row(5), tables, ys, final_norm, l == L - 1, S)

    return xt.reshape(B, S, D)
```

```python
import functools

import jax
import jax.numpy as jnp
from jax import lax
from jax.experimental import pallas as pl
from jax.experimental.pallas import tpu as pltpu

F32 = jnp.float32
BF16 = jnp.bfloat16
U32 = jnp.uint32
I32 = jnp.int32

EPS = 1e-6
NEG_INF = -1e30

D_MODEL = 2048
HALF_D = D_MODEL // 2
CHUNK = 64

A_HEADS = 16
A_HEAD_DIM = 64
A_WIDTH = A_HEADS * A_HEAD_DIM
A_LEFT_CHUNKS = 8
MAX_REL = 128
A_QBLOCK = 256
A_KBLOCKS = 3
A_PAIRS_PER_STEP = 4

B_HEADS = 8
B_NOPE = 128
B_ROPE = 64
B_VDIM = 128
B_Q_LORA = 512
B_KV_LORA = 256
B_QK_PAD = 256
B_VT_ROWS = B_VDIM + 16
LOG2_E = 1.4426950408889634
ROPE_THETA = 10000.0
B_TQ = 1024
B_TK = 1024
B_HEADS_PER_STEP = 4

C_BLOCK = 128
C_GROUPS = 8
C_WIDTH = 1024

N_EXPERTS = 64
EXPERT_GROUP = 8
TOP_K = 8
TOPK_GROUPS = 4
D_EXPERT = 512
ROUTED_SCALE = 2.5
EXPERT_TILE = 512
TOKEN_TILE = 256
ROW_ALIGN = 8
WAIT_ROWS = 2048
PERM_BLOCK = 256
PERM_COLS = 256
COMBINE_BLOCK = 512

LANE = 128
V7X_VMEM_LIMIT = 56 * 1024 * 1024


def _params(*sem):
    return pltpu.CompilerParams(dimension_semantics=sem, vmem_limit_bytes=V7X_VMEM_LIMIT)


def _pack_halves(v):
    w = v.shape[1] // 2
    r = v.astype(BF16).astype(F32)
    return _pack_bf16_pair(r[:, :w], r[:, w:])


def _pack_bf16_pair(lo, hi):
    return (lax.bitcast_convert_type(lo, U32) >> 16) | lax.bitcast_convert_type(hi, U32)


def _unpack_halves(p):
    lo = lax.bitcast_convert_type(p << 16, F32)
    hi = lax.bitcast_convert_type(p & jnp.uint32(0xFFFF0000), F32)
    return lo, hi


def _mod_kernel(c_ref, w_ref, b_ref, o_ref):
    c = c_ref[...]
    ca = (c * jax.nn.sigmoid(c)).astype(BF16)
    o_ref[...] = jnp.dot(ca, w_ref[...].astype(BF16), preferred_element_type=F32) + b_ref[...]


def _modulation(c, w_mod, b_mod):
    L, D, N = w_mod.shape
    B = c.shape[0]
    tn = 512
    return pl.pallas_call(
        _mod_kernel,
        grid=(L, N // tn),
        in_specs=[
            pl.BlockSpec((B, D), lambda l, j: (0, 0)),
            pl.BlockSpec((None, D, tn), lambda l, j: (l, 0, j)),
            pl.BlockSpec((None, 1, tn), lambda l, j: (l, 0, j)),
        ],
        out_specs=pl.BlockSpec((None, B, tn), lambda l, j: (l, 0, j)),
        out_shape=jax.ShapeDtypeStruct((L, B, N), F32),
        compiler_params=_params("parallel", "arbitrary"),
        name="modulation",
    )(c, w_mod, b_mod.reshape(L, 1, N))


def _modulated_norm(x, g, sc, sh):
    r = lax.rsqrt(jnp.mean(x * x, axis=-1, keepdims=True) + EPS)
    return (x * r * g) * (1.0 + sc) + sh


def _norm_mod_kernel(x_ref, g_ref, sc_ref, sh_ref, o_ref):
    o_ref[...] = _modulated_norm(x_ref[...], g_ref[...], sc_ref[...], sh_ref[...]).astype(o_ref.dtype)


def _norm_mod(x, g, mod3, row_sc, row_sh, S, tm=512):
    T, D = x.shape
    per = S // tm
    return pl.pallas_call(
        _norm_mod_kernel,
        grid=(T // tm,),
        in_specs=[
            pl.BlockSpec((tm, D), lambda i: (i, 0)),
            pl.BlockSpec((1, D), lambda i: (0, 0)),
            pl.BlockSpec((None, 1, D), lambda i: (row_sc(i // per), 0, 0)),
            pl.BlockSpec((None, 1, D), lambda i: (row_sh(i // per), 0, 0)),
        ],
        out_specs=pl.BlockSpec((tm, D), lambda i: (i, 0)),
        out_shape=jax.ShapeDtypeStruct((T, D), BF16),
        compiler_params=_params("parallel"),
        name="norm_mod",
    )(x, g.reshape(1, D), mod3, mod3)


def _mm_kernel(a_ref, w_ref, o_ref, *, act):
    acc = jnp.dot(a_ref[...], w_ref[...], preferred_element_type=F32)
    if act == "gelu":
        acc = jax.nn.gelu(acc)
    elif act == "sigmoid":
        acc = jax.nn.sigmoid(acc)
    o_ref[...] = acc.astype(o_ref.dtype)


def _matmul(a, w, out_dtype, act=None, tm=1024, tn=1024, name="matmul"):
    M, K = a.shape
    N = w.shape[1]
    tm = min(tm, M)
    return pl.pallas_call(
        functools.partial(_mm_kernel, act=act),
        grid=(M // tm, N // tn),
        in_specs=[
            pl.BlockSpec((tm, K), lambda i, j: (i, 0)),
            pl.BlockSpec((K, tn), lambda i, j: (0, j)),
        ],
        out_specs=pl.BlockSpec((tm, tn), lambda i, j: (i, j)),
        out_shape=jax.ShapeDtypeStruct((M, N), out_dtype),
        compiler_params=_params("parallel", "arbitrary"),
        name=name,
    )(a, w)


def _mm_residual_kernel(a_ref, w_ref, x_ref, g_ref, o_ref):
    acc = jnp.dot(a_ref[...], w_ref[...], preferred_element_type=F32)
    o_ref[...] = x_ref[...] + g_ref[...] * acc


def _matmul_residual(a, w, x, mod3, row_g, S, tm=1024, tn=1024):
    M, K = a.shape
    N = w.shape[1]
    tm = min(tm, S)
    per = S // tm
    return pl.pallas_call(
        _mm_residual_kernel,
        grid=(M // tm, N // tn),
        in_specs=[
            pl.BlockSpec((tm, K), lambda i, j: (i, 0)),
            pl.BlockSpec((K, tn), lambda i, j: (0, j)),
            pl.BlockSpec((tm, tn), lambda i, j: (i, j)),
            pl.BlockSpec((None, 1, tn), lambda i, j: (row_g(i // per), 0, j)),
        ],
        out_specs=pl.BlockSpec((tm, tn), lambda i, j: (i, j)),
        out_shape=jax.ShapeDtypeStruct((M, N), F32),
        compiler_params=_params("parallel", "arbitrary"),
        name="out_proj_residual",
    )(a, w, x, mod3)


def _mixer_a_kernel(q_ref, k0_ref, k1_ref, k2_ref, v0_ref, v1_ref, v2_ref, bias_ref, o_ref, *, scale):
    i = pl.program_id(2)
    qb = A_QBLOCK
    kw = A_KBLOCKS * qb
    lane = lax.broadcasted_iota(I32, (qb, LANE), 1)

    def attend(mask_before_sequence):
        ones_col = jnp.where(lax.broadcasted_iota(I32, (kw, LANE), 1) == 0, 1.0, 0.0).astype(BF16)
        for pp in range(A_PAIRS_PER_STEP):
            cols = slice(pp * LANE, (pp + 1) * LANE)
            q = q_ref[:, cols]
            k = jnp.concatenate([k0_ref[:, cols], k1_ref[:, cols], k2_ref[:, cols]], axis=0)
            v = jnp.concatenate([v0_ref[:, cols], v1_ref[:, cols], v2_ref[:, cols]], axis=0)
            v_ext = jnp.concatenate([v, ones_col], axis=1)
            outs = []
            for hh in range(2):
                head_lanes = (lane < A_HEAD_DIM) if hh == 0 else (lane >= A_HEAD_DIM)
                qm = jnp.where(head_lanes, q, jnp.zeros_like(q)) * jnp.asarray(scale, q.dtype)
                s = lax.dot_general(qm, k, (((1,), (1,)), ((), ())), preferred_element_type=F32)
                s = s + bias_ref[2 * pp + hh]
                if mask_before_sequence:
                    kpos = (i - (A_KBLOCKS - 1)) * qb + lax.broadcasted_iota(I32, (1, kw), 1)
                    s = jnp.where(kpos >= 0, s, NEG_INF)
                m = jnp.max(s, axis=-1, keepdims=True)
                p = jnp.exp((s - m).astype(BF16))
                o = jnp.dot(p, v_ext, preferred_element_type=F32)
                outs.append(o[:, :LANE] / o[:, LANE:LANE + 1])
            o_ref[:, cols] = jnp.where(lane < A_HEAD_DIM, outs[0], outs[1]).astype(o_ref.dtype)

    @pl.when(i < A_KBLOCKS - 1)
    def _():
        attend(True)

    @pl.when(i >= A_KBLOCKS - 1)
    def _():
        attend(False)


def _mixer_a_bias(rel_bias):
    qb = A_QBLOCK
    kw = A_KBLOCKS * qb
    qi = jnp.arange(qb)[:, None]
    kj = jnp.arange(kw)[None, :]
    qc = qi // CHUNK
    kc = kj // CHUNK
    first = (A_KBLOCKS - 1) * qb // CHUNK - A_LEFT_CHUNKS
    in_band = (kc >= qc + first) & (kc <= qc + first + A_LEFT_CHUNKS)
    period = kw + qb
    m = jnp.arange(period)
    k_minus_q = jnp.where(m < kw, m, m - period)
    d = (A_KBLOCKS - 1) * qb - k_minus_q
    line = rel_bias[:, jnp.clip(d, -MAX_REL, MAX_REL) + MAX_REL].astype(F32)
    heads = rel_bias.shape[0]
    flat = jnp.broadcast_to(line[:, None, :], (heads, qb, period)).reshape(heads, qb * period)
    bias = flat[:, :qb * (period - 1)].reshape(heads, qb, period - 1)[:, :, :kw]
    return jnp.where(in_band[None], bias, NEG_INF)


def _mixer_a(qkv, bias, B, S):
    T = qkv.shape[0]
    qb = A_QBLOCK
    nq = S // qb
    width = A_PAIRS_PER_STEP * LANE
    groups = A_WIDTH // width

    def kv_spec(j, col0):
        return pl.BlockSpec(
            (qb, width), lambda p, b, i: (b * nq + jnp.maximum(i - (A_KBLOCKS - 1) + j, 0), col0 + p))

    return pl.pallas_call(
        functools.partial(_mixer_a_kernel, scale=A_HEAD_DIM ** -0.5),
        grid=(groups, B, nq),
        in_specs=[
            pl.BlockSpec((qb, width), lambda p, b, i: (b * nq + i, p)),
            kv_spec(0, groups), kv_spec(1, groups), kv_spec(2, groups),
            kv_spec(0, 2 * groups), kv_spec(1, 2 * groups), kv_spec(2, 2 * groups),
            pl.BlockSpec((2 * A_PAIRS_PER_STEP, qb, A_KBLOCKS * qb), lambda p, b, i: (p, 0, 0)),
        ],
        out_specs=pl.BlockSpec((qb, width), lambda p, b, i: (b * nq + i, p)),
        out_shape=jax.ShapeDtypeStruct((T, A_WIDTH), BF16),
        compiler_params=_params("parallel", "parallel", "arbitrary"),
        name="mixer_a",
    )(qkv, qkv, qkv, qkv, qkv, qkv, qkv, bias)


def _rope_table_kernel(pos_ref, inv_ref, c_ref, s_ref):
    ang = pos_ref[...].astype(F32) * inv_ref[...]
    lane = lax.broadcasted_iota(I32, ang.shape, 1)
    first_half = (lane % B_ROPE) < (B_ROPE // 2)
    c_ref[...] = jnp.cos(ang)
    sn = jnp.sin(ang)
    s_ref[...] = jnp.where(first_half, -sn, sn)


def _rope_tables(positions, tm=512):
    T = positions.size
    half = B_ROPE // 2
    inv = ROPE_THETA ** (-jnp.arange(half, dtype=F32) / half)
    inv = jnp.tile(inv, LANE // half).reshape(1, LANE)
    return pl.pallas_call(
        _rope_table_kernel,
        grid=(T // tm,),
        in_specs=[pl.BlockSpec((tm, 1), lambda i: (i, 0)), pl.BlockSpec((1, LANE), lambda i: (0, 0))],
        out_specs=[pl.BlockSpec((tm, LANE), lambda i: (i, 0))] * 2,
        out_shape=[jax.ShapeDtypeStruct((T, LANE), F32)] * 2,
        compiler_params=_params("parallel"),
        name="rope_tables",
    )(positions.reshape(T, 1), inv)


def _mla_prep_kernel(bp_ref, c_ref, s_ref, qn_ref, kvn_ref, wq_ref, wk_ref, wvt_ref, q_out, k_out, vt_out, *, scale):
    bp = bp_ref[...]
    cq = bp[:, :B_Q_LORA]
    ckv = bp[:, B_Q_LORA:B_Q_LORA + B_KV_LORA]
    kpe = bp[:, B_Q_LORA + B_KV_LORA:B_Q_LORA + B_KV_LORA + LANE]
    kpe_sw = bp[:, B_Q_LORA + B_KV_LORA + LANE:]
    cqn = cq * lax.rsqrt(jnp.mean(cq * cq, axis=-1, keepdims=True) + EPS) * qn_ref[...]
    ckvn = ckv * lax.rsqrt(jnp.mean(ckv * ckv, axis=-1, keepdims=True) + EPS) * kvn_ref[...]
    qa = jnp.dot(cqn.astype(BF16), wq_ref[...], preferred_element_type=F32)
    ckvb = ckvn.astype(BF16)
    kn = jnp.dot(ckvb, wk_ref[...], preferred_element_type=F32)
    vt = lax.dot_general(wvt_ref[...], ckvb, (((1,), (1,)), ((), ())), preferred_element_type=F32).astype(BF16)
    ones_rows = jnp.ones((B_VT_ROWS - B_VDIM, vt.shape[1]), BF16)
    for h in range(B_HEADS):
        vt_out[h * B_VT_ROWS:h * B_VT_ROWS + B_VDIM, :] = vt[h * B_VDIM:(h + 1) * B_VDIM, :]
        vt_out[h * B_VT_ROWS + B_VDIM:(h + 1) * B_VT_ROWS, :] = ones_rows
    cos = c_ref[...]
    sin = s_ref[...]
    k_rot = (kpe * cos + kpe_sw * sin).astype(BF16)
    sw0 = B_HEADS * B_QK_PAD
    for h in range(B_HEADS):
        c0 = h * B_QK_PAD
        q_out[:, c0:c0 + B_NOPE] = (qa[:, c0:c0 + B_NOPE] * scale).astype(BF16)
        pe = qa[:, c0 + B_NOPE:c0 + B_QK_PAD]
        pe_sw = qa[:, sw0 + h * LANE:sw0 + (h + 1) * LANE]
        q_out[:, c0 + B_NOPE:c0 + B_QK_PAD] = ((pe * cos + pe_sw * sin) * scale).astype(BF16)
        k_out[:, c0:c0 + B_NOPE] = kn[:, h * B_NOPE:(h + 1) * B_NOPE].astype(BF16)
        k_out[:, c0 + B_NOPE:c0 + B_QK_PAD] = k_rot


def _mla_prep(bproj, cos, sin, q_norm, kv_norm, wq, wk, wvt, tm=256):
    T = bproj.shape[0]
    scale = (B_NOPE + B_ROPE) ** -0.5 * LOG2_E
    qk_w = B_HEADS * B_QK_PAD
    v_w = B_HEADS * B_VT_ROWS
    row = lambda i: (i, 0)
    fixed = lambda i: (0, 0)
    return pl.pallas_call(
        functools.partial(_mla_prep_kernel, scale=scale),
        grid=(T // tm,),
        in_specs=[
            pl.BlockSpec((tm, bproj.shape[1]), row),
            pl.BlockSpec((tm, LANE), row),
            pl.BlockSpec((tm, LANE), row),
            pl.BlockSpec((1, B_Q_LORA), fixed),
            pl.BlockSpec((1, B_KV_LORA), fixed),
            pl.BlockSpec(wq.shape, fixed),
            pl.BlockSpec(wk.shape, fixed),
            pl.BlockSpec(wvt.shape, fixed),
        ],
        out_specs=[pl.BlockSpec((tm, qk_w), row), pl.BlockSpec((tm, qk_w), row),
                   pl.BlockSpec((v_w, tm), lambda i: (0, i))],
        out_shape=[jax.ShapeDtypeStruct((T, qk_w), BF16), jax.ShapeDtypeStruct((T, qk_w), BF16),
                   jax.ShapeDtypeStruct((v_w, T), BF16)],
        compiler_params=_params("parallel"),
        name="mla_prep",
    )(bproj, cos, sin, q_norm.reshape(1, -1), kv_norm.reshape(1, -1), wq, wk, wvt)


def _mla_attn_kernel(qt_ref, kt_ref, q_ref, k_ref, vt_ref, o_ref, m_sc, acc_sc, *, tq, tk):
    p = pl.program_id(2)
    qi = qt_ref[p]
    kj = kt_ref[p]

    @pl.when(kj == 0)
    def _():
        m_sc[...] = jnp.full_like(m_sc, NEG_INF)
        acc_sc[...] = jnp.zeros_like(acc_sc)

    def update(g, keys, queries, masked):
        qk_cols = slice(g * B_QK_PAD, (g + 1) * B_QK_PAD)
        v_rows = slice(g * B_VT_ROWS, (g + 1) * B_VT_ROWS)
        s = lax.dot_general(k_ref[keys, qk_cols], q_ref[queries, qk_cols], (((1,), (1,)), ((), ())),
                            preferred_element_type=F32)
        if masked:
            kc = (kj * tk + keys.start + lax.broadcasted_iota(I32, s.shape, 0)) // CHUNK
            qc = (qi * tq + queries.start + lax.broadcasted_iota(I32, s.shape, 1)) // CHUNK
            s = jnp.where(kc <= qc, s, NEG_INF)
        m_prev = m_sc[g:g + 1, queries]
        m_new = jnp.maximum(m_prev, jnp.max(s, axis=0, keepdims=True))
        alpha = jnp.exp2(m_prev - m_new)
        pr = jnp.exp2((s - m_new).astype(BF16))
        acc_sc[v_rows, queries] = alpha * acc_sc[v_rows, queries] + jnp.dot(vt_ref[v_rows, keys], pr,
                                                                            preferred_element_type=F32)
        m_sc[g:g + 1, queries] = m_new

    k_end = (kj + 1) * tk
    crosses_diagonal = k_end > qi * tq

    @pl.when(jnp.logical_not(crosses_diagonal))
    def _():
        for g in range(B_HEADS_PER_STEP):
            update(g, slice(0, tk), slice(0, tq), masked=False)

    @pl.when(crosses_diagonal)
    def _():
        half = tk // 2
        for g in range(B_HEADS_PER_STEP):
            update(g, slice(0, half), slice(0, tq), masked=True)
            update(g, slice(half, tk), slice(half, tq), masked=True)

    @pl.when(k_end == (qi + 1) * tq)
    def _():
        for g in range(B_HEADS_PER_STEP):
            r0 = g * B_VT_ROWS
            row_sum = acc_sc[r0 + B_VDIM:r0 + B_VDIM + 1, :]
            o_ref[:, g * B_VDIM:(g + 1) * B_VDIM] = (acc_sc[r0:r0 + B_VDIM, :] / row_sum).T.astype(o_ref.dtype)


def _mla_attention(qp, kp, vt, B, S):
    T = qp.shape[0]
    tq = min(B_TQ, S)
    tk = min(B_TK, S)
    assert tq == tk, "the diagonal-block split in the kernel assumes square blocks"
    nq = S // tq
    nk = S // tk
    hg = B_HEADS_PER_STEP
    pairs = [(qi, kj) for qi in range(nq) for kj in range((qi + 1) * tq // tk)]
    qt = jnp.asarray([p[0] for p in pairs], I32)
    kt = jnp.asarray([p[1] for p in pairs], I32)
    grid_spec = pltpu.PrefetchScalarGridSpec(
        num_scalar_prefetch=2,
        grid=(B, B_HEADS // hg, len(pairs)),
        in_specs=[
            pl.BlockSpec((tq, hg * B_QK_PAD), lambda b, h, p, qt, kt: (b * nq + qt[p], h)),
            pl.BlockSpec((tk, hg * B_QK_PAD), lambda b, h, p, qt, kt: (b * nk + kt[p], h)),
            pl.BlockSpec((hg * B_VT_ROWS, tk), lambda b, h, p, qt, kt: (h, b * nk + kt[p])),
        ],
        out_specs=pl.BlockSpec((tq, hg * B_VDIM), lambda b, h, p, qt, kt: (b * nq + qt[p], h)),
        scratch_shapes=[pltpu.VMEM((hg, tq), F32), pltpu.VMEM((hg * B_VT_ROWS, tq), F32)],
    )
    return pl.pallas_call(
        functools.partial(_mla_attn_kernel, tq=tq, tk=tk),
        grid_spec=grid_spec,
        out_shape=jax.ShapeDtypeStruct((T, B_HEADS * B_VDIM), BF16),
        compiler_params=_params("parallel", "parallel", "arbitrary"),
        name="mla_attention",
    )(qt, kt, qp, kp, vt)


def _mixer_c_kernel(z_ref, lg_ref, lb_ref, ws_ref, bs_ref, o_ref):
    tm = z_ref.shape[0]
    u = z_ref[:, :C_WIDTH].astype(F32)
    v = z_ref[:, C_WIDTH:].astype(F32)
    mu = jnp.mean(v, axis=-1, keepdims=True)
    var = jnp.mean(jnp.square(v - mu), axis=-1, keepdims=True)
    vn = ((v - mu) * lax.rsqrt(var + EPS) * lg_ref[...] + lb_ref[...]).astype(BF16)
    row = lax.broadcasted_iota(I32, (C_BLOCK, C_BLOCK), 0)
    col = lax.broadcasted_iota(I32, (C_BLOCK, C_BLOCK), 1)
    causal = col <= row
    gd = C_WIDTH // C_GROUPS
    for g in range(C_GROUPS):
        w = jnp.where(causal, ws_ref[g], 0.0).astype(BF16)
        b = bs_ref[:, g:g + 1]
        for n in range(tm // C_BLOCK):
            rows = slice(n * C_BLOCK, (n + 1) * C_BLOCK)
            cols = slice(g * gd, (g + 1) * gd)
            mixed = jnp.dot(w, vn[rows, cols], preferred_element_type=F32) + b
            o_ref[rows, cols] = (u[rows, cols] * mixed).astype(o_ref.dtype)


def _mixer_c(z, ln_g, ln_b, w_s, b_s, tm=256):
    T = z.shape[0]
    return pl.pallas_call(
        _mixer_c_kernel,
        grid=(T // tm,),
        in_specs=[
            pl.BlockSpec((tm, 2 * C_WIDTH), lambda i: (i, 0)),
            pl.BlockSpec((1, C_WIDTH), lambda i: (0, 0)),
            pl.BlockSpec((1, C_WIDTH), lambda i: (0, 0)),
            pl.BlockSpec((C_GROUPS, C_BLOCK, C_BLOCK), lambda i: (0, 0, 0)),
            pl.BlockSpec((C_BLOCK, C_GROUPS), lambda i: (0, 0)),
        ],
        out_specs=pl.BlockSpec((tm, C_WIDTH), lambda i: (i, 0)),
        out_shape=jax.ShapeDtypeStruct((T, C_WIDTH), BF16),
        compiler_params=_params("parallel"),
        name="mixer_c",
    )(z, ln_g.reshape(1, -1), ln_b.reshape(1, -1), w_s, b_s.T)


def _branch_kernel(ya_ref, yb_ref, yc_ref, wb_ref, ga_ref, gb_ref, gc_ref, o_ref):
    acc = ga_ref[...].astype(F32) * jnp.dot(ya_ref[...], wb_ref[0], preferred_element_type=F32)
    acc += gb_ref[...].astype(F32) * jnp.dot(yb_ref[...], wb_ref[1], preferred_element_type=F32)
    acc += gc_ref[...].astype(F32) * jnp.dot(yc_ref[...], wb_ref[2], preferred_element_type=F32)
    o_ref[...] = acc.astype(o_ref.dtype)


def _branch_merge(ya, yb, yc, wb, gates, tm=1024, tn=512):
    T, W = ya.shape
    D = wb.shape[2]
    tm = min(tm, T)
    nj = D // tn
    y_spec = pl.BlockSpec((tm, W), lambda i, j: (i, 0))
    return pl.pallas_call(
        _branch_kernel,
        grid=(T // tm, nj),
        in_specs=[
            y_spec, y_spec, y_spec,
            pl.BlockSpec((3, W, tn), lambda i, j: (0, 0, j)),
            pl.BlockSpec((tm, tn), lambda i, j: (i, j)),
            pl.BlockSpec((tm, tn), lambda i, j: (i, nj + j)),
            pl.BlockSpec((tm, tn), lambda i, j: (i, 2 * nj + j)),
        ],
        out_specs=pl.BlockSpec((tm, tn), lambda i, j: (i, j)),
        out_shape=jax.ShapeDtypeStruct((T, D), BF16),
        compiler_params=_params("parallel", "arbitrary"),
        name="branch_merge",
    )(ya, yb, yc, wb, gates, gates, gates)


def _route_kernel(x_ref, g_ref, sc_ref, sh_ref, wrh_ref, wrl_ref, rb_ref,
                  hb_ref, ew_ref, lpos_ref, cnt_ref):
    tm = x_ref.shape[0]
    h = _modulated_norm(x_ref[...], g_ref[...], sc_ref[...], sh_ref[...])
    hb = h.astype(BF16)
    hb_ref[...] = hb
    h_lo = (h - hb.astype(F32)).astype(BF16)
    dn = (((1,), (1,)), ((), ()))
    logits = (lax.dot_general(wrh_ref[...], hb, dn, preferred_element_type=F32)
              + lax.dot_general(wrh_ref[...], h_lo, dn, preferred_element_type=F32)
              + lax.dot_general(wrl_ref[...], hb, dn, preferred_element_type=F32))
    scores = jax.nn.sigmoid(logits)
    sel = scores + rb_ref[...]
    neg = -jnp.inf
    n_groups = N_EXPERTS // EXPERT_GROUP

    i8 = lax.broadcasted_iota(I32, (EXPERT_GROUP, tm), 0)
    group_scores = []
    for g in range(n_groups):
        blk = sel[g * EXPERT_GROUP:(g + 1) * EXPERT_GROUP, :]
        m1 = jnp.max(blk, axis=0, keepdims=True)
        first = jnp.min(jnp.where(blk == m1, i8, EXPERT_GROUP), axis=0, keepdims=True)
        m2 = jnp.max(jnp.where(i8 == first, neg, blk), axis=0, keepdims=True)
        group_scores.append(m1 + m2)
    gs = jnp.concatenate(group_scores, axis=0)

    ig = lax.broadcasted_iota(I32, (n_groups, tm), 0)
    gsel = jnp.zeros((n_groups, tm), F32)
    for _ in range(TOPK_GROUPS):
        m = jnp.max(gs, axis=0, keepdims=True)
        first = jnp.min(jnp.where(gs == m, ig, n_groups), axis=0, keepdims=True)
        hit = ig == first
        gsel = jnp.where(hit, 1.0, gsel)
        gs = jnp.where(hit, neg, gs)

    masked = jnp.concatenate(
        [jnp.where(gsel[g:g + 1, :] > 0.0, sel[g * EXPERT_GROUP:(g + 1) * EXPERT_GROUP, :], neg)
         for g in range(n_groups)], axis=0)

    ie = lax.broadcasted_iota(I32, (N_EXPERTS, tm), 0)
    chosen = jnp.zeros((N_EXPERTS, tm), F32)
    hits, ws = [], []
    for _ in range(TOP_K):
        m = jnp.max(masked, axis=0, keepdims=True)
        first = jnp.min(jnp.where(masked == m, ie, N_EXPERTS), axis=0, keepdims=True)
        hit = ie == first
        ws.append(jnp.sum(jnp.where(hit, scores, 0.0), axis=0, keepdims=True))
        hits.append(hit)
        chosen = jnp.where(hit, 1.0, chosen)
        masked = jnp.where(hit, neg, masked)
    w = jnp.concatenate(ws, axis=0)
    denom = jnp.sum(w, axis=0, keepdims=True) + 1e-20
    ew_ref[...] = w / denom * ROUTED_SCALE

    s_idx = lax.broadcasted_iota(I32, (tm, tm), 0)
    t_idx = lax.broadcasted_iota(I32, (tm, tm), 1)
    earlier = jnp.where(s_idx < t_idx, 1.0, 0.0).astype(BF16)
    local_rank = jnp.dot(chosen.astype(BF16), earlier, preferred_element_type=F32)
    n_e = jnp.sum(chosen, axis=1, keepdims=True)
    run = jnp.floor((n_e + (ROW_ALIGN - 1.0)) * (1.0 / ROW_ALIGN)) * ROW_ALIGN
    run_b = jnp.broadcast_to(run, (N_EXPERTS, LANE))
    e_row = lax.broadcasted_iota(I32, (N_EXPERTS, N_EXPERTS), 0)
    e_col = lax.broadcasted_iota(I32, (N_EXPERTS, N_EXPERTS), 1)
    before = jnp.where(e_col < e_row, 1.0, 0.0).astype(BF16)
    run_start = jnp.dot(before, run_b.astype(BF16), preferred_element_type=F32)
    base = run_start[:, :1] + local_rank
    lpos = [jnp.sum(jnp.where(hit, base, 0.0), axis=0, keepdims=True) for hit in hits]
    lpos_ref[...] = jnp.concatenate(lpos, axis=0).astype(I32)
    cnt_ref[...] = run_b.astype(I32)


def _route(x, g, mod3, row_sc, row_sh, w_router, router_bias, S):
    T, D = x.shape
    tm = min(TOKEN_TILE, S)
    per = S // tm
    wr_t = w_router.T
    wr_hi = wr_t.astype(BF16)
    wr_lo = (wr_t - wr_hi.astype(F32)).astype(BF16)
    row = lambda i: (i, 0)
    col = lambda i: (0, i)
    fixed = lambda i: (0, 0)
    return pl.pallas_call(
        _route_kernel,
        grid=(T // tm,),
        in_specs=[
            pl.BlockSpec((tm, D), row),
            pl.BlockSpec((1, D), fixed),
            pl.BlockSpec((None, 1, D), lambda i: (row_sc(i // per), 0, 0)),
            pl.BlockSpec((None, 1, D), lambda i: (row_sh(i // per), 0, 0)),
            pl.BlockSpec((N_EXPERTS, D), fixed),
            pl.BlockSpec((N_EXPERTS, D), fixed),
            pl.BlockSpec((N_EXPERTS, 1), fixed),
        ],
        out_specs=[
            pl.BlockSpec((tm, D), row),
            pl.BlockSpec((TOP_K, tm), col),
            pl.BlockSpec((TOP_K, tm), col),
            pl.BlockSpec((N_EXPERTS, LANE), col),
        ],
        out_shape=[
            jax.ShapeDtypeStruct((T, D), BF16),
            jax.ShapeDtypeStruct((TOP_K, T), F32),
            jax.ShapeDtypeStruct((TOP_K, T), I32),
            jax.ShapeDtypeStruct((N_EXPERTS, (T // tm) * LANE), I32),
        ],
        compiler_params=_params("parallel"),
        name="norm_route",
    )(x, g.reshape(1, D), mod3, mod3, wr_hi, wr_lo, router_bias.reshape(N_EXPERTS, 1))


def _moe_tables(cnt, n_tiles, local_rows):
    runs = cnt[:, ::LANE].T
    total = jnp.sum(runs, axis=0)
    region = (total + EXPERT_TILE - 1) // EXPERT_TILE * EXPERT_TILE
    region_end = jnp.cumsum(region)
    region_start = region_end - region
    dst = region_start[None, :] + jnp.cumsum(runs, axis=0) - runs
    chunks = runs // ROW_ALIGN
    chunk_end = jnp.cumsum(chunks, axis=1)
    chunk_start = chunk_end - chunks
    c = jnp.arange(local_rows // ROW_ALIGN, dtype=I32)
    run_of = jnp.sum(chunk_end[:, None, :] <= c[None, :, None], axis=2)
    in_run = run_of[:, :, None] == jnp.arange(N_EXPERTS, dtype=I32)[None, None, :]
    chunk_row = dst[:, None, :] + (c[None, :, None] - chunk_start[:, None, :]) * ROW_ALIGN
    chunk_dst = jnp.sum(jnp.where(in_run, chunk_row, 0), axis=2)
    tile_end = region_end // EXPERT_TILE
    tile_expert = jnp.sum(tile_end[None, :] <= jnp.arange(n_tiles, dtype=I32)[:, None], axis=1)
    tile_expert = jnp.minimum(tile_expert, N_EXPERTS - 1).astype(I32)
    as_i32 = lambda a: a.reshape(-1).astype(I32)
    gap_start = jnp.concatenate([region_start + total, region_end[-1:]])
    gap_rows = jnp.concatenate([region - total, n_tiles * EXPERT_TILE - region_end[-1:]])
    return dict(chunk_dst=as_i32(chunk_dst), n_chunks=as_i32(chunk_end[:, -1]),
                gap_start=as_i32(gap_start), gap_chunks=as_i32(gap_rows // ROW_ALIGN),
                tile_expert=tile_expert, n_used=as_i32(tile_end[-1:]))


def _run_copies(i, chunk_dst_ref, n_chunks_ref, make_copy, max_chunks):
    def body(c, carry):
        make_copy(pl.multiple_of(c * ROW_ALIGN, ROW_ALIGN),
                  pl.multiple_of(chunk_dst_ref[i * max_chunks + c], ROW_ALIGN)).start()
        return carry
    lax.fori_loop(0, n_chunks_ref[i], body, 0)


def _wait_rows(n_rows, src, dst, sem, max_rows):
    def wait_for(size):
        pltpu.make_async_copy(src.at[pl.ds(0, size)], dst.at[pl.ds(0, size)], sem).wait()

    if max_rows >= 2 * WAIT_ROWS:
        def body(c, carry):
            wait_for(WAIT_ROWS)
            return carry
        lax.fori_loop(0, n_rows // WAIT_ROWS, body, 0)
        n_rows = n_rows % WAIT_ROWS
        max_rows = WAIT_ROWS - ROW_ALIGN
    size = ROW_ALIGN
    while size <= max_rows:
        @pl.when((n_rows & size) != 0)
        def _(size=size):
            wait_for(size)
        size *= 2


def _dispatch_kernel(chunk_dst_ref, n_chunks_ref, ts_ref, tc_ref, hb_ref, lp_ref, xs_hbm,
                     xloc, pbuf, zrows, sem):
    i = pl.program_id(0)
    n = pl.num_programs(0)
    tt = hb_ref.shape[0]
    slot = i % 2

    @pl.when(i == 0)
    def _():
        zrows[...] = jnp.zeros_like(zrows)
        gap_chunks = 0
        for e in range(N_EXPERTS + 1):
            def body(c, carry, e=e):
                pltpu.make_async_copy(
                    zrows, xs_hbm.at[pl.ds(pl.multiple_of(ts_ref[e] + c * ROW_ALIGN, ROW_ALIGN), ROW_ALIGN)],
                    sem.at[0]).start()
                return carry
            lax.fori_loop(0, tc_ref[e], body, 0)
            gap_chunks = gap_chunks + tc_ref[e]
        _wait_rows(gap_chunks * ROW_ALIGN, xs_hbm, xs_hbm, sem.at[0], xs_hbm.shape[0])

    lp = lp_ref[...]

    def onehot_block(rb, carry):
        r0 = pl.multiple_of(rb * PERM_BLOCK, PERM_BLOCK)
        rows = r0 + lax.broadcasted_iota(I32, (PERM_BLOCK, tt), 0)
        onehot = jnp.zeros((PERM_BLOCK, tt), F32)
        for k in range(TOP_K):
            onehot = jnp.where(rows == lp[k:k + 1, :], 1.0, onehot)
        pbuf[pl.ds(r0, PERM_BLOCK), :] = onehot.astype(BF16)
        return carry
    lax.fori_loop(0, pbuf.shape[0] // PERM_BLOCK, onehot_block, 0)

    perm = pbuf[...]
    for c in range(HALF_D // PERM_COLS):
        lo_cols = slice(c * PERM_COLS, (c + 1) * PERM_COLS)
        hi_cols = slice(HALF_D + c * PERM_COLS, HALF_D + (c + 1) * PERM_COLS)
        lo = jnp.dot(perm, hb_ref[:, lo_cols], preferred_element_type=F32)
        hi = jnp.dot(perm, hb_ref[:, hi_cols], preferred_element_type=F32)
        xloc[slot, :, lo_cols] = _pack_bf16_pair(lo, hi)

    def copies_from(buf_slot):
        def make_copy(s, d):
            return pltpu.make_async_copy(xloc.at[buf_slot, pl.ds(s, ROW_ALIGN)], xs_hbm.at[pl.ds(d, ROW_ALIGN)],
                                         sem.at[buf_slot])
        return make_copy
    local_rows = xloc.shape[1]
    _run_copies(i, chunk_dst_ref, n_chunks_ref, copies_from(slot), local_rows // ROW_ALIGN)

    @pl.when(i > 0)
    def _():
        _wait_rows(n_chunks_ref[i - 1] * ROW_ALIGN, xloc.at[1 - slot], xs_hbm, sem.at[1 - slot], local_rows)

    @pl.when(i == n - 1)
    def _():
        _wait_rows(n_chunks_ref[i] * ROW_ALIGN, xloc.at[slot], xs_hbm, sem.at[slot], local_rows)


def _local_rows(tt):
    rows = TOP_K * tt + N_EXPERTS * ROW_ALIGN
    return (rows + COMBINE_BLOCK - 1) // COMBINE_BLOCK * COMBINE_BLOCK


def _dispatch(hb, lpos, tables, n_rows, S):
    T, D = hb.shape
    tt = min(TOKEN_TILE, S)
    grid_spec = pltpu.PrefetchScalarGridSpec(
        num_scalar_prefetch=4,
        grid=(T // tt,),
        in_specs=[
            pl.BlockSpec((tt, D), lambda i, *_: (i, 0)),
            pl.BlockSpec((TOP_K, tt), lambda i, *_: (0, i)),
        ],
        out_specs=pl.BlockSpec(memory_space=pl.ANY),
        scratch_shapes=[pltpu.VMEM((2, _local_rows(tt), HALF_D), U32), pltpu.VMEM((_local_rows(tt), tt), BF16),
                        pltpu.VMEM((ROW_ALIGN, HALF_D), U32), pltpu.SemaphoreType.DMA((2,))],
    )
    return pl.pallas_call(
        _dispatch_kernel,
        grid_spec=grid_spec,
        out_shape=jax.ShapeDtypeStruct((n_rows, HALF_D), U32),
        compiler_params=_params("arbitrary"),
        name="moe_dispatch",
    )(tables["chunk_dst"], tables["n_chunks"], tables["gap_start"], tables["gap_chunks"], hb, lpos)


def _expert_kernel(te_ref, nu_ref, x_ref, wg_ref, wu_ref, wd_ref, y_ref, wgu_bf, wd_bf):
    i = pl.program_id(0)

    @pl.when(i < nu_ref[0])
    def _():
        @pl.when((i == 0) | (te_ref[i] != te_ref[jnp.maximum(i - 1, 0)]))
        def _():
            wgu_bf[:, :D_EXPERT] = wg_ref[...].astype(BF16)
            wgu_bf[:, D_EXPERT:] = wu_ref[...].astype(BF16)
            wd_bf[...] = wd_ref[...].astype(BF16)

        x_lo, x_hi = _unpack_halves(x_ref[...])
        gu = jnp.dot(x_lo.astype(BF16), wgu_bf[:HALF_D, :], preferred_element_type=F32)
        gu += jnp.dot(x_hi.astype(BF16), wgu_bf[HALF_D:, :], preferred_element_type=F32)
        gate = gu[:, :D_EXPERT]
        act = (gate * jax.nn.sigmoid(gate) * gu[:, D_EXPERT:]).astype(BF16)
        y = jnp.dot(act, wd_bf[...], preferred_element_type=F32)
        y_ref[...] = _pack_halves(y)

    @pl.when(i >= nu_ref[0])
    def _():
        y_ref[...] = jnp.zeros_like(y_ref)


def _experts(xs, tables, we_gate, we_up, we_down, layer):
    tm = EXPERT_TILE
    n_tiles = xs.shape[0] // tm
    used = lambda i, nu: jnp.minimum(i, nu[0] - 1)
    expert_block = lambda i, te, nu: (layer, te[used(i, nu)], 0, 0)
    row_block = lambda i, te, nu: (used(i, nu), 0)
    grid_spec = pltpu.PrefetchScalarGridSpec(
        num_scalar_prefetch=2,
        grid=(n_tiles,),
        in_specs=[
            pl.BlockSpec((tm, HALF_D), row_block),
            pl.BlockSpec((None, None, D_MODEL, D_EXPERT), expert_block),
            pl.BlockSpec((None, None, D_MODEL, D_EXPERT), expert_block),
            pl.BlockSpec((None, None, D_EXPERT, D_MODEL), expert_block),
        ],
        out_specs=pl.BlockSpec((tm, HALF_D), lambda i, te, nu: (i, 0)),
        scratch_shapes=[pltpu.VMEM((D_MODEL, 2 * D_EXPERT), BF16), pltpu.VMEM((D_EXPERT, D_MODEL), BF16)],
    )
    return pl.pallas_call(
        _expert_kernel,
        grid_spec=grid_spec,
        out_shape=jax.ShapeDtypeStruct(xs.shape, U32),
        compiler_params=_params("arbitrary"),
        name="experts",
    )(tables["tile_expert"], tables["n_used"], xs, we_gate, we_up, we_down)


def _shared_expert_kernel(h_ref, wgu_ref, wd_ref, o_ref):
    gu = jnp.dot(h_ref[...], wgu_ref[...], preferred_element_type=F32)
    gate = gu[:, :D_EXPERT]
    act = (gate * jax.nn.sigmoid(gate) * gu[:, D_EXPERT:]).astype(BF16)
    o_ref[...] = jnp.dot(act, wd_ref[...], preferred_element_type=F32).astype(o_ref.dtype)


def _shared_expert(hb, wsgu, wsd, tm=512):
    T, D = hb.shape
    return pl.pallas_call(
        _shared_expert_kernel,
        grid=(T // tm,),
        in_specs=[pl.BlockSpec((tm, D), lambda i: (i, 0)), pl.BlockSpec(wsgu.shape, lambda i: (0, 0)),
                  pl.BlockSpec(wsd.shape, lambda i: (0, 0))],
        out_specs=pl.BlockSpec((tm, D), lambda i: (i, 0)),
        out_shape=jax.ShapeDtypeStruct((T, D), BF16),
        compiler_params=_params("parallel"),
        name="shared_expert",
    )(hb, wsgu, wsd)


def _combine_kernel(chunk_dst_ref, n_chunks_ref, lp_ref, ew_ref, x_ref, sh_ref, g2_ref, fg_ref, ys_hbm,
                    o_ref, yloc, acc, sem, *, final_norm):
    i = pl.program_id(0)
    n = pl.num_programs(0)
    tt = x_ref.shape[0]
    slot = i % 2
    max_chunks = yloc.shape[1] // ROW_ALIGN

    def copies_into(buf_slot):
        def make_copy(s, d):
            return pltpu.make_async_copy(ys_hbm.at[pl.ds(d, ROW_ALIGN)], yloc.at[buf_slot, pl.ds(s, ROW_ALIGN)],
                                         sem.at[buf_slot])
        return make_copy

    @pl.when(i == 0)
    def _():
        yloc[...] = jnp.zeros_like(yloc)
        _run_copies(i, chunk_dst_ref, n_chunks_ref, copies_into(0), max_chunks)

    tile_rows = n_chunks_ref[i] * ROW_ALIGN
    _wait_rows(tile_rows, ys_hbm, yloc.at[slot], sem.at[slot], yloc.shape[1])

    @pl.when(i + 1 < n)
    def _():
        _run_copies(i + 1, chunk_dst_ref, n_chunks_ref, copies_into(1 - slot), max_chunks)

    lp = lp_ref[...]
    ew = ew_ref[...]
    acc[...] = jnp.zeros_like(acc)

    def sum_block(rb, carry):
        r0 = pl.multiple_of(rb * COMBINE_BLOCK, COMBINE_BLOCK)
        rows = r0 + lax.broadcasted_iota(I32, (COMBINE_BLOCK, tt), 0)
        wmat = jnp.zeros((COMBINE_BLOCK, tt), F32)
        for k in range(TOP_K):
            wmat = jnp.where(rows == lp[k:k + 1, :], ew[k:k + 1, :], wmat)
        w_row = jnp.sum(wmat, axis=1, keepdims=True)
        onehot = jnp.where(wmat != 0.0, 1.0, 0.0).astype(BF16)
        y_lo, y_hi = _unpack_halves(yloc[slot, pl.ds(r0, COMBINE_BLOCK), :])
        dn = (((0,), (0,)), ((), ()))
        acc[:HALF_D, :] += lax.dot_general((w_row * y_lo).astype(BF16), onehot, dn, preferred_element_type=F32)
        acc[HALF_D:, :] += lax.dot_general((w_row * y_hi).astype(BF16), onehot, dn, preferred_element_type=F32)
        return carry
    lax.fori_loop(0, pl.cdiv(tile_rows, COMBINE_BLOCK), sum_block, 0)

    out = x_ref[...] + g2_ref[...] * (sh_ref[...].astype(F32) + acc[...].T)
    if final_norm:
        out = out * lax.rsqrt(jnp.mean(out * out, axis=-1, keepdims=True) + EPS) * fg_ref[...]
    o_ref[...] = out


def _combine(x, shared, lpos, ew, mod3, row_g, tables, ys, final_g, final_norm, S):
    T, D = x.shape
    tt = min(TOKEN_TILE, S)
    per = S // tt
    row = lambda i, *_: (i, 0)
    col = lambda i, *_: (0, i)
    grid_spec = pltpu.PrefetchScalarGridSpec(
        num_scalar_prefetch=2,
        grid=(T // tt,),
        in_specs=[
            pl.BlockSpec((TOP_K, tt), col),
            pl.BlockSpec((TOP_K, tt), col),
            pl.BlockSpec((tt, D), row),
            pl.BlockSpec((tt, D), row),
            pl.BlockSpec((None, 1, D), lambda i, *_: (row_g(i // per), 0, 0)),
            pl.BlockSpec((1, D), lambda i, *_: (0, 0)),
            pl.BlockSpec(memory_space=pl.ANY),
        ],
        out_specs=pl.BlockSpec((tt, D), row),
        scratch_shapes=[pltpu.VMEM((2, _local_rows(tt), HALF_D), U32), pltpu.VMEM((D, tt), F32),
                        pltpu.SemaphoreType.DMA((2,))],
    )
    return pl.pallas_call(
        functools.partial(_combine_kernel, final_norm=final_norm),
        grid_spec=grid_spec,
        out_shape=jax.ShapeDtypeStruct((T, D), F32),
        compiler_params=_params("arbitrary"),
        name="moe_combine",
    )(tables["chunk_dst"], tables["n_chunks"], lpos, ew, x, shared, mod3, final_g.reshape(1, D), ys)


def _swap_halves(w):
    half = w.shape[-1] // 2
    return jnp.concatenate([w[..., half:], w[..., :half]], axis=-1)


def _split_w_in(w_in):
    a_cols = 3 * A_WIDTH
    b_cols = B_Q_LORA + B_KV_LORA + B_ROPE
    c_cols = 2 * C_WIDTH
    wa = w_in[:, :a_cols].astype(BF16)
    wb_raw = w_in[:, a_cols:a_cols + b_cols]
    kpe = wb_raw[:, B_Q_LORA + B_KV_LORA:]
    zeros = jnp.zeros_like(kpe)
    wb = jnp.concatenate([wb_raw[:, :B_Q_LORA + B_KV_LORA], kpe, zeros, _swap_halves(kpe), zeros],
                         axis=1).astype(BF16)
    wc = w_in[:, a_cols + b_cols:a_cols + b_cols + c_cols].astype(BF16)
    wg = w_in[:, a_cols + b_cols + c_cols:].astype(BF16)
    return wa, wb, wc, wg


def _mla_weights(w_uq, w_ukv):
    r = w_uq.shape[0]
    wq = w_uq.reshape(r, B_HEADS, B_NOPE + B_ROPE)
    nope, pe = wq[..., :B_NOPE], wq[..., B_NOPE:]
    zeros = jnp.zeros_like(pe)
    main = jnp.concatenate([nope, pe, zeros], axis=-1).reshape(r, B_HEADS * B_QK_PAD)
    swapped = jnp.concatenate([_swap_halves(pe), zeros], axis=-1).reshape(r, B_HEADS * LANE)
    wq_ext = jnp.concatenate([main, swapped], axis=1).astype(BF16)
    rk = w_ukv.shape[0]
    wkv = w_ukv.reshape(rk, B_HEADS, B_NOPE + B_VDIM)
    wk = wkv[..., :B_NOPE].reshape(rk, -1).astype(BF16)
    wvt = wkv[..., B_NOPE:].reshape(rk, -1).T.astype(BF16)
    return wq_ext, wk, wvt


def kernel(x, c, positions, rel_bias, norm_mix, norm_ffn, w_mod, b_mod, w_in, mla_q_norm, mla_w_uq, mla_kv_norm, mla_w_ukv, gmlp_ln_g, gmlp_ln_b, gmlp_w_s, gmlp_b_s, w_branch, w_out, w_router, router_bias, we_gate, we_up, we_down, ws_gate, ws_up, ws_down, final_norm):
    B, S, D = x.shape
    T = B * S
    L = w_mod.shape[0]
    xt = x.reshape(T, D)

    mod = _modulation(c, w_mod, b_mod)
    mod3 = mod.reshape(L * B * 6, 1, D)
    cos, sin = _rope_tables(positions)
    a_bias = _mixer_a_bias(rel_bias)
    n_runs = (T // min(TOKEN_TILE, S)) * N_EXPERTS
    n_tiles = pl.cdiv(T * TOP_K + n_runs * (ROW_ALIGN - 1), EXPERT_TILE) + N_EXPERTS

    for l in range(L):
        row = lambda k: (lambda b, l=l, k=k: (l * B + b) * 6 + k)
        wa, wb, wc, wg = _split_w_in(w_in[l])
        wq_ext, wk_nope, wv_t = _mla_weights(mla_w_uq[l], mla_w_ukv[l])

        h = _norm_mod(xt, norm_mix[l], mod3, row(1), row(0), S)
        qkv = _matmul(h, wa, BF16, name="proj_a")
        bproj = _matmul(h, wb, F32, name="proj_b")
        z = _matmul(h, wc, BF16, act="gelu", name="proj_c")
        gates = _matmul(h, wg, BF16, act="sigmoid", name="proj_gates")

        y_a = _mixer_a(qkv, a_bias, B, S)
        qp, kp, vt = _mla_prep(bproj, cos, sin, mla_q_norm[l], mla_kv_norm[l], wq_ext, wk_nope, wv_t)
        y_b = _mla_attention(qp, kp, vt, B, S)
        y_c = _mixer_c(z, gmlp_ln_g[l], gmlp_ln_b[l], gmlp_w_s[l], gmlp_b_s[l])

        merged = _branch_merge(y_a, y_b, y_c, w_branch[l].astype(BF16), gates)
        xt = _matmul_residual(merged, w_out[l].astype(BF16), xt, mod3, row(2), S)

        hb, ew, lpos, cnt = _route(xt, norm_ffn[l], mod3, row(4), row(3), w_router[l], router_bias[l], S)
        tables = _moe_tables(cnt, n_tiles, _local_rows(min(TOKEN_TILE, S)))
        xs = _dispatch(hb, lpos, tables, n_tiles * EXPERT_TILE, S)
        ys = _experts(xs, tables, we_gate, we_up, we_down, l)
        wsgu = jnp.concatenate([ws_gate[l], ws_up[l]], axis=1).astype(BF16)
        shared = _shared_expert(hb, wsgu, ws_down[l].astype(BF16))
        xt = _combine(xt, shared, lpos, ew, mod3, row(5), tables, ys, final_norm, l == L - 1, S)

    return xt.reshape(B, S, D)
```

```python
import functools

import jax
import jax.numpy as jnp
from jax import lax
from jax.experimental import pallas as pl
from jax.experimental.pallas import tpu as pltpu

F32 = jnp.float32
BF16 = jnp.bfloat16
U32 = jnp.uint32
I32 = jnp.int32

EPS = 1e-6
NEG_INF = -1e30

D_MODEL = 2048
HALF_D = D_MODEL // 2
CHUNK = 64

A_HEADS = 16
A_HEAD_DIM = 64
A_WIDTH = A_HEADS * A_HEAD_DIM
A_LEFT_CHUNKS = 8
MAX_REL = 128
A_QBLOCK = 256
A_KBLOCKS = 3
A_PAIRS_PER_STEP = 8

B_HEADS = 8
B_NOPE = 128
B_ROPE = 64
B_VDIM = 128
B_Q_LORA = 512
B_KV_LORA = 256
B_QK_PAD = 256
B_VT_ROWS = B_VDIM + 16
LOG2_E = 1.4426950408889634
ROPE_THETA = 10000.0
B_TQ = 1024
B_TK = 1024
B_HEADS_PER_STEP = 4

C_BLOCK = 128
C_GROUPS = 8
C_WIDTH = 1024

N_EXPERTS = 64
EXPERT_GROUP = 8
TOP_K = 8
TOPK_GROUPS = 4
D_EXPERT = 512
ROUTED_SCALE = 2.5
EXPERT_TILE = 512
TOKEN_TILE = 256
ROW_ALIGN = 8
WAIT_ROWS = 2048
PERM_BLOCK = 256
PERM_COLS = 256
COMBINE_BLOCK = 512

LANE = 128
V7X_VMEM_LIMIT = 56 * 1024 * 1024


def _params(*sem):
    return pltpu.CompilerParams(dimension_semantics=sem, vmem_limit_bytes=V7X_VMEM_LIMIT)


def _pack_halves(v):
    w = v.shape[1] // 2
    r = v.astype(BF16).astype(F32)
    return _pack_bf16_pair(r[:, :w], r[:, w:])


def _pack_bf16_pair(lo, hi):
    return (lax.bitcast_convert_type(lo, U32) >> 16) | lax.bitcast_convert_type(hi, U32)


def _unpack_halves(p):
    lo = lax.bitcast_convert_type(p << 16, F32)
    hi = lax.bitcast_convert_type(p & jnp.uint32(0xFFFF0000), F32)
    return lo, hi


def _mod_kernel(c_ref, w_ref, b_ref, o_ref):
    c = c_ref[...]
    ca = (c * jax.nn.sigmoid(c)).astype(BF16)
    o_ref[...] = jnp.dot(ca, w_ref[...].astype(BF16), preferred_element_type=F32) + b_ref[...]


def _modulation(c, w_mod, b_mod):
    L, D, N = w_mod.shape
    B = c.shape[0]
    tn = 512
    return pl.pallas_call(
        _mod_kernel,
        grid=(L, N // tn),
        in_specs=[
            pl.BlockSpec((B, D), lambda l, j: (0, 0)),
            pl.BlockSpec((None, D, tn), lambda l, j: (l, 0, j)),
            pl.BlockSpec((None, 1, tn), lambda l, j: (l, 0, j)),
        ],
        out_specs=pl.BlockSpec((None, B, tn), lambda l, j: (l, 0, j)),
        out_shape=jax.ShapeDtypeStruct((L, B, N), F32),
        compiler_params=_params("parallel", "arbitrary"),
        name="modulation",
    )(c, w_mod, b_mod.reshape(L, 1, N))


def _modulated_norm(x, g, sc, sh):
    r = lax.rsqrt(jnp.mean(x * x, axis=-1, keepdims=True) + EPS)
    return (x * r * g) * (1.0 + sc) + sh


def _norm_mod_kernel(x_ref, g_ref, sc_ref, sh_ref, o_ref):
    o_ref[...] = _modulated_norm(x_ref[...], g_ref[...], sc_ref[...], sh_ref[...]).astype(o_ref.dtype)


def _norm_mod(x, g, mod3, row_sc, row_sh, S, tm=1024):
    T, D = x.shape
    per = S // tm
    return pl.pallas_call(
        _norm_mod_kernel,
        grid=(T // tm,),
        in_specs=[
            pl.BlockSpec((tm, D), lambda i: (i, 0)),
            pl.BlockSpec((1, D), lambda i: (0, 0)),
            pl.BlockSpec((None, 1, D), lambda i: (row_sc(i // per), 0, 0)),
            pl.BlockSpec((None, 1, D), lambda i: (row_sh(i // per), 0, 0)),
        ],
        out_specs=pl.BlockSpec((tm, D), lambda i: (i, 0)),
        out_shape=jax.ShapeDtypeStruct((T, D), BF16),
        compiler_params=_params("parallel"),
        name="norm_mod",
    )(x, g.reshape(1, D), mod3, mod3)


def _mm_kernel(a_ref, w_ref, o_ref, *, act):
    acc = jnp.dot(a_ref[...], w_ref[...], preferred_element_type=F32)
    if act == "gelu":
        acc = jax.nn.gelu(acc)
    elif act == "sigmoid":
        acc = jax.nn.sigmoid(acc)
    o_ref[...] = acc.astype(o_ref.dtype)


def _matmul(a, w, out_dtype, act=None, tm=1024, tn=1024, name="matmul"):
    M, K = a.shape
    N = w.shape[1]
    tm = min(tm, M)
    return pl.pallas_call(
        functools.partial(_mm_kernel, act=act),
        grid=(M // tm, N // tn),
        in_specs=[
            pl.BlockSpec((tm, K), lambda i, j: (i, 0)),
            pl.BlockSpec((K, tn), lambda i, j: (0, j)),
        ],
        out_specs=pl.BlockSpec((tm, tn), lambda i, j: (i, j)),
        out_shape=jax.ShapeDtypeStruct((M, N), out_dtype),
        compiler_params=_params("parallel", "arbitrary"),
        name=name,
    )(a, w)


def _mm_residual_kernel(a_ref, w_ref, x_ref, g_ref, o_ref):
    acc = jnp.dot(a_ref[...], w_ref[...], preferred_element_type=F32)
    o_ref[...] = x_ref[...] + g_ref[...] * acc


def _matmul_residual(a, w, x, mod3, row_g, S, tm=1024, tn=1024):
    M, K = a.shape
    N = w.shape[1]
    tm = min(tm, S)
    per = S // tm
    return pl.pallas_call(
        _mm_residual_kernel,
        grid=(M // tm, N // tn),
        in_specs=[
            pl.BlockSpec((tm, K), lambda i, j: (i, 0)),
            pl.BlockSpec((K, tn), lambda i, j: (0, j)),
            pl.BlockSpec((tm, tn), lambda i, j: (i, j)),
            pl.BlockSpec((None, 1, tn), lambda i, j: (row_g(i // per), 0, j)),
        ],
        out_specs=pl.BlockSpec((tm, tn), lambda i, j: (i, j)),
        out_shape=jax.ShapeDtypeStruct((M, N), F32),
        compiler_params=_params("parallel", "arbitrary"),
        name="out_proj_residual",
    )(a, w, x, mod3)


def _mixer_a_kernel(q_ref, k0_ref, k1_ref, k2_ref, v0_ref, v1_ref, v2_ref, bias_ref, o_ref, *, scale):
    i = pl.program_id(2)
    qb = A_QBLOCK
    kw = A_KBLOCKS * qb
    lane = lax.broadcasted_iota(I32, (qb, LANE), 1)

    def attend(mask_before_sequence):
        ones_col = jnp.where(lax.broadcasted_iota(I32, (kw, LANE), 1) == 0, 1.0, 0.0).astype(BF16)
        for pp in range(A_PAIRS_PER_STEP):
            cols = slice(pp * LANE, (pp + 1) * LANE)
            q = q_ref[:, cols]
            k = jnp.concatenate([k0_ref[:, cols], k1_ref[:, cols], k2_ref[:, cols]], axis=0)
            v = jnp.concatenate([v0_ref[:, cols], v1_ref[:, cols], v2_ref[:, cols]], axis=0)
            v_ext = jnp.concatenate([v, ones_col], axis=1)
            outs = []
            for hh in range(2):
                head_lanes = (lane < A_HEAD_DIM) if hh == 0 else (lane >= A_HEAD_DIM)
                qm = jnp.where(head_lanes, q, jnp.zeros_like(q)) * jnp.asarray(scale, q.dtype)
                s = lax.dot_general(qm, k, (((1,), (1,)), ((), ())), preferred_element_type=F32)
                s = s + bias_ref[2 * pp + hh]
                if mask_before_sequence:
                    kpos = (i - (A_KBLOCKS - 1)) * qb + lax.broadcasted_iota(I32, (1, kw), 1)
                    s = jnp.where(kpos >= 0, s, NEG_INF)
                m = jnp.max(s, axis=-1, keepdims=True)
                p = jnp.exp((s - m).astype(BF16))
                o = jnp.dot(p, v_ext, preferred_element_type=F32)
                outs.append(o[:, :LANE] / o[:, LANE:LANE + 1])
            o_ref[:, cols] = jnp.where(lane < A_HEAD_DIM, outs[0], outs[1]).astype(o_ref.dtype)

    @pl.when(i < A_KBLOCKS - 1)
    def _():
        attend(True)

    @pl.when(i >= A_KBLOCKS - 1)
    def _():
        attend(False)


def _mixer_a_bias(rel_bias):
    qb = A_QBLOCK
    kw = A_KBLOCKS * qb
    qi = jnp.arange(qb)[:, None]
    kj = jnp.arange(kw)[None, :]
    qc = qi // CHUNK
    kc = kj // CHUNK
    first = (A_KBLOCKS - 1) * qb // CHUNK - A_LEFT_CHUNKS
    in_band = (kc >= qc + first) & (kc <= qc + first + A_LEFT_CHUNKS)
    period = kw + qb
    m = jnp.arange(period)
    k_minus_q = jnp.where(m < kw, m, m - period)
    d = (A_KBLOCKS - 1) * qb - k_minus_q
    line = rel_bias[:, jnp.clip(d, -MAX_REL, MAX_REL) + MAX_REL].astype(F32)
    heads = rel_bias.shape[0]
    flat = jnp.broadcast_to(line[:, None, :], (heads, qb, period)).reshape(heads, qb * period)
    bias = flat[:, :qb * (period - 1)].reshape(heads, qb, period - 1)[:, :, :kw]
    return jnp.where(in_band[None], bias, NEG_INF)


def _mixer_a(qkv, bias, B, S):
    T = qkv.shape[0]
    qb = A_QBLOCK
    nq = S // qb
    width = A_PAIRS_PER_STEP * LANE
    groups = A_WIDTH // width

    def kv_spec(j, col0):
        return pl.BlockSpec(
            (qb, width), lambda p, b, i: (b * nq + jnp.maximum(i - (A_KBLOCKS - 1) + j, 0), col0 + p))

    return pl.pallas_call(
        functools.partial(_mixer_a_kernel, scale=A_HEAD_DIM ** -0.5),
        grid=(groups, B, nq),
        in_specs=[
            pl.BlockSpec((qb, width), lambda p, b, i: (b * nq + i, p)),
            kv_spec(0, groups), kv_spec(1, groups), kv_spec(2, groups),
            kv_spec(0, 2 * groups), kv_spec(1, 2 * groups), kv_spec(2, 2 * groups),
            pl.BlockSpec((2 * A_PAIRS_PER_STEP, qb, A_KBLOCKS * qb), lambda p, b, i: (p, 0, 0)),
        ],
        out_specs=pl.BlockSpec((qb, width), lambda p, b, i: (b * nq + i, p)),
        out_shape=jax.ShapeDtypeStruct((T, A_WIDTH), BF16),
        compiler_params=_params("parallel", "parallel", "arbitrary"),
        name="mixer_a",
    )(qkv, qkv, qkv, qkv, qkv, qkv, qkv, bias)


def _rope_table_kernel(pos_ref, inv_ref, c_ref, s_ref):
    ang = pos_ref[...].astype(F32) * inv_ref[...]
    lane = lax.broadcasted_iota(I32, ang.shape, 1)
    first_half = (lane % B_ROPE) < (B_ROPE // 2)
    c_ref[...] = jnp.cos(ang)
    sn = jnp.sin(ang)
    s_ref[...] = jnp.where(first_half, -sn, sn)


def _rope_tables(positions, tm=512):
    T = positions.size
    half = B_ROPE // 2
    inv = ROPE_THETA ** (-jnp.arange(half, dtype=F32) / half)
    inv = jnp.tile(inv, LANE // half).reshape(1, LANE)
    return pl.pallas_call(
        _rope_table_kernel,
        grid=(T // tm,),
        in_specs=[pl.BlockSpec((tm, 1), lambda i: (i, 0)), pl.BlockSpec((1, LANE), lambda i: (0, 0))],
        out_specs=[pl.BlockSpec((tm, LANE), lambda i: (i, 0))] * 2,
        out_shape=[jax.ShapeDtypeStruct((T, LANE), F32)] * 2,
        compiler_params=_params("parallel"),
        name="rope_tables",
    )(positions.reshape(T, 1), inv)


def _mla_prep_kernel(bp_ref, c_ref, s_ref, qn_ref, kvn_ref, wq_ref, wk_ref, wvt_ref, q_out, k_out, vt_out, *, scale):
    bp = bp_ref[...]
    cq = bp[:, :B_Q_LORA]
    ckv = bp[:, B_Q_LORA:B_Q_LORA + B_KV_LORA]
    kpe = bp[:, B_Q_LORA + B_KV_LORA:B_Q_LORA + B_KV_LORA + LANE]
    kpe_sw = bp[:, B_Q_LORA + B_KV_LORA + LANE:]
    cqn = cq * lax.rsqrt(jnp.mean(cq * cq, axis=-1, keepdims=True) + EPS) * qn_ref[...]
    ckvn = ckv * lax.rsqrt(jnp.mean(ckv * ckv, axis=-1, keepdims=True) + EPS) * kvn_ref[...]
    qa = jnp.dot(cqn.astype(BF16), wq_ref[...], preferred_element_type=F32)
    ckvb = ckvn.astype(BF16)
    kn = jnp.dot(ckvb, wk_ref[...], preferred_element_type=F32)
    vt = lax.dot_general(wvt_ref[...], ckvb, (((1,), (1,)), ((), ())), preferred_element_type=F32).astype(BF16)
    ones_rows = jnp.ones((B_VT_ROWS - B_VDIM, vt.shape[1]), BF16)
    for h in range(B_HEADS):
        vt_out[h * B_VT_ROWS:h * B_VT_ROWS + B_VDIM, :] = vt[h * B_VDIM:(h + 1) * B_VDIM, :]
        vt_out[h * B_VT_ROWS + B_VDIM:(h + 1) * B_VT_ROWS, :] = ones_rows
    cos = c_ref[...]
    sin = s_ref[...]
    k_rot = (kpe * cos + kpe_sw * sin).astype(BF16)
    sw0 = B_HEADS * B_QK_PAD
    for h in range(B_HEADS):
        c0 = h * B_QK_PAD
        q_out[:, c0:c0 + B_NOPE] = (qa[:, c0:c0 + B_NOPE] * scale).astype(BF16)
        pe = qa[:, c0 + B_NOPE:c0 + B_QK_PAD]
        pe_sw = qa[:, sw0 + h * LANE:sw0 + (h + 1) * LANE]
        q_out[:, c0 + B_NOPE:c0 + B_QK_PAD] = ((pe * cos + pe_sw * sin) * scale).astype(BF16)
        k_out[:, c0:c0 + B_NOPE] = kn[:, h * B_NOPE:(h + 1) * B_NOPE].astype(BF16)
        k_out[:, c0 + B_NOPE:c0 + B_QK_PAD] = k_rot


def _mla_prep(bproj, cos, sin, q_norm, kv_norm, wq, wk, wvt, tm=512):
    T = bproj.shape[0]
    scale = (B_NOPE + B_ROPE) ** -0.5 * LOG2_E
    qk_w = B_HEADS * B_QK_PAD
    v_w = B_HEADS * B_VT_ROWS
    row = lambda i: (i, 0)
    fixed = lambda i: (0, 0)
    return pl.pallas_call(
        functools.partial(_mla_prep_kernel, scale=scale),
        grid=(T // tm,),
        in_specs=[
            pl.BlockSpec((tm, bproj.shape[1]), row),
            pl.BlockSpec((tm, LANE), row),
            pl.BlockSpec((tm, LANE), row),
            pl.BlockSpec((1, B_Q_LORA), fixed),
            pl.BlockSpec((1, B_KV_LORA), fixed),
            pl.BlockSpec(wq.shape, fixed),
            pl.BlockSpec(wk.shape, fixed),
            pl.BlockSpec(wvt.shape, fixed),
        ],
        out_specs=[pl.BlockSpec((tm, qk_w), row), pl.BlockSpec((tm, qk_w), row),
                   pl.BlockSpec((v_w, tm), lambda i: (0, i))],
        out_shape=[jax.ShapeDtypeStruct((T, qk_w), BF16), jax.ShapeDtypeStruct((T, qk_w), BF16),
                   jax.ShapeDtypeStruct((v_w, T), BF16)],
        compiler_params=_params("parallel"),
        name="mla_prep",
    )(bproj, cos, sin, q_norm.reshape(1, -1), kv_norm.reshape(1, -1), wq, wk, wvt)


def _mla_attn_kernel(qt_ref, kt_ref, q_ref, k_ref, vt_ref, o_ref, m_sc, acc_sc, *, tq, tk):
    p = pl.program_id(2)
    qi = qt_ref[p]
    kj = kt_ref[p]

    @pl.when(kj == 0)
    def _():
        m_sc[...] = jnp.full_like(m_sc, NEG_INF)
        acc_sc[...] = jnp.zeros_like(acc_sc)

    def update(g, keys, queries, masked):
        qk_cols = slice(g * B_QK_PAD, (g + 1) * B_QK_PAD)
        v_rows = slice(g * B_VT_ROWS, (g + 1) * B_VT_ROWS)
        s = lax.dot_general(k_ref[keys, qk_cols], q_ref[queries, qk_cols], (((1,), (1,)), ((), ())),
                            preferred_element_type=F32)
        if masked:
            kc = (kj * tk + keys.start + lax.broadcasted_iota(I32, s.shape, 0)) // CHUNK
            qc = (qi * tq + queries.start + lax.broadcasted_iota(I32, s.shape, 1)) // CHUNK
            s = jnp.where(kc <= qc, s, NEG_INF)
        m_prev = m_sc[g:g + 1, queries]
        m_new = jnp.maximum(m_prev, jnp.max(s, axis=0, keepdims=True))
        alpha = jnp.exp2(m_prev - m_new)
        pr = jnp.exp2((s - m_new).astype(BF16))
        acc_sc[v_rows, queries] = alpha * acc_sc[v_rows, queries] + jnp.dot(vt_ref[v_rows, keys], pr,
                                                                            preferred_element_type=F32)
        m_sc[g:g + 1, queries] = m_new

    k_end = (kj + 1) * tk
    crosses_diagonal = k_end > qi * tq

    @pl.when(jnp.logical_not(crosses_diagonal))
    def _():
        for g in range(B_HEADS_PER_STEP):
            update(g, slice(0, tk), slice(0, tq), masked=False)

    @pl.when(crosses_diagonal)
    def _():
        half = tk // 2
        for g in range(B_HEADS_PER_STEP):
            update(g, slice(0, half), slice(0, tq), masked=True)
            update(g, slice(half, tk), slice(half, tq), masked=True)

    @pl.when(k_end == (qi + 1) * tq)
    def _():
        for g in range(B_HEADS_PER_STEP):
            r0 = g * B_VT_ROWS
            row_sum = acc_sc[r0 + B_VDIM:r0 + B_VDIM + 1, :]
            o_ref[:, g * B_VDIM:(g + 1) * B_VDIM] = (acc_sc[r0:r0 + B_VDIM, :] / row_sum).T.astype(o_ref.dtype)


def _mla_attention(qp, kp, vt, B, S):
    T = qp.shape[0]
    tq = min(B_TQ, S)
    tk = min(B_TK, S)
    assert tq == tk, "the diagonal-block split in the kernel assumes square blocks"
    nq = S // tq
    nk = S // tk
    hg = B_HEADS_PER_STEP
    pairs = [(qi, kj) for qi in range(nq) for kj in range((qi + 1) * tq // tk)]
    qt = jnp.asarray([p[0] for p in pairs], I32)
    kt = jnp.asarray([p[1] for p in pairs], I32)
    grid_spec = pltpu.PrefetchScalarGridSpec(
        num_scalar_prefetch=2,
        grid=(B, B_HEADS // hg, len(pairs)),
        in_specs=[
            pl.BlockSpec((tq, hg * B_QK_PAD), lambda b, h, p, qt, kt: (b * nq + qt[p], h)),
            pl.BlockSpec((tk, hg * B_QK_PAD), lambda b, h, p, qt, kt: (b * nk + kt[p], h)),
            pl.BlockSpec((hg * B_VT_ROWS, tk), lambda b, h, p, qt, kt: (h, b * nk + kt[p])),
        ],
        out_specs=pl.BlockSpec((tq, hg * B_VDIM), lambda b, h, p, qt, kt: (b * nq + qt[p], h)),
        scratch_shapes=[pltpu.VMEM((hg, tq), F32), pltpu.VMEM((hg * B_VT_ROWS, tq), F32)],
    )
    return pl.pallas_call(
        functools.partial(_mla_attn_kernel, tq=tq, tk=tk),
        grid_spec=grid_spec,
        out_shape=jax.ShapeDtypeStruct((T, B_HEADS * B_VDIM), BF16),
        compiler_params=_params("parallel", "parallel", "arbitrary"),
        name="mla_attention",
    )(qt, kt, qp, kp, vt)


def _mixer_c_kernel(z_ref, lg_ref, lb_ref, ws_ref, bs_ref, o_ref):
    tm = z_ref.shape[0]
    u = z_ref[:, :C_WIDTH].astype(F32)
    v = z_ref[:, C_WIDTH:].astype(F32)
    mu = jnp.mean(v, axis=-1, keepdims=True)
    var = jnp.mean(jnp.square(v - mu), axis=-1, keepdims=True)
    vn = ((v - mu) * lax.rsqrt(var + EPS) * lg_ref[...] + lb_ref[...]).astype(BF16)
    row = lax.broadcasted_iota(I32, (C_BLOCK, C_BLOCK), 0)
    col = lax.broadcasted_iota(I32, (C_BLOCK, C_BLOCK), 1)
    causal = col <= row
    gd = C_WIDTH // C_GROUPS
    for g in range(C_GROUPS):
        w = jnp.where(causal, ws_ref[g], 0.0).astype(BF16)
        b = bs_ref[:, g:g + 1]
        for n in range(tm // C_BLOCK):
            rows = slice(n * C_BLOCK, (n + 1) * C_BLOCK)
            cols = slice(g * gd, (g + 1) * gd)
            mixed = jnp.dot(w, vn[rows, cols], preferred_element_type=F32) + b
            o_ref[rows, cols] = (u[rows, cols] * mixed).astype(o_ref.dtype)


def _mixer_c(z, ln_g, ln_b, w_s, b_s, tm=512):
    T = z.shape[0]
    return pl.pallas_call(
        _mixer_c_kernel,
        grid=(T // tm,),
        in_specs=[
            pl.BlockSpec((tm, 2 * C_WIDTH), lambda i: (i, 0)),
            pl.BlockSpec((1, C_WIDTH), lambda i: (0, 0)),
            pl.BlockSpec((1, C_WIDTH), lambda i: (0, 0)),
            pl.BlockSpec((C_GROUPS, C_BLOCK, C_BLOCK), lambda i: (0, 0, 0)),
            pl.BlockSpec((C_BLOCK, C_GROUPS), lambda i: (0, 0)),
        ],
        out_specs=pl.BlockSpec((tm, C_WIDTH), lambda i: (i, 0)),
        out_shape=jax.ShapeDtypeStruct((T, C_WIDTH), BF16),
        compiler_params=_params("parallel"),
        name="mixer_c",
    )(z, ln_g.reshape(1, -1), ln_b.reshape(1, -1), w_s, b_s.T)


def _branch_kernel(ya_ref, yb_ref, yc_ref, wb_ref, ga_ref, gb_ref, gc_ref, o_ref):
    acc = ga_ref[...].astype(F32) * jnp.dot(ya_ref[...], wb_ref[0], preferred_element_type=F32)
    acc += gb_ref[...].astype(F32) * jnp.dot(yb_ref[...], wb_ref[1], preferred_element_type=F32)
    acc += gc_ref[...].astype(F32) * jnp.dot(yc_ref[...], wb_ref[2], preferred_element_type=F32)
    o_ref[...] = acc.astype(o_ref.dtype)


def _branch_merge(ya, yb, yc, wb, gates, tm=1024, tn=512):
    T, W = ya.shape
    D = wb.shape[2]
    tm = min(tm, T)
    nj = D // tn
    y_spec = pl.BlockSpec((tm, W), lambda i, j: (i, 0))
    return pl.pallas_call(
        _branch_kernel,
        grid=(T // tm, nj),
        in_specs=[
            y_spec, y_spec, y_spec,
            pl.BlockSpec((3, W, tn), lambda i, j: (0, 0, j)),
            pl.BlockSpec((tm, tn), lambda i, j: (i, j)),
            pl.BlockSpec((tm, tn), lambda i, j: (i, nj + j)),
            pl.BlockSpec((tm, tn), lambda i, j: (i, 2 * nj + j)),
        ],
        out_specs=pl.BlockSpec((tm, tn), lambda i, j: (i, j)),
        out_shape=jax.ShapeDtypeStruct((T, D), BF16),
        compiler_params=_params("parallel", "arbitrary"),
        name="branch_merge",
    )(ya, yb, yc, wb, gates, gates, gates)


def _route_kernel(x_ref, g_ref, sc_ref, sh_ref, wrh_ref, wrl_ref, rb_ref,
                  hb_ref, ew_ref, lpos_ref, cnt_ref):
    tm = x_ref.shape[0]
    h = _modulated_norm(x_ref[...], g_ref[...], sc_ref[...], sh_ref[...])
    hb = h.astype(BF16)
    hb_ref[...] = hb
    h_lo = (h - hb.astype(F32)).astype(BF16)
    dn = (((1,), (1,)), ((), ()))
    logits = (lax.dot_general(wrh_ref[...], hb, dn, preferred_element_type=F32)
              + lax.dot_general(wrh_ref[...], h_lo, dn, preferred_element_type=F32)
              + lax.dot_general(wrl_ref[...], hb, dn, preferred_element_type=F32))
    scores = jax.nn.sigmoid(logits)
    sel = scores + rb_ref[...]
    neg = -jnp.inf
    n_groups = N_EXPERTS // EXPERT_GROUP

    i8 = lax.broadcasted_iota(I32, (EXPERT_GROUP, tm), 0)
    group_scores = []
    for g in range(n_groups):
        blk = sel[g * EXPERT_GROUP:(g + 1) * EXPERT_GROUP, :]
        m1 = jnp.max(blk, axis=0, keepdims=True)
        first = jnp.min(jnp.where(blk == m1, i8, EXPERT_GROUP), axis=0, keepdims=True)
        m2 = jnp.max(jnp.where(i8 == first, neg, blk), axis=0, keepdims=True)
        group_scores.append(m1 + m2)
    gs = jnp.concatenate(group_scores, axis=0)

    ig = lax.broadcasted_iota(I32, (n_groups, tm), 0)
    gsel = jnp.zeros((n_groups, tm), F32)
    for _ in range(TOPK_GROUPS):
        m = jnp.max(gs, axis=0, keepdims=True)
        first = jnp.min(jnp.where(gs == m, ig, n_groups), axis=0, keepdims=True)
        hit = ig == first
        gsel = jnp.where(hit, 1.0, gsel)
        gs = jnp.where(hit, neg, gs)

    masked = jnp.concatenate(
        [jnp.where(gsel[g:g + 1, :] > 0.0, sel[g * EXPERT_GROUP:(g + 1) * EXPERT_GROUP, :], neg)
         for g in range(n_groups)], axis=0)

    ie = lax.broadcasted_iota(I32, (N_EXPERTS, tm), 0)
    chosen = jnp.zeros((N_EXPERTS, tm), F32)
    hits, ws = [], []
    for _ in range(TOP_K):
        m = jnp.max(masked, axis=0, keepdims=True)
        first = jnp.min(jnp.where(masked == m, ie, N_EXPERTS), axis=0, keepdims=True)
        hit = ie == first
        ws.append(jnp.sum(jnp.where(hit, scores, 0.0), axis=0, keepdims=True))
        hits.append(hit)
        chosen = jnp.where(hit, 1.0, chosen)
        masked = jnp.where(hit, neg, masked)
    w = jnp.concatenate(ws, axis=0)
    denom = jnp.sum(w, axis=0, keepdims=True) + 1e-20
    ew_ref[...] = w / denom * ROUTED_SCALE

    s_idx = lax.broadcasted_iota(I32, (tm, tm), 0)
    t_idx = lax.broadcasted_iota(I32, (tm, tm), 1)
    earlier = jnp.where(s_idx < t_idx, 1.0, 0.0).astype(BF16)
    local_rank = jnp.dot(chosen.astype(BF16), earlier, preferred_element_type=F32)
    n_e = jnp.sum(chosen, axis=1, keepdims=True)
    run = jnp.floor((n_e + (ROW_ALIGN - 1.0)) * (1.0 / ROW_ALIGN)) * ROW_ALIGN
    run_b = jnp.broadcast_to(run, (N_EXPERTS, LANE))
    e_row = lax.broadcasted_iota(I32, (N_EXPERTS, N_EXPERTS), 0)
    e_col = lax.broadcasted_iota(I32, (N_EXPERTS, N_EXPERTS), 1)
    before = jnp.where(e_col < e_row, 1.0, 0.0).astype(BF16)
    run_start = jnp.dot(before, run_b.astype(BF16), preferred_element_type=F32)
    base = run_start[:, :1] + local_rank
    lpos = [jnp.sum(jnp.where(hit, base, 0.0), axis=0, keepdims=True) for hit in hits]
    lpos_ref[...] = jnp.concatenate(lpos, axis=0).astype(I32)
    cnt_ref[...] = run_b.astype(I32)


def _route(x, g, mod3, row_sc, row_sh, w_router, router_bias, S):
    T, D = x.shape
    tm = min(TOKEN_TILE, S)
    per = S // tm
    wr_t = w_router.T
    wr_hi = wr_t.astype(BF16)
    wr_lo = (wr_t - wr_hi.astype(F32)).astype(BF16)
    row = lambda i: (i, 0)
    col = lambda i: (0, i)
    fixed = lambda i: (0, 0)
    return pl.pallas_call(
        _route_kernel,
        grid=(T // tm,),
        in_specs=[
            pl.BlockSpec((tm, D), row),
            pl.BlockSpec((1, D), fixed),
            pl.BlockSpec((None, 1, D), lambda i: (row_sc(i // per), 0, 0)),
            pl.BlockSpec((None, 1, D), lambda i: (row_sh(i // per), 0, 0)),
            pl.BlockSpec((N_EXPERTS, D), fixed),
            pl.BlockSpec((N_EXPERTS, D), fixed),
            pl.BlockSpec((N_EXPERTS, 1), fixed),
        ],
        out_specs=[
            pl.BlockSpec((tm, D), row),
            pl.BlockSpec((TOP_K, tm), col),
            pl.BlockSpec((TOP_K, tm), col),
            pl.BlockSpec((N_EXPERTS, LANE), col),
        ],
        out_shape=[
            jax.ShapeDtypeStruct((T, D), BF16),
            jax.ShapeDtypeStruct((TOP_K, T), F32),
            jax.ShapeDtypeStruct((TOP_K, T), I32),
            jax.ShapeDtypeStruct((N_EXPERTS, (T // tm) * LANE), I32),
        ],
        compiler_params=_params("parallel"),
        name="norm_route",
    )(x, g.reshape(1, D), mod3, mod3, wr_hi, wr_lo, router_bias.reshape(N_EXPERTS, 1))


def _moe_tables(cnt, n_tiles, local_rows):
    runs = cnt[:, ::LANE].T
    total = jnp.sum(runs, axis=0)
    region = (total + EXPERT_TILE - 1) // EXPERT_TILE * EXPERT_TILE
    region_end = jnp.cumsum(region)
    region_start = region_end - region
    dst = region_start[None, :] + jnp.cumsum(runs, axis=0) - runs
    chunks = runs // ROW_ALIGN
    chunk_end = jnp.cumsum(chunks, axis=1)
    chunk_start = chunk_end - chunks
    c = jnp.arange(local_rows // ROW_ALIGN, dtype=I32)
    run_of = jnp.sum(chunk_end[:, None, :] <= c[None, :, None], axis=2)
    in_run = run_of[:, :, None] == jnp.arange(N_EXPERTS, dtype=I32)[None, None, :]
    chunk_row = dst[:, None, :] + (c[None, :, None] - chunk_start[:, None, :]) * ROW_ALIGN
    chunk_dst = jnp.sum(jnp.where(in_run, chunk_row, 0), axis=2)
    tile_end = region_end // EXPERT_TILE
    tile_expert = jnp.sum(tile_end[None, :] <= jnp.arange(n_tiles, dtype=I32)[:, None], axis=1)
    tile_expert = jnp.minimum(tile_expert, N_EXPERTS - 1).astype(I32)
    as_i32 = lambda a: a.reshape(-1).astype(I32)
    gap_start = jnp.concatenate([region_start + total, region_end[-1:]])
    gap_rows = jnp.concatenate([region - total, n_tiles * EXPERT_TILE - region_end[-1:]])
    return dict(chunk_dst=as_i32(chunk_dst), n_chunks=as_i32(chunk_end[:, -1]),
                gap_start=as_i32(gap_start), gap_chunks=as_i32(gap_rows // ROW_ALIGN),
                tile_expert=tile_expert, n_used=as_i32(tile_end[-1:]))


def _run_copies(i, chunk_dst_ref, n_chunks_ref, make_copy, max_chunks):
    def body(c, carry):
        make_copy(pl.multiple_of(c * ROW_ALIGN, ROW_ALIGN),
                  pl.multiple_of(chunk_dst_ref[i * max_chunks + c], ROW_ALIGN)).start()
        return carry
    lax.fori_loop(0, n_chunks_ref[i], body, 0)


def _wait_rows(n_rows, src, dst, sem, max_rows):
    def wait_for(size):
        pltpu.make_async_copy(src.at[pl.ds(0, size)], dst.at[pl.ds(0, size)], sem).wait()

    if max_rows >= 2 * WAIT_ROWS:
        def body(c, carry):
            wait_for(WAIT_ROWS)
            return carry
        lax.fori_loop(0, n_rows // WAIT_ROWS, body, 0)
        n_rows = n_rows % WAIT_ROWS
        max_rows = WAIT_ROWS - ROW_ALIGN
    size = ROW_ALIGN
    while size <= max_rows:
        @pl.when((n_rows & size) != 0)
        def _(size=size):
            wait_for(size)
        size *= 2


def _dispatch_kernel(chunk_dst_ref, n_chunks_ref, ts_ref, tc_ref, hb_ref, lp_ref, xs_hbm,
                     xloc, pbuf, zrows, sem):
    i = pl.program_id(0)
    n = pl.num_programs(0)
    tt = hb_ref.shape[0]
    slot = i % 2

    @pl.when(i == 0)
    def _():
        zrows[...] = jnp.zeros_like(zrows)
        gap_chunks = 0
        for e in range(N_EXPERTS + 1):
            def body(c, carry, e=e):
                pltpu.make_async_copy(
                    zrows, xs_hbm.at[pl.ds(pl.multiple_of(ts_ref[e] + c * ROW_ALIGN, ROW_ALIGN), ROW_ALIGN)],
                    sem.at[0]).start()
                return carry
            lax.fori_loop(0, tc_ref[e], body, 0)
            gap_chunks = gap_chunks + tc_ref[e]
        _wait_rows(gap_chunks * ROW_ALIGN, xs_hbm, xs_hbm, sem.at[0], xs_hbm.shape[0])

    lp = lp_ref[...]

    def onehot_block(rb, carry):
        r0 = pl.multiple_of(rb * PERM_BLOCK, PERM_BLOCK)
        rows = r0 + lax.broadcasted_iota(I32, (PERM_BLOCK, tt), 0)
        onehot = jnp.zeros((PERM_BLOCK, tt), F32)
        for k in range(TOP_K):
            onehot = jnp.where(rows == lp[k:k + 1, :], 1.0, onehot)
        pbuf[pl.ds(r0, PERM_BLOCK), :] = onehot.astype(BF16)
        return carry
    lax.fori_loop(0, pbuf.shape[0] // PERM_BLOCK, onehot_block, 0)

    perm = pbuf[...]
    for c in range(HALF_D // PERM_COLS):
        lo_cols = slice(c * PERM_COLS, (c + 1) * PERM_COLS)
        hi_cols = slice(HALF_D + c * PERM_COLS, HALF_D + (c + 1) * PERM_COLS)
        lo = jnp.dot(perm, hb_ref[:, lo_cols], preferred_element_type=F32)
        hi = jnp.dot(perm, hb_ref[:, hi_cols], preferred_element_type=F32)
        xloc[slot, :, lo_cols] = _pack_bf16_pair(lo, hi)

    def copies_from(buf_slot):
        def make_copy(s, d):
            return pltpu.make_async_copy(xloc.at[buf_slot, pl.ds(s, ROW_ALIGN)], xs_hbm.at[pl.ds(d, ROW_ALIGN)],
                                         sem.at[buf_slot])
        return make_copy
    local_rows = xloc.shape[1]
    _run_copies(i, chunk_dst_ref, n_chunks_ref, copies_from(slot), local_rows // ROW_ALIGN)

    @pl.when(i > 0)
    def _():
        _wait_rows(n_chunks_ref[i - 1] * ROW_ALIGN, xloc.at[1 - slot], xs_hbm, sem.at[1 - slot], local_rows)

    @pl.when(i == n - 1)
    def _():
        _wait_rows(n_chunks_ref[i] * ROW_ALIGN, xloc.at[slot], xs_hbm, sem.at[slot], local_rows)


def _local_rows(tt):
    rows = TOP_K * tt + N_EXPERTS * ROW_ALIGN
    return (rows + COMBINE_BLOCK - 1) // COMBINE_BLOCK * COMBINE_BLOCK


def _dispatch(hb, lpos, tables, n_rows, S):
    T, D = hb.shape
    tt = min(TOKEN_TILE, S)
    grid_spec = pltpu.PrefetchScalarGridSpec(
        num_scalar_prefetch=4,
        grid=(T // tt,),
        in_specs=[
            pl.BlockSpec((tt, D), lambda i, *_: (i, 0)),
            pl.BlockSpec((TOP_K, tt), lambda i, *_: (0, i)),
        ],
        out_specs=pl.BlockSpec(memory_space=pl.ANY),
        scratch_shapes=[pltpu.VMEM((2, _local_rows(tt), HALF_D), U32), pltpu.VMEM((_local_rows(tt), tt), BF16),
                        pltpu.VMEM((ROW_ALIGN, HALF_D), U32), pltpu.SemaphoreType.DMA((2,))],
    )
    return pl.pallas_call(
        _dispatch_kernel,
        grid_spec=grid_spec,
        out_shape=jax.ShapeDtypeStruct((n_rows, HALF_D), U32),
        compiler_params=_params("arbitrary"),
        name="moe_dispatch",
    )(tables["chunk_dst"], tables["n_chunks"], tables["gap_start"], tables["gap_chunks"], hb, lpos)


def _expert_kernel(te_ref, nu_ref, x_ref, wg_ref, wu_ref, wd_ref, y_ref, wgu_bf, wd_bf):
    i = pl.program_id(0)

    @pl.when(i < nu_ref[0])
    def _():
        @pl.when((i == 0) | (te_ref[i] != te_ref[jnp.maximum(i - 1, 0)]))
        def _():
            wgu_bf[:, :D_EXPERT] = wg_ref[...].astype(BF16)
            wgu_bf[:, D_EXPERT:] = wu_ref[...].astype(BF16)
            wd_bf[...] = wd_ref[...].astype(BF16)

        x_lo, x_hi = _unpack_halves(x_ref[...])
        gu = jnp.dot(x_lo.astype(BF16), wgu_bf[:HALF_D, :], preferred_element_type=F32)
        gu += jnp.dot(x_hi.astype(BF16), wgu_bf[HALF_D:, :], preferred_element_type=F32)
        gate = gu[:, :D_EXPERT]
        act = (gate * jax.nn.sigmoid(gate) * gu[:, D_EXPERT:]).astype(BF16)
        y = jnp.dot(act, wd_bf[...], preferred_element_type=F32)
        y_ref[...] = _pack_halves(y)

    @pl.when(i >= nu_ref[0])
    def _():
        y_ref[...] = jnp.zeros_like(y_ref)


def _experts(xs, tables, we_gate, we_up, we_down, layer):
    tm = EXPERT_TILE
    n_tiles = xs.shape[0] // tm
    used = lambda i, nu: jnp.minimum(i, nu[0] - 1)
    expert_block = lambda i, te, nu: (layer, te[used(i, nu)], 0, 0)
    row_block = lambda i, te, nu: (used(i, nu), 0)
    grid_spec = pltpu.PrefetchScalarGridSpec(
        num_scalar_prefetch=2,
        grid=(n_tiles,),
        in_specs=[
            pl.BlockSpec((tm, HALF_D), row_block),
            pl.BlockSpec((None, None, D_MODEL, D_EXPERT), expert_block),
            pl.BlockSpec((None, None, D_MODEL, D_EXPERT), expert_block),
            pl.BlockSpec((None, None, D_EXPERT, D_MODEL), expert_block),
        ],
        out_specs=pl.BlockSpec((tm, HALF_D), lambda i, te, nu: (i, 0)),
        scratch_shapes=[pltpu.VMEM((D_MODEL, 2 * D_EXPERT), BF16), pltpu.VMEM((D_EXPERT, D_MODEL), BF16)],
    )
    return pl.pallas_call(
        _expert_kernel,
        grid_spec=grid_spec,
        out_shape=jax.ShapeDtypeStruct(xs.shape, U32),
        compiler_params=_params("arbitrary"),
        name="experts",
    )(tables["tile_expert"], tables["n_used"], xs, we_gate, we_up, we_down)


def _shared_expert_kernel(h_ref, wgu_ref, wd_ref, o_ref):
    gu = jnp.dot(h_ref[...], wgu_ref[...], preferred_element_type=F32)
    gate = gu[:, :D_EXPERT]
    act = (gate * jax.nn.sigmoid(gate) * gu[:, D_EXPERT:]).astype(BF16)
    o_ref[...] = jnp.dot(act, wd_ref[...], preferred_element_type=F32).astype(o_ref.dtype)


def _shared_expert(hb, wsgu, wsd, tm=1024):
    T, D = hb.shape
    return pl.pallas_call(
        _shared_expert_kernel,
        grid=(T // tm,),
        in_specs=[pl.BlockSpec((tm, D), lambda i: (i, 0)), pl.BlockSpec(wsgu.shape, lambda i: (0, 0)),
                  pl.BlockSpec(wsd.shape, lambda i: (0, 0))],
        out_specs=pl.BlockSpec((tm, D), lambda i: (i, 0)),
        out_shape=jax.ShapeDtypeStruct((T, D), BF16),
        compiler_params=_params("parallel"),
        name="shared_expert",
    )(hb, wsgu, wsd)


def _combine_kernel(chunk_dst_ref, n_chunks_ref, lp_ref, ew_ref, x_ref, sh_ref, g2_ref, fg_ref, ys_hbm,
                    o_ref, yloc, acc, sem, *, final_norm):
    i = pl.program_id(0)
    n = pl.num_programs(0)
    tt = x_ref.shape[0]
    slot = i % 2
    max_chunks = yloc.shape[1] // ROW_ALIGN

    def copies_into(buf_slot):
        def make_copy(s, d):
            return pltpu.make_async_copy(ys_hbm.at[pl.ds(d, ROW_ALIGN)], yloc.at[buf_slot, pl.ds(s, ROW_ALIGN)],
                                         sem.at[buf_slot])
        return make_copy

    @pl.when(i == 0)
    def _():
        yloc[...] = jnp.zeros_like(yloc)
        _run_copies(i, chunk_dst_ref, n_chunks_ref, copies_into(0), max_chunks)

    tile_rows = n_chunks_ref[i] * ROW_ALIGN
    _wait_rows(tile_rows, ys_hbm, yloc.at[slot], sem.at[slot], yloc.shape[1])

    @pl.when(i + 1 < n)
    def _():
        _run_copies(i + 1, chunk_dst_ref, n_chunks_ref, copies_into(1 - slot), max_chunks)

    lp = lp_ref[...]
    ew = ew_ref[...]
    acc[...] = jnp.zeros_like(acc)

    def sum_block(rb, carry):
        r0 = pl.multiple_of(rb * COMBINE_BLOCK, COMBINE_BLOCK)
        rows = r0 + lax.broadcasted_iota(I32, (COMBINE_BLOCK, tt), 0)
        wmat = jnp.zeros((COMBINE_BLOCK, tt), F32)
        for k in range(TOP_K):
            wmat = jnp.where(rows == lp[k:k + 1, :], ew[k:k + 1, :], wmat)
        w_row = jnp.sum(wmat, axis=1, keepdims=True)
        onehot = jnp.where(wmat != 0.0, 1.0, 0.0).astype(BF16)
        y_lo, y_hi = _unpack_halves(yloc[slot, pl.ds(r0, COMBINE_BLOCK), :])
        dn = (((0,), (0,)), ((), ()))
        acc[:HALF_D, :] += lax.dot_general((w_row * y_lo).astype(BF16), onehot, dn, preferred_element_type=F32)
        acc[HALF_D:, :] += lax.dot_general((w_row * y_hi).astype(BF16), onehot, dn, preferred_element_type=F32)
        return carry
    lax.fori_loop(0, pl.cdiv(tile_rows, COMBINE_BLOCK), sum_block, 0)

    out = x_ref[...] + g2_ref[...] * (sh_ref[...].astype(F32) + acc[...].T)
    if final_norm:
        out = out * lax.rsqrt(jnp.mean(out * out, axis=-1, keepdims=True) + EPS) * fg_ref[...]
    o_ref[...] = out


def _combine(x, shared, lpos, ew, mod3, row_g, tables, ys, final_g, final_norm, S):
    T, D = x.shape
    tt = min(TOKEN_TILE, S)
    per = S // tt
    row = lambda i, *_: (i, 0)
    col = lambda i, *_: (0, i)
    grid_spec = pltpu.PrefetchScalarGridSpec(
        num_scalar_prefetch=2,
        grid=(T // tt,),
        in_specs=[
            pl.BlockSpec((TOP_K, tt), col),
            pl.BlockSpec((TOP_K, tt), col),
            pl.BlockSpec((tt, D), row),
            pl.BlockSpec((tt, D), row),
            pl.BlockSpec((None, 1, D), lambda i, *_: (row_g(i // per), 0, 0)),
            pl.BlockSpec((1, D), lambda i, *_: (0, 0)),
            pl.BlockSpec(memory_space=pl.ANY),
        ],
        out_specs=pl.BlockSpec((tt, D), row),
        scratch_shapes=[pltpu.VMEM((2, _local_rows(tt), HALF_D), U32), pltpu.VMEM((D, tt), F32),
                        pltpu.SemaphoreType.DMA((2,))],
    )
    return pl.pallas_call(
        functools.partial(_combine_kernel, final_norm=final_norm),
        grid_spec=grid_spec,
        out_shape=jax.ShapeDtypeStruct((T, D), F32),
        compiler_params=_params("arbitrary"),
        name="moe_combine",
    )(tables["chunk_dst"], tables["n_chunks"], lpos, ew, x, shared, mod3, final_g.reshape(1, D), ys)


def _swap_halves(w):
    half = w.shape[-1] // 2
    return jnp.concatenate([w[..., half:], w[..., :half]], axis=-1)


def _split_w_in(w_in):
    a_cols = 3 * A_WIDTH
    b_cols = B_Q_LORA + B_KV_LORA + B_ROPE
    c_cols = 2 * C_WIDTH
    wa = w_in[:, :a_cols].astype(BF16)
    wb_raw = w_in[:, a_cols:a_cols + b_cols]
    kpe = wb_raw[:, B_Q_LORA + B_KV_LORA:]
    zeros = jnp.zeros_like(kpe)
    wb = jnp.concatenate([wb_raw[:, :B_Q_LORA + B_KV_LORA], kpe, zeros, _swap_halves(kpe), zeros],
                         axis=1).astype(BF16)
    wc = w_in[:, a_cols + b_cols:a_cols + b_cols + c_cols].astype(BF16)
    wg = w_in[:, a_cols + b_cols + c_cols:].astype(BF16)
    return wa, wb, wc, wg


def _mla_weights(w_uq, w_ukv):
    r = w_uq.shape[0]
    wq = w_uq.reshape(r, B_HEADS, B_NOPE + B_ROPE)
    nope, pe = wq[..., :B_NOPE], wq[..., B_NOPE:]
    zeros = jnp.zeros_like(pe)
    main = jnp.concatenate([nope, pe, zeros], axis=-1).reshape(r, B_HEADS * B_QK_PAD)
    swapped = jnp.concatenate([_swap_halves(pe), zeros], axis=-1).reshape(r, B_HEADS * LANE)
    wq_ext = jnp.concatenate([main, swapped], axis=1).astype(BF16)
    rk = w_ukv.shape[0]
    wkv = w_ukv.reshape(rk, B_HEADS, B_NOPE + B_VDIM)
    wk = wkv[..., :B_NOPE].reshape(rk, -1).astype(BF16)
    wvt = wkv[..., B_NOPE:].reshape(rk, -1).T.astype(BF16)
    return wq_ext, wk, wvt


def kernel(x, c, positions, rel_bias, norm_mix, norm_ffn, w_mod, b_mod, w_in, mla_q_norm, mla_w_uq, mla_kv_norm, mla_w_ukv, gmlp_ln_g, gmlp_ln_b, gmlp_w_s, gmlp_b_s, w_branch, w_out, w_router, router_bias, we_gate, we_up, we_down, ws_gate, ws_up, ws_down, final_norm):
    B, S, D = x.shape
    T = B * S
    L = w_mod.shape[0]
    xt = x.reshape(T, D)

    mod = _modulation(c, w_mod, b_mod)
    mod3 = mod.reshape(L * B * 6, 1, D)
    cos, sin = _rope_tables(positions)
    a_bias = _mixer_a_bias(rel_bias)
    n_runs = (T // min(TOKEN_TILE, S)) * N_EXPERTS
    n_tiles = pl.cdiv(T * TOP_K + n_runs * (ROW_ALIGN - 1), EXPERT_TILE) + N_EXPERTS

    for l in range(L):
        row = lambda k: (lambda b, l=l, k=k: (l * B + b) * 6 + k)
        wa, wb, wc, wg = _split_w_in(w_in[l])
        wq_ext, wk_nope, wv_t = _mla_weights(mla_w_uq[l], mla_w_ukv[l])

        h = _norm_mod(xt, norm_mix[l], mod3, row(1), row(0), S)
        qkv = _matmul(h, wa, BF16, name="proj_a")
        bproj = _matmul(h, wb, F32, name="proj_b")
        z = _matmul(h, wc, BF16, act="gelu", name="proj_c")
        gates = _matmul(h, wg, BF16, act="sigmoid", name="proj_gates")

        y_a = _mixer_a(qkv, a_bias, B, S)
        qp, kp, vt = _mla_prep(bproj, cos, sin, mla_q_norm[l], mla_kv_norm[l], wq_ext, wk_nope, wv_t)
        y_b = _mla_attention(qp, kp, vt, B, S)
        y_c = _mixer_c(z, gmlp_ln_g[l], gmlp_ln_b[l], gmlp_w_s[l], gmlp_b_s[l])

        merged = _branch_merge(y_a, y_b, y_c, w_branch[l].astype(BF16), gates)
        xt = _matmul_residual(merged, w_out[l].astype(BF16), xt, mod3, row(2), S)

        hb, ew, lpos, cnt = _route(xt, norm_ffn[l], mod3, row(4), row(3), w_router[l], router_bias[l], S)
        tables = _moe_tables(cnt, n_tiles, _local_rows(min(TOKEN_TILE, S)))
        xs = _dispatch(hb, lpos, tables, n_tiles * EXPERT_TILE, S)
        ys = _experts(xs, tables, we_gate, we_up, we_down, l)
        wsgu = jnp.concatenate([ws_gate[l], ws_up[l]], axis=1).astype(BF16)
        shared = _shared_expert(hb, wsgu, ws_down[l].astype(BF16))
        xt = _combine(xt, shared, lpos, ew, mod3, row(5), tables, ys, final_norm, l == L - 1, S)

    return xt.reshape(B, S, D)
```
